```python
import jax, jax.numpy as jnp
from jax import lax
import numpy as np

D_MODEL = 1024
BATCH = 8
SEQ = 8192
DEPTH = 4

CHUNK = 64
N_HEADS = 16
HEAD_DIM = D_MODEL // N_HEADS
D_FF = 4 * D_MODEL
N_MIXERS = 2
LEFT_CHUNKS = 8
BAND = (LEFT_CHUNKS + 1) * CHUNK
MAX_REL = 256
N_REL = 2 * MAX_REL + 1
Q_BLOCK = 128
RMS_EPS = 1e-6
N_SB_LAYERS = (DEPTH + N_MIXERS - 1) // N_MIXERS
N_CA_LAYERS = DEPTH // N_MIXERS

kernel_name = "hybrid_stickbreak_chunkrel_trunk"


def rms_norm(x, gain):
    x32 = x.astype(jnp.float32)
    y = x32 * lax.rsqrt(jnp.mean(x32 * x32, axis=-1, keepdims=True) + RMS_EPS)
    return (y * gain.astype(jnp.float32)).astype(x.dtype)


def split_heads(t):
    b, s, _ = t.shape
    return jnp.transpose(t.reshape(b, s, N_HEADS, HEAD_DIM), (0, 2, 1, 3))


def merge_heads(t):
    b, h, s, d = t.shape
    return jnp.transpose(t, (0, 2, 1, 3)).reshape(b, s, h * d)


def stick_breaking_attention(q, k, v):
    seq = q.shape[2]
    scale = HEAD_DIM ** -0.5
    outs = []
    for qb in range(seq // Q_BLOCK):
        q0 = qb * Q_BLOCK
        kl = q0 + Q_BLOCK
        qblk = q[:, :, q0:kl].astype(jnp.float32)
        kblk = k[:, :, :kl].astype(jnp.float32)
        vblk = v[:, :, :kl].astype(jnp.float32)
        z = jnp.einsum('bhqd,bhkd->bhqk', qblk, kblk) * scale
        t_pos = q0 + jnp.arange(Q_BLOCK)[:, None]
        s_pos = jnp.arange(kl)[None, :]
        valid = s_pos < t_pos
        log_1mb = jnp.where(valid, -jax.nn.softplus(z), 0.0)
        between = lax.cumsum(log_1mb, axis=3, reverse=True) - log_1mb
        log_a = jax.nn.log_sigmoid(z) + between
        a = jnp.where(valid, jnp.exp(log_a), 0.0)
        outs.append(jnp.einsum('bhqk,bhkd->bhqd', a, vblk))
    return jnp.concatenate(outs, axis=2).astype(q.dtype)


def head_rms_norm(t, gain):
    return rms_norm(t, gain)


def chunked_relpos_attention(q, k, v, rel_bias):
    b, h, seq, dh = q.shape
    n_chunks = seq // CHUNK
    pad = LEFT_CHUNKS * CHUNK
    kp = jnp.pad(k, ((0, 0), (0, 0), (pad, 0), (0, 0)))
    vp = jnp.pad(v, ((0, 0), (0, 0), (pad, 0), (0, 0)))
    qi = jnp.arange(CHUNK)[:, None]
    kj = jnp.arange(BAND)[None, :]
    rel_idx = jnp.clip(qi + pad - kj, -MAX_REL, MAX_REL) + MAX_REL
    bias = rel_bias[:, rel_idx].astype(jnp.float32)
    scale = dh ** -0.5

    def one_chunk(c):
        start = c * CHUNK
        qc = lax.dynamic_slice_in_dim(q, start, CHUNK, axis=2).astype(jnp.float32)
        kc = lax.dynamic_slice_in_dim(kp, start, BAND, axis=2).astype(jnp.float32)
        vc = lax.dynamic_slice_in_dim(vp, start, BAND, axis=2).astype(jnp.float32)
        logits = jnp.einsum('bhqd,bhkd->bhqk', qc, kc) * scale + bias[None]
        key_pos = start - pad + jnp.arange(BAND)
        logits = jnp.where((key_pos >= 0)[None, None, None, :], logits, -jnp.inf)
        p = jax.nn.softmax(logits, axis=-1)
        return jnp.einsum('bhqk,bhkd->bhqd', p, vc)

    out = lax.map(one_chunk, jnp.arange(n_chunks))
    out = jnp.transpose(out, (1, 2, 0, 3, 4)).reshape(b, h, seq, dh)
    return out.astype(q.dtype)


def _fwd_setup_inputs(seed: int = 0) -> dict:
    key = jax.random.key(seed)
    ks = jax.random.split(key, 12)
    f32 = jnp.float32
    x = jax.random.normal(ks[0], (BATCH, SEQ, D_MODEL), f32)
    mix_norm = 1.0 + 0.02 * jax.random.normal(ks[1], (DEPTH, D_MODEL), f32)
    w_qkv = jax.random.normal(ks[2], (DEPTH, D_MODEL, 3 * D_MODEL), f32) * D_MODEL ** -0.5
    w_o = jax.random.normal(ks[3], (DEPTH, D_MODEL, D_MODEL), f32) * D_MODEL ** -0.5
    q_norm = 1.0 + 0.02 * jax.random.normal(ks[4], (N_CA_LAYERS, HEAD_DIM), f32)
    k_norm = 1.0 + 0.02 * jax.random.normal(ks[5], (N_CA_LAYERS, HEAD_DIM), f32)
    rel_bias = 0.1 * jax.random.normal(ks[6], (N_CA_LAYERS, N_HEADS, N_REL), f32)
    ffn_norm = 1.0 + 0.02 * jax.random.normal(ks[7], (DEPTH, D_MODEL), f32)
    w_up = jax.random.normal(ks[8], (DEPTH, D_MODEL, D_FF), f32) * D_MODEL ** -0.5
    w_down = jax.random.normal(ks[9], (DEPTH, D_FF, D_MODEL), f32) * D_FF ** -0.5
    return {"x": x, "mix_norm": mix_norm, "w_qkv": w_qkv, "w_o": w_o,
            "q_norm": q_norm, "k_norm": k_norm, "rel_bias": rel_bias,
            "ffn_norm": ffn_norm, "w_up": w_up, "w_down": w_down}


def _fwd_reference(x, mix_norm, w_qkv, w_o, q_norm, k_norm, rel_bias, ffn_norm, w_up, w_down):
    for layer in range(DEPTH):
        h = rms_norm(x, mix_norm[layer])
        qkv = jnp.einsum('bsd,de->bse', h, w_qkv[layer])
        q, k, v = jnp.split(qkv, 3, axis=-1)
        q, k, v = split_heads(q), split_heads(k), split_heads(v)
        if layer % N_MIXERS == 0:
            o = stick_breaking_attention(q, k, v)
        else:
            idx = layer // N_MIXERS
            q = head_rms_norm(q, q_norm[idx])
            k = head_rms_norm(k, k_norm[idx])
            o = chunked_relpos_attention(q, k, v, rel_bias[idx])
        x = x + jnp.einsum('bsd,de->bse', merge_heads(o), w_o[layer])
        h = rms_norm(x, ffn_norm[layer])
        u = jnp.square(jax.nn.relu(jnp.einsum('bsd,df->bsf', h, w_up[layer])))
        x = x + jnp.einsum('bsf,fd->bsd', u, w_down[layer])
    return x


import jax as _jax
import jax.numpy as _jnp

TWIN_FORMAT = 'train_step'
FWD_PARAMS = ['x', 'mix_norm', 'w_qkv', 'w_o', 'q_norm', 'k_norm', 'rel_bias', 'ffn_norm', 'w_up', 'w_down']
TWIN_WEIGHTS = ['mix_norm', 'w_qkv', 'w_o', 'q_norm', 'k_norm', 'rel_bias', 'ffn_norm', 'w_up', 'w_down']
TWIN_DIFF_INPUT = 'x'
TWIN_INPUTS = ['x', 'mix_norm', 'w_qkv', 'w_o', 'q_norm', 'k_norm', 'rel_bias', 'ffn_norm', 'w_up', 'w_down', 'loss_target', 'm_mix_norm', 'm_w_qkv', 'm_w_o', 'm_q_norm', 'm_k_norm', 'm_rel_bias', 'm_ffn_norm', 'm_w_up', 'm_w_down', 'v_mix_norm', 'v_w_qkv', 'v_w_o', 'v_q_norm', 'v_k_norm', 'v_rel_bias', 'v_ffn_norm', 'v_w_up', 'v_w_down']
TWIN_OUTPUTS = ['loss', 'grad_x', 'grad_mix_norm', 'grad_w_qkv', 'grad_w_o', 'grad_q_norm', 'grad_k_norm', 'grad_rel_bias', 'grad_ffn_norm', 'grad_w_up', 'grad_w_down', 'delta_mix_norm', 'delta_w_qkv', 'delta_w_o', 'delta_q_norm', 'delta_k_norm', 'delta_rel_bias', 'delta_ffn_norm', 'delta_w_up', 'delta_w_down', 'new_m_mix_norm', 'new_m_w_qkv', 'new_m_w_o', 'new_m_q_norm', 'new_m_k_norm', 'new_m_rel_bias', 'new_m_ffn_norm', 'new_m_w_up', 'new_m_w_down', 'new_v_mix_norm', 'new_v_w_qkv', 'new_v_w_o', 'new_v_q_norm', 'new_v_k_norm', 'new_v_rel_bias', 'new_v_ffn_norm', 'new_v_w_up', 'new_v_w_down']
TWIN_LEAF_KINDS = {'loss': 'loss', 'grad_x': 'grad_x', 'grad_mix_norm': 'grad_w', 'grad_w_qkv': 'grad_w', 'grad_w_o': 'grad_w', 'grad_q_norm': 'grad_w', 'grad_k_norm': 'grad_w', 'grad_rel_bias': 'grad_w', 'grad_ffn_norm': 'grad_w', 'grad_w_up': 'grad_w', 'grad_w_down': 'grad_w', 'delta_mix_norm': 'delta_w', 'delta_w_qkv': 'delta_w', 'delta_w_o': 'delta_w', 'delta_q_norm': 'delta_w', 'delta_k_norm': 'delta_w', 'delta_rel_bias': 'delta_w', 'delta_ffn_norm': 'delta_w', 'delta_w_up': 'delta_w', 'delta_w_down': 'delta_w', 'new_m_mix_norm': 'new_m', 'new_m_w_qkv': 'new_m', 'new_m_w_o': 'new_m', 'new_m_q_norm': 'new_m', 'new_m_k_norm': 'new_m', 'new_m_rel_bias': 'new_m', 'new_m_ffn_norm': 'new_m', 'new_m_w_up': 'new_m', 'new_m_w_down': 'new_m', 'new_v_mix_norm': 'new_v', 'new_v_w_qkv': 'new_v', 'new_v_w_o': 'new_v', 'new_v_q_norm': 'new_v', 'new_v_k_norm': 'new_v', 'new_v_rel_bias': 'new_v', 'new_v_ffn_norm': 'new_v', 'new_v_w_up': 'new_v', 'new_v_w_down': 'new_v'}


def _forward(args):
    return _fwd_reference(*[args[k] for k in FWD_PARAMS])


def _output_shape():
    def fwd():
        inp = _fwd_setup_inputs(0)
        return _fwd_reference(*[inp[k] for k in FWD_PARAMS])
    out = _jax.eval_shape(fwd)
    return out.shape, out.dtype

N_MICROBATCH = 1
ADAM_LR = 0.001
ADAM_B1 = 0.9
ADAM_B2 = 0.999
ADAM_EPS = 1e-08
ADAM_WD = 0.01
ADAM_STEP = 10
PER_EXAMPLE_BATCH_AXIS = {'x': 0, 'loss_target': 0}
SHARED_INPUTS = []
_WEIGHT_DTYPES = {'mix_norm': _jnp.float32, 'w_qkv': _jnp.float32, 'w_o': _jnp.float32, 'q_norm': _jnp.float32, 'k_norm': _jnp.float32, 'rel_bias': _jnp.float32, 'ffn_norm': _jnp.float32, 'w_up': _jnp.float32, 'w_down': _jnp.float32}
MOMENT_SCALE = {'mix_norm': 4.695413e+01, 'w_qkv': 2.284647e+01, 'w_o': 3.908546e+01, 'q_norm': 3.986651e+00, 'k_norm': 3.977476e+00, 'rel_bias': 6.494662e-02, 'ffn_norm': 2.048395e+02, 'w_up': 2.073549e+01, 'w_down': 7.557059e+01}


def _to_microbatches(a, axis):
    t = _jnp.moveaxis(a, axis, 0)
    t = t.reshape((N_MICROBATCH, t.shape[0] // N_MICROBATCH) + t.shape[1:])
    return _jnp.moveaxis(t, 1, axis + 1)


def setup_inputs(seed: int = 0) -> dict:
    inp = _fwd_setup_inputs(seed)
    key = _jax.random.fold_in(_jax.random.key(seed), 7919)
    shape, _ = _output_shape()
    out = dict(inp)
    out["loss_target"] = _jax.random.normal(_jax.random.fold_in(key, 0), shape, _jnp.float32)
    for i, name in enumerate(TWIN_WEIGHTS):
        w = inp[name].astype(_jnp.float32)
        if MOMENT_SCALE is None:
            s = _jnp.sqrt(_jnp.mean(_jnp.square(w)) + 1e-30)
        else:
            s = MOMENT_SCALE[name]
        km, kv = _jax.random.split(_jax.random.fold_in(key, i + 1))
        out[name] = w
        out["m_" + name] = s * _jax.random.normal(km, w.shape, _jnp.float32)
        out["v_" + name] = (s * s) * _jax.random.uniform(kv, w.shape, _jnp.float32, 0.5, 1.5)
    if N_MICROBATCH > 1:
        for name, axis in PER_EXAMPLE_BATCH_AXIS.items():
            out[name] = _to_microbatches(out[name], axis)
    return {'x': out['x'], 'mix_norm': out['mix_norm'], 'w_qkv': out['w_qkv'], 'w_o': out['w_o'], 'q_norm': out['q_norm'], 'k_norm': out['k_norm'], 'rel_bias': out['rel_bias'], 'ffn_norm': out['ffn_norm'], 'w_up': out['w_up'], 'w_down': out['w_down'], 'loss_target': out['loss_target'], 'm_mix_norm': out['m_mix_norm'], 'm_w_qkv': out['m_w_qkv'], 'm_w_o': out['m_w_o'], 'm_q_norm': out['m_q_norm'], 'm_k_norm': out['m_k_norm'], 'm_rel_bias': out['m_rel_bias'], 'm_ffn_norm': out['m_ffn_norm'], 'm_w_up': out['m_w_up'], 'm_w_down': out['m_w_down'], 'v_mix_norm': out['v_mix_norm'], 'v_w_qkv': out['v_w_qkv'], 'v_w_o': out['v_w_o'], 'v_q_norm': out['v_q_norm'], 'v_k_norm': out['v_k_norm'], 'v_rel_bias': out['v_rel_bias'], 'v_ffn_norm': out['v_ffn_norm'], 'v_w_up': out['v_w_up'], 'v_w_down': out['v_w_down']}


def _loss(weights, diff, rest, loss_target):
    with _jax.named_scope("forward"):
        args = {**rest, TWIN_DIFF_INPUT: diff, **{k: w.astype(_WEIGHT_DTYPES[k]) for k, w in weights.items()}}
        y = _forward(args)
    with _jax.named_scope("loss_head"):
        err = _jnp.square(y.astype(_jnp.float32) - loss_target)
        return 0.5 * _jnp.sum(_jnp.mean(err, axis=-1)) if err.ndim else 0.5 * err


def _adamw(w, g, m, v):
    m = ADAM_B1 * m + (1.0 - ADAM_B1) * g
    v = ADAM_B2 * v + (1.0 - ADAM_B2) * _jnp.square(g)
    m_hat = m / (1.0 - ADAM_B1 ** ADAM_STEP)
    v_hat = v / (1.0 - ADAM_B2 ** ADAM_STEP)
    delta = -ADAM_LR * (m_hat / (_jnp.sqrt(v_hat) + ADAM_EPS) + ADAM_WD * w)
    return delta, m, v


def reference(x, mix_norm, w_qkv, w_o, q_norm, k_norm, rel_bias, ffn_norm, w_up, w_down, loss_target, m_mix_norm, m_w_qkv, m_w_o, m_q_norm, m_k_norm, m_rel_bias, m_ffn_norm, m_w_up, m_w_down, v_mix_norm, v_w_qkv, v_w_o, v_q_norm, v_k_norm, v_rel_bias, v_ffn_norm, v_w_up, v_w_down):
    given = dict(x=x, mix_norm=mix_norm, w_qkv=w_qkv, w_o=w_o, q_norm=q_norm, k_norm=k_norm, rel_bias=rel_bias, ffn_norm=ffn_norm, w_up=w_up, w_down=w_down, loss_target=loss_target, m_mix_norm=m_mix_norm, m_w_qkv=m_w_qkv, m_w_o=m_w_o, m_q_norm=m_q_norm, m_k_norm=m_k_norm, m_rel_bias=m_rel_bias, m_ffn_norm=m_ffn_norm, m_w_up=m_w_up, m_w_down=m_w_down, v_mix_norm=v_mix_norm, v_w_qkv=v_w_qkv, v_w_o=v_w_o, v_q_norm=v_q_norm, v_k_norm=v_k_norm, v_rel_bias=v_rel_bias, v_ffn_norm=v_ffn_norm, v_w_up=v_w_up, v_w_down=v_w_down)
    weights = {n: given[n] for n in TWIN_WEIGHTS}
    shared = {n: given[n] for n in SHARED_INPUTS}
    per_example = {n: given[n] for n in ['x']}
    grad_fn = _jax.value_and_grad(_loss, argnums=(0, 1))

    def one_microbatch(ex, loss_target):
        ex = dict(ex)
        diff = ex.pop(TWIN_DIFF_INPUT)
        return grad_fn(weights, diff, {**shared, **ex}, loss_target)

    if N_MICROBATCH == 1:
        loss, (grad_w, grad_x) = one_microbatch(per_example, given["loss_target"])
    else:
        def body(carry, xs):
            loss_sum, grad_sum = carry
            l_k, (gw_k, gx_k) = one_microbatch(xs[0], xs[1])
            with _jax.named_scope("update"):
                return (loss_sum + l_k, _jax.tree.map(_jnp.add, grad_sum, gw_k)), gx_k

        init = (_jnp.zeros((), _jnp.float32), _jax.tree.map(_jnp.zeros_like, weights))
        (loss, grad_w), grad_x = _jax.lax.scan(body, init, (per_example, given["loss_target"]))
    with _jax.named_scope("update"):
        delta_w, new_m, new_v = {}, {}, {}
        for n in TWIN_WEIGHTS:
            delta_w[n], new_m[n], new_v[n] = _adamw(weights[n], grad_w[n], given["m_" + n], given["v_" + n])
    return (loss, grad_x, *[grad_w[n] for n in TWIN_WEIGHTS], *[delta_w[n] for n in TWIN_WEIGHTS],
            *[new_m[n] for n in TWIN_WEIGHTS], *[new_v[n] for n in TWIN_WEIGHTS])
```

```python
import jax
import jax.numpy as jnp
from jax import lax
from jax.experimental import pallas as pl
from jax.experimental.pallas import tpu as pltpu

F32 = jnp.float32
BF16 = jnp.bfloat16
MESH = pl.DeviceIdType.MESH

D_MODEL = 1024
N_HEADS = 16
HEAD_DIM = 64
PAIR = 2 * HEAD_DIM
N_PAIRS = N_HEADS // 2
D_FF = 4 * D_MODEL
DEPTH = 4
N_DEV = 8
RMS_EPS = 1e-6
QK_SCALE = HEAD_DIM ** -0.5

SB_TILE = 256

CHUNK = 64
LEFT_CHUNKS = 8
CA_QBLK = 256
CA_PAD = LEFT_CHUNKS * CHUNK
CA_WIN = CA_QBLK + CA_PAD
CA_RING = 1024
MAX_REL = 256
N_REL = 2 * MAX_REL + 1
N_REL_PAD = 640
NEG_BIG = -1e30

ADAM_LR = 0.001
ADAM_B1 = 0.9
ADAM_B2 = 0.999
ADAM_EPS = 1e-08
ADAM_WD = 0.01
ADAM_STEP = 10

ROW_TILE = 512
VMEM_LIMIT = 56 * 1024 * 1024

OFF_MIX = 0
OFF_FFN = OFF_MIX + DEPTH * D_MODEL
OFF_QN = OFF_FFN + DEPTH * D_MODEL
OFF_KN = OFF_QN + 2 * HEAD_DIM
OFF_REL = OFF_KN + 2 * HEAD_DIM
OFF_LOSS = OFF_REL + 2 * N_HEADS * N_REL
PACK_ROWS = 200
PACK_LEN = PACK_ROWS * 128


def _call(body, **kw):
    return pl.pallas_call(body, **kw)


def _params(sem=None, vmem=VMEM_LIMIT):
    if sem is None:
        return pltpu.CompilerParams(vmem_limit_bytes=vmem)
    return pltpu.CompilerParams(dimension_semantics=sem, vmem_limit_bytes=vmem)


def _nt(a, b):
    return lax.dot_general(a, b, (((1,), (1,)), ((), ())), preferred_element_type=F32)


def _tn(a, b):
    return lax.dot_general(a, b, (((0,), (0,)), ((), ())), preferred_element_type=F32)


def _nn(a, b):
    return jnp.dot(a, b, preferred_element_type=F32)


def rms_fwd(x, g, name):
    s = x.shape[0]
    tm = ROW_TILE

    def body(x_ref, g_ref, o_ref):
        xv = x_ref[...]
        r = lax.rsqrt(jnp.mean(xv * xv, axis=-1, keepdims=True) + RMS_EPS)
        o_ref[...] = (xv * r * g_ref[...]).astype(o_ref.dtype)

    return _call(
        body, name=name, grid=(s // tm,),
        in_specs=[pl.BlockSpec((tm, D_MODEL), lambda i: (i, 0)), pl.BlockSpec((1, D_MODEL), lambda i: (0, 0))],
        out_specs=pl.BlockSpec((tm, D_MODEL), lambda i: (i, 0)),
        out_shape=jax.ShapeDtypeStruct((s, D_MODEL), BF16),
        compiler_params=_params(("parallel",)),
    )(x, g)


def rms_bwd(dh, x, g, dres, name):
    s = x.shape[0]
    tm = ROW_TILE

    def body(dh_ref, x_ref, g_ref, dres_ref, dx_ref, dg_ref):
        i = pl.program_id(0)
        xv = x_ref[...]
        dhv = dh_ref[...].astype(F32)
        r = lax.rsqrt(jnp.mean(xv * xv, axis=-1, keepdims=True) + RMS_EPS)
        xh = xv * r
        dy = dhv * g_ref[...]
        mdot = jnp.mean(dy * xh, axis=-1, keepdims=True)
        dx_ref[...] = dres_ref[...] + r * (dy - xh * mdot)

        @pl.when(i == 0)
        def _():
            dg_ref[...] = jnp.zeros_like(dg_ref)

        dg_ref[0:1, :] += jnp.sum(dhv * xh, axis=0, keepdims=True)

    row = pl.BlockSpec((tm, D_MODEL), lambda i: (i, 0))
    dx, dg = _call(
        body, name=name, grid=(s // tm,),
        in_specs=[row, row, pl.BlockSpec((1, D_MODEL), lambda i: (0, 0)), row],
        out_specs=[row, pl.BlockSpec((8, D_MODEL), lambda i: (0, 0))],
        out_shape=[jax.ShapeDtypeStruct((s, D_MODEL), F32), jax.ShapeDtypeStruct((8, D_MODEL), F32)],
        compiler_params=_params(("arbitrary",)),
    )(dh, x, g, dres)
    return dx, dg[0:1]


def loss_head(y, target, name):
    s = y.shape[0]
    tm = ROW_TILE

    def body(y_ref, t_ref, dy_ref, l_ref):
        i = pl.program_id(0)
        e = y_ref[...] - t_ref[...]
        dy_ref[...] = e * (1.0 / D_MODEL)

        @pl.when(i == 0)
        def _():
            l_ref[...] = jnp.zeros_like(l_ref)

        per_row = jnp.sum(e * e, axis=-1, keepdims=True) * (1.0 / D_MODEL)
        l_ref[...] += jnp.broadcast_to(0.5 * jnp.sum(per_row, axis=0, keepdims=True), l_ref.shape)

    row = pl.BlockSpec((tm, D_MODEL), lambda i: (i, 0))
    return _call(
        body, name=name, grid=(s // tm,),
        in_specs=[row, row],
        out_specs=[row, pl.BlockSpec((8, 128), lambda i: (0, 0))],
        out_shape=[jax.ShapeDtypeStruct((s, D_MODEL), F32), jax.ShapeDtypeStruct((8, 128), F32)],
        compiler_params=_params(("arbitrary",)),
    )(y, target)


def _group_sum_matrix():
    r = lax.broadcasted_iota(jnp.int32, (PAIR, PAIR), 0) // HEAD_DIM
    c = lax.broadcasted_iota(jnp.int32, (PAIR, PAIR), 1) // HEAD_DIM
    return (r == c).astype(F32)


def _head_mean(v, gmat):
    return jnp.dot(v, gmat, preferred_element_type=F32, precision=lax.Precision.HIGHEST) * (1.0 / HEAD_DIM)


def headnorm_fwd(qkv, gq, gk, name):
    s = qkv.shape[0]
    tm = ROW_TILE
    ncol = 3 * N_PAIRS

    def body(x_ref, gq_ref, gk_ref, o_ref):
        j = pl.program_id(1)
        xv = x_ref[...].astype(F32)
        gmat = _group_sum_matrix()
        r = lax.rsqrt(_head_mean(xv * xv, gmat) + RMS_EPS)
        gain = jnp.where(j < N_PAIRS, gq_ref[...], gk_ref[...])
        normed = xv * r * gain
        o_ref[...] = jnp.where(j < 2 * N_PAIRS, normed, xv).astype(o_ref.dtype)

    blk = pl.BlockSpec((tm, PAIR), lambda i, j: (i, j))
    gspec = pl.BlockSpec((1, PAIR), lambda i, j: (0, 0))
    return _call(
        body, name=name, grid=(s // tm, ncol),
        in_specs=[blk, gspec, gspec], out_specs=blk,
        out_shape=jax.ShapeDtypeStruct(qkv.shape, BF16),
        compiler_params=_params(("parallel", "parallel")),
    )(qkv, gq, gk)


def headnorm_bwd(dn, qkv, gq, gk, name):
    s = qkv.shape[0]
    tm = ROW_TILE
    ncol = 3 * N_PAIRS

    def body(dn_ref, x_ref, gq_ref, gk_ref, dx_ref, dg_ref):
        i = pl.program_id(0)
        j = pl.program_id(1)
        xv = x_ref[...].astype(F32)
        dv = dn_ref[...].astype(F32)
        gmat = _group_sum_matrix()
        r = lax.rsqrt(_head_mean(xv * xv, gmat) + RMS_EPS)
        xh = xv * r
        gain = jnp.where(j < N_PAIRS, gq_ref[...], gk_ref[...])
        dy = dv * gain
        mdot = _head_mean(dy * xh, gmat)
        dx = r * (dy - xh * mdot)
        dx_ref[...] = jnp.where(j < 2 * N_PAIRS, dx, dv).astype(dx_ref.dtype)

        @pl.when((i == 0) & (j == 0))
        def _():
            dg_ref[...] = jnp.zeros_like(dg_ref)

        part = jnp.sum(dv * xh, axis=0, keepdims=True)

        @pl.when(j < N_PAIRS)
        def _():
            dg_ref[0:1, :] += part

        @pl.when((j >= N_PAIRS) & (j < 2 * N_PAIRS))
        def _():
            dg_ref[1:2, :] += part

    blk = pl.BlockSpec((tm, PAIR), lambda i, j: (i, j))
    gspec = pl.BlockSpec((1, PAIR), lambda i, j: (0, 0))
    dx, dg = _call(
        body, name=name, grid=(s // tm, ncol),
        in_specs=[blk, blk, gspec, gspec],
        out_specs=[blk, pl.BlockSpec((8, PAIR), lambda i, j: (0, 0))],
        out_shape=[jax.ShapeDtypeStruct(qkv.shape, BF16), jax.ShapeDtypeStruct((8, PAIR), F32)],
        compiler_params=_params(("arbitrary", "arbitrary")),
    )(dn, qkv, gq, gk)
    return dx, dg[0:2]


def _relu2(a):
    r = jnp.maximum(a.astype(F32), 0.0)
    return r * r


def dense_rowblock(a, w3, *, nt, name, out_dtype, relu2_in=False, relu2_grad=None, res=None):
    s, k = a.shape
    nb = w3.shape[0]
    bn = w3.shape[1] if nt else w3.shape[2]
    tm = ROW_TILE

    def body(*refs):
        a_ref, w_ref = refs[0], refs[1]
        o_ref = refs[-1]
        extra = list(refs[2:-1])
        av = a_ref[...]
        av = _relu2(av).astype(BF16) if relu2_in else av.astype(BF16)
        acc = _nt(av, w_ref[...]) if nt else _nn(av, w_ref[...])
        if relu2_grad is not None:
            acc = acc * (2.0 * jnp.maximum(extra.pop(0)[...].astype(F32), 0.0))
        if res is not None:
            acc = acc + extra.pop(0)[...]
        o_ref[...] = acc.astype(o_ref.dtype)

    wblk = (None, bn, k) if nt else (None, k, bn)
    oblk = pl.BlockSpec((tm, bn), lambda i, j: (i, j))
    in_specs = [pl.BlockSpec((tm, k), lambda i, j: (i, 0)), pl.BlockSpec(wblk, lambda i, j: (j, 0, 0))]
    args = [a, w3]
    for e in (relu2_grad, res):
        if e is not None:
            in_specs.append(oblk)
            args.append(e)
    return _call(
        body, name=name, grid=(s // tm, nb), in_specs=in_specs, out_specs=oblk,
        out_shape=jax.ShapeDtypeStruct((s, nb * bn), out_dtype),
        compiler_params=_params(("parallel", "parallel")),
    )(*args)


def dense_kacc_nt(a, w3, name):
    s = a.shape[0]
    nb, n, bk = w3.shape
    tm = ROW_TILE

    def body(a_ref, w_ref, o_ref):
        j = pl.program_id(1)

        @pl.when(j == 0)
        def _():
            o_ref[...] = jnp.zeros_like(o_ref)

        o_ref[...] += _nt(a_ref[...].astype(BF16), w_ref[...])

    return _call(
        body, name=name, grid=(s // tm, nb),
        in_specs=[pl.BlockSpec((tm, bk), lambda i, j: (i, j)), pl.BlockSpec((None, n, bk), lambda i, j: (j, 0, 0))],
        out_specs=pl.BlockSpec((tm, n), lambda i, j: (i, 0)),
        out_shape=jax.ShapeDtypeStruct((s, n), F32),
        compiler_params=_params(("parallel", "arbitrary")),
    )(a, w3)


def dense_tn(a, b, *, nb, name, relu2_in=False):
    s, k = a.shape
    n = b.shape[1]
    bn = n // nb
    tk = 1024
    ts = ROW_TILE
    ns = s // ts

    def body(a_ref, b_ref, o_ref, acc_ref):
        t = pl.program_id(2)

        @pl.when(t == 0)
        def _():
            acc_ref[...] = jnp.zeros_like(acc_ref)

        av = a_ref[...]
        av = _relu2(av).astype(BF16) if relu2_in else av.astype(BF16)
        acc_ref[...] += _tn(av, b_ref[...].astype(BF16))

        @pl.when(t == ns - 1)
        def _():
            o_ref[...] = acc_ref[...].astype(o_ref.dtype)

    return _call(
        body, name=name, grid=(nb, k // tk, ns),
        in_specs=[pl.BlockSpec((ts, tk), lambda j, kb, t: (t, kb)), pl.BlockSpec((ts, bn), lambda j, kb, t: (t, j))],
        out_specs=pl.BlockSpec((None, tk, bn), lambda j, kb, t: (j, kb, 0)),
        out_shape=jax.ShapeDtypeStruct((nb, k, bn), BF16),
        scratch_shapes=[pltpu.VMEM((tk, bn), F32)],
        compiler_params=_params(("parallel", "parallel", "arbitrary")),
    )(a, b)


def _softplus(z):
    return jnp.maximum(z, 0.0) + jnp.log(1.0 + jnp.exp(-jnp.abs(z)))


def _split_bf16(v):
    hi = v.astype(BF16)
    lo = (v - hi.astype(F32)).astype(BF16)
    return hi, lo


def _tri_mats():
    t = SB_TILE
    r = jnp.arange(t)[:, None]
    c = jnp.arange(t)[None, :]
    return jnp.stack([(r > c), (r < c), (r <= c)]).astype(BF16)


def sb_fwd(qkv, tri, name):
    s = qkv.shape[0]
    t = SB_TILE

    def body(q_ref, k_ref, v_ref, u_ref, o_ref, lt_ref, c_ref, acc_ref):
        i = pl.program_id(1)
        lane = lax.broadcasted_iota(jnp.int32, (t, PAIR), 1)
        causal = lax.broadcasted_iota(jnp.int32, (t, t), 1) < lax.broadcasted_iota(jnp.int32, (t, t), 0)
        qv = q_ref[...]
        u_after = u_ref[0]
        outs, tots = [], []
        for h in range(2):
            hm = (lane < HEAD_DIM) if h == 0 else (lane >= HEAD_DIM)
            qh = jnp.where(hm, qv, 0) * QK_SCALE
            c_ref[...] = jnp.zeros_like(c_ref)
            acc_ref[...] = jnp.zeros_like(acc_ref)

            def tile(j, masked, qh=qh):
                off = pl.multiple_of(j * t, t)
                kj = k_ref[pl.ds(off, t), :]
                vj = v_ref[pl.ds(off, t), :]
                z = _nt(qh, kj)
                l = -_softplus(z)
                if masked:
                    l = jnp.where(causal, l, 0.0)
                hi, lo = _split_bf16(l)
                after = _nn(hi, u_after) + _nn(lo, u_after)
                a = jnp.exp(z + l + after + c_ref[...])
                if masked:
                    a = jnp.where(causal, a, 0.0)
                acc_ref[...] += _nn(a.astype(BF16), vj)
                c_ref[...] += after[:, 0:1] + l[:, 0:1]

            tile(i, True)

            def step(jj, carry):
                tile(i - 1 - jj, False)
                return carry

            lax.fori_loop(0, i, step, 0)
            outs.append(acc_ref[...])
            tots.append(c_ref[...])
        o_ref[...] = jnp.where(lane < HEAD_DIM, outs[0], outs[1]).astype(o_ref.dtype)
        lt_ref[...] = jnp.where(lane == 0, tots[0], jnp.where(lane == 1, tots[1], 0.0))

    return _call(
        body, name=name, grid=(N_PAIRS, s // t),
        in_specs=[
            pl.BlockSpec((t, PAIR), lambda p, i: (i, p)),
            pl.BlockSpec((s, PAIR), lambda p, i: (0, N_PAIRS + p)),
            pl.BlockSpec((s, PAIR), lambda p, i: (0, 2 * N_PAIRS + p)),
            pl.BlockSpec((3, t, t), lambda p, i: (0, 0, 0)),
        ],
        out_specs=[pl.BlockSpec((t, PAIR), lambda p, i: (i, p)), pl.BlockSpec((None, t, PAIR), lambda p, i: (p, i, 0))],
        out_shape=[jax.ShapeDtypeStruct((s, D_MODEL), BF16), jax.ShapeDtypeStruct((N_PAIRS, s, PAIR), F32)],
        scratch_shapes=[pltpu.VMEM((t, 1), F32), pltpu.VMEM((t, PAIR), F32)],
        compiler_params=_params(("parallel", "arbitrary")),
    )(qkv, qkv, qkv, tri)


def sb_bwd(qkv, o, do, ltot, tri, name):
    s = qkv.shape[0]
    t = SB_TILE
    nq = s // t

    def body(q_ref, k_ref, v_ref, o_ref, do_ref, lt_ref, u_ref, dq_ref, dk_ref, dv_ref,
             pl_ref, pg_ref, dqa_ref, dka_ref, dva_ref):
        i = pl.program_id(1)

        @pl.when(i == 0)
        def _():
            dka_ref[...] = jnp.zeros_like(dka_ref)
            dva_ref[...] = jnp.zeros_like(dva_ref)

        lane = lax.broadcasted_iota(jnp.int32, (t, PAIR), 1)
        causal = lax.broadcasted_iota(jnp.int32, (t, t), 1) < lax.broadcasted_iota(jnp.int32, (t, t), 0)
        qv = q_ref[...]
        dov = do_ref[...]
        prod = dov.astype(F32) * o_ref[...].astype(F32)
        ltv = lt_ref[...]
        u_before = u_ref[1]
        u_upto = u_ref[2]
        dqs = []
        for h in range(2):
            hm = (lane < HEAD_DIM) if h == 0 else (lane >= HEAD_DIM)
            qh = jnp.where(hm, qv, 0) * QK_SCALE
            doh = jnp.where(hm, dov, 0)
            ltot_h = jnp.sum(jnp.where(lane == h, ltv, 0.0), axis=1, keepdims=True)
            pl_ref[...] = jnp.zeros_like(pl_ref)
            pg_ref[...] = jnp.zeros_like(pg_ref)
            dqa_ref[...] = jnp.zeros_like(dqa_ref)

            def tile(j, masked, qh=qh, doh=doh, ltot_h=ltot_h):
                off = pl.multiple_of(j * t, t)
                kj = k_ref[pl.ds(off, t), :]
                vj = v_ref[pl.ds(off, t), :]
                z = _nt(qh, kj)
                sp = _softplus(z)
                l = -sp
                if masked:
                    l = jnp.where(causal, l, 0.0)
                hi, lo = _split_bf16(l)
                before = _nn(hi, u_before) + _nn(lo, u_before)
                a = jnp.exp(z + (ltot_h - pl_ref[...] - before))
                if masked:
                    a = jnp.where(causal, a, 0.0)
                g = a * _nt(doh, vj)
                upto = _nn(g.astype(BF16), u_upto)
                dz = g - jnp.exp(z - sp) * (pg_ref[...] + upto)
                if masked:
                    dz = jnp.where(causal, dz, 0.0)
                dzb = dz.astype(BF16)
                dqa_ref[...] += _nn(dzb, kj)
                dka_ref[pl.ds(off, t), :] += _tn(dzb, qh)
                dva_ref[pl.ds(off, t), :] += _tn(a.astype(BF16), doh)
                pl_ref[...] += before[:, t - 1:t] + l[:, t - 1:t]
                pg_ref[...] += upto[:, t - 1:t]

            def step(j, carry):
                tile(j, False)
                return carry

            lax.fori_loop(0, i, step, 0)
            tile(i, True)
            dqs.append(dqa_ref[...] * QK_SCALE)
        dq_ref[...] = jnp.where(lane < HEAD_DIM, dqs[0], dqs[1]).astype(dq_ref.dtype)

        @pl.when(i == nq - 1)
        def _():
            dk_ref[...] = dka_ref[...].astype(dk_ref.dtype)
            dv_ref[...] = dva_ref[...].astype(dv_ref.dtype)

    qblk = pl.BlockSpec((t, PAIR), lambda p, i: (i, p))
    full = pl.BlockSpec((s, PAIR), lambda p, i: (0, p))
    out = jax.ShapeDtypeStruct((s, D_MODEL), BF16)
    return _call(
        body, name=name, grid=(N_PAIRS, nq),
        in_specs=[
            qblk,
            pl.BlockSpec((s, PAIR), lambda p, i: (0, N_PAIRS + p)),
            pl.BlockSpec((s, PAIR), lambda p, i: (0, 2 * N_PAIRS + p)),
            qblk, qblk,
            pl.BlockSpec((None, t, PAIR), lambda p, i: (p, i, 0)),
            pl.BlockSpec((3, t, t), lambda p, i: (0, 0, 0)),
        ],
        out_specs=[qblk, full, full],
        out_shape=[out, out, out],
        scratch_shapes=[pltpu.VMEM((t, 1), F32), pltpu.VMEM((t, 1), F32), pltpu.VMEM((t, PAIR), F32),
                        pltpu.VMEM((s, PAIR), F32), pltpu.VMEM((s, PAIR), F32)],
        compiler_params=_params(("parallel", "arbitrary")),
    )(qkv, qkv, qkv, o, do, ltot, tri)


def _ring_onehot():
    r = lax.broadcasted_iota(jnp.int32, (N_REL_PAD, CA_RING), 0)
    w = lax.broadcasted_iota(jnp.int32, (N_REL_PAD, CA_RING), 1)
    idx = jnp.where(w <= CA_WIN, jnp.clip(CA_WIN - w, 0, 2 * MAX_REL), 2 * MAX_REL)
    return (r == idx).astype(F32)


def bias_table(rel_pad, name):
    def body(rb_ref, o_ref):
        ring = jnp.dot(rb_ref[...], _ring_onehot(), preferred_element_type=F32, precision=lax.Precision.HIGHEST)
        col = lax.broadcasted_iota(jnp.int32, (N_HEADS, CA_WIN), 1)

        def row(tl, carry):
            v = pltpu.roll(ring, tl, 1)[:, :CA_WIN]
            first = (tl // CHUNK) * CHUNK
            ok = (col >= first) & (col < first + (LEFT_CHUNKS + 1) * CHUNK)
            o_ref[tl] = jnp.where(ok, v, NEG_BIG)
            return carry

        lax.fori_loop(0, CA_QBLK, row, 0)

    return _call(
        body, name=name,
        in_specs=[pl.BlockSpec(memory_space=pltpu.VMEM)], out_specs=pl.BlockSpec(memory_space=pltpu.VMEM),
        out_shape=jax.ShapeDtypeStruct((CA_QBLK, N_HEADS, CA_WIN), F32),
        compiler_params=_params(),
    )(rel_pad)


def bias_fold(db, name):
    def body(db_ref, o_ref):
        zeros = jnp.zeros((N_HEADS, CA_RING - CA_WIN), F32)

        def row(tl, acc):
            v = jnp.concatenate([db_ref[tl], zeros], axis=1)
            return acc + pltpu.roll(v, (CA_RING - tl) % CA_RING, 1)

        ring = lax.fori_loop(0, CA_QBLK, row, jnp.zeros((N_HEADS, CA_RING), F32))
        o_ref[...] = lax.dot_general(ring, _ring_onehot(), (((1,), (1,)), ((), ())),
                                     preferred_element_type=F32, precision=lax.Precision.HIGHEST)

    return _call(
        body, name=name,
        in_specs=[pl.BlockSpec(memory_space=pltpu.VMEM)], out_specs=pl.BlockSpec(memory_space=pltpu.VMEM),
        out_shape=jax.ShapeDtypeStruct((N_HEADS, N_REL_PAD), F32),
        compiler_params=_params(),
    )(db)


def _ca_logits(qh, kw, bias_h, first_key_ok):
    sc = _nt(qh, kw) + bias_h
    col = lax.broadcasted_iota(jnp.int32, sc.shape, 1)
    return jnp.where(col >= first_key_ok, sc, NEG_BIG)


def ca_fwd(qkv, table, name):
    s = qkv.shape[0]
    tq = CA_QBLK

    def body(q_ref, k_ref, v_ref, b_ref, o_ref, kp_ref, vp_ref):
        i = pl.program_id(1)

        @pl.when(i == 0)
        def _():
            kp_ref[0:CA_PAD, :] = jnp.zeros((CA_PAD, PAIR), BF16)
            vp_ref[0:CA_PAD, :] = jnp.zeros((CA_PAD, PAIR), BF16)
            kp_ref[CA_PAD:, :] = k_ref[...]
            vp_ref[CA_PAD:, :] = v_ref[...]

        lane = lax.broadcasted_iota(jnp.int32, (tq, PAIR), 1)
        off = pl.multiple_of(i * tq, tq)
        kw = kp_ref[pl.ds(off, CA_WIN), :]
        vw = vp_ref[pl.ds(off, CA_WIN), :]
        qv = q_ref[...]
        first_ok = CA_PAD - i * tq
        outs = []
        for h in range(2):
            hm = (lane < HEAD_DIM) if h == 0 else (lane >= HEAD_DIM)
            qh = jnp.where(hm, qv, 0) * QK_SCALE
            sc = _ca_logits(qh, kw, b_ref[h], first_ok)
            m = jnp.max(sc, axis=1, keepdims=True)
            e = jnp.exp(sc - m)
            den = jnp.sum(e, axis=1, keepdims=True)
            outs.append(_nn((e / den).astype(BF16), vw))
        o_ref[...] = jnp.where(lane < HEAD_DIM, outs[0], outs[1]).astype(o_ref.dtype)

    return _call(
        body, name=name, grid=(N_PAIRS, s // tq),
        in_specs=[
            pl.BlockSpec((tq, PAIR), lambda p, i: (i, p)),
            pl.BlockSpec((s, PAIR), lambda p, i: (0, N_PAIRS + p)),
            pl.BlockSpec((s, PAIR), lambda p, i: (0, 2 * N_PAIRS + p)),
            pl.BlockSpec((2, tq, CA_WIN), lambda p, i: (p, 0, 0)),
        ],
        out_specs=pl.BlockSpec((tq, PAIR), lambda p, i: (i, p)),
        out_shape=jax.ShapeDtypeStruct((s, D_MODEL), BF16),
        scratch_shapes=[pltpu.VMEM((s + CA_PAD, PAIR), BF16), pltpu.VMEM((s + CA_PAD, PAIR), BF16)],
        compiler_params=_params(("parallel", "arbitrary")),
    )(qkv, qkv, qkv, table)


def ca_bwd(qkv, o, do, table, name):
    s = qkv.shape[0]
    tq = CA_QBLK
    nq = s // tq

    def body(q_ref, k_ref, v_ref, o_ref, do_ref, b_ref, dq_ref, dk_ref, dv_ref, db_ref,
             kp_ref, vp_ref, dka_ref, dva_ref):
        i = pl.program_id(1)

        @pl.when(i == 0)
        def _():
            kp_ref[0:CA_PAD, :] = jnp.zeros((CA_PAD, PAIR), BF16)
            vp_ref[0:CA_PAD, :] = jnp.zeros((CA_PAD, PAIR), BF16)
            kp_ref[CA_PAD:, :] = k_ref[...]
            vp_ref[CA_PAD:, :] = v_ref[...]
            dka_ref[...] = jnp.zeros_like(dka_ref)
            dva_ref[...] = jnp.zeros_like(dva_ref)
            db_ref[...] = jnp.zeros_like(db_ref)

        lane = lax.broadcasted_iota(jnp.int32, (tq, PAIR), 1)
        off = pl.multiple_of(i * tq, tq)
        kw = kp_ref[pl.ds(off, CA_WIN), :]
        vw = vp_ref[pl.ds(off, CA_WIN), :]
        qv = q_ref[...]
        dov = do_ref[...]
        prod = dov.astype(F32) * o_ref[...].astype(F32)
        first_ok = CA_PAD - i * tq
        dqs = []
        for h in range(2):
            hm = (lane < HEAD_DIM) if h == 0 else (lane >= HEAD_DIM)
            qh = jnp.where(hm, qv, 0) * QK_SCALE
            doh = jnp.where(hm, dov, 0)
            delta = jnp.sum(jnp.where(hm, prod, 0.0), axis=1, keepdims=True)
            sc = _ca_logits(qh, kw, b_ref[h], first_ok)
            m = jnp.max(sc, axis=1, keepdims=True)
            e = jnp.exp(sc - m)
            p = e / jnp.sum(e, axis=1, keepdims=True)
            ds = p * (_nt(doh, vw) - delta)
            db_ref[h] += ds
            dsb = ds.astype(BF16)
            dqs.append(_nn(dsb, kw) * QK_SCALE)
            dka_ref[pl.ds(off, CA_WIN), :] += _tn(dsb, qh)
            dva_ref[pl.ds(off, CA_WIN), :] += _tn(p.astype(BF16), doh)
        dq_ref[...] = jnp.where(lane < HEAD_DIM, dqs[0], dqs[1]).astype(dq_ref.dtype)

        @pl.when(i == nq - 1)
        def _():
            dk_ref[...] = dka_ref[CA_PAD:, :].astype(dk_ref.dtype)
            dv_ref[...] = dva_ref[CA_PAD:, :].astype(dv_ref.dtype)

    qblk = pl.BlockSpec((tq, PAIR), lambda p, i: (i, p))
    full = pl.BlockSpec((s, PAIR), lambda p, i: (0, p))
    tblk = pl.BlockSpec((2, tq, CA_WIN), lambda p, i: (p, 0, 0))
    out = jax.ShapeDtypeStruct((s, D_MODEL), BF16)
    return _call(
        body, name=name, grid=(N_PAIRS, nq),
        in_specs=[
            qblk,
            pl.BlockSpec((s, PAIR), lambda p, i: (0, N_PAIRS + p)),
            pl.BlockSpec((s, PAIR), lambda p, i: (0, 2 * N_PAIRS + p)),
            qblk, qblk, tblk,
        ],
        out_specs=[qblk, full, full, tblk],
        out_shape=[out, out, out, jax.ShapeDtypeStruct((N_HEADS, tq, CA_WIN), F32)],
        scratch_shapes=[pltpu.VMEM((s + CA_PAD, PAIR), BF16), pltpu.VMEM((s + CA_PAD, PAIR), BF16),
                        pltpu.VMEM((s + CA_PAD, PAIR), F32), pltpu.VMEM((s + CA_PAD, PAIR), F32)],
        compiler_params=_params(("parallel", "arbitrary")),
    )(qkv, qkv, qkv, o, do, table)


def _me():
    x, y, c = lax.axis_index("x"), lax.axis_index("y"), lax.axis_index("c")
    return x, y, c, 4 * x + 2 * y + c


def _peer(k):
    x, y, c, _ = _me()
    px, py, pc = x ^ ((k >> 2) & 1), y ^ ((k >> 1) & 1), c ^ (k & 1)
    return (px, py, pc), 4 * px + 2 * py + pc


ANY = pl.BlockSpec(memory_space=pl.ANY)


def gather_weights(shards, name):
    n = len(shards)

    def body(*refs):
        src, dst = refs[:n], refs[n:2 * n]
        send_sems, recv_sems, local_sems = refs[2 * n:]
        _, _, _, me = _me()
        local = [pltpu.make_async_copy(src[a], dst[a].at[:, me], local_sems.at[a]) for a in range(n)]
        for cp in local:
            cp.start()
        sends = []
        for k in range(1, N_DEV):
            peer, _ = _peer(k)
            for a in range(n):
                cp = pltpu.make_async_remote_copy(
                    src_ref=src[a], dst_ref=dst[a].at[:, me], send_sem=send_sems.at[a, k - 1],
                    recv_sem=recv_sems.at[a, k - 1], device_id=peer, device_id_type=MESH)
                cp.start()
                sends.append(cp)
        for k in range(1, N_DEV):
            peer, pidx = _peer(k)
            for a in range(n):
                pltpu.make_async_remote_copy(
                    src_ref=src[a], dst_ref=dst[a].at[:, pidx], send_sem=send_sems.at[a, k - 1],
                    recv_sem=recv_sems.at[a, k - 1], device_id=peer, device_id_type=MESH).wait_recv()
        for cp in sends:
            cp.wait_send()
        for cp in local:
            cp.wait()

    return _call(
        body, name=name,
        in_specs=[ANY] * n, out_specs=[ANY] * n,
        out_shape=[jax.ShapeDtypeStruct((DEPTH, N_DEV) + w.shape[1:], w.dtype) for w in shards],
        scratch_shapes=[pltpu.SemaphoreType.DMA((n, N_DEV - 1)), pltpu.SemaphoreType.DMA((n, N_DEV - 1)),
                        pltpu.SemaphoreType.DMA((n,))],
        compiler_params=pltpu.CompilerParams(has_side_effects=True),
    )(*shards)


def scatter_grads(partials, name):
    n = len(partials)

    def body(*refs):
        src, dst = refs[:n], refs[n:2 * n]
        send_sems, recv_sems, local_sems = refs[2 * n:]
        _, _, _, me = _me()
        local = [pltpu.make_async_copy(src[a].at[:, me], dst[a].at[me], local_sems.at[a]) for a in range(n)]
        for cp in local:
            cp.start()
        sends = []
        for k in range(1, N_DEV):
            peer, pidx = _peer(k)
            for a in range(n):
                cp = pltpu.make_async_remote_copy(
                    src_ref=src[a].at[:, pidx], dst_ref=dst[a].at[me], send_sem=send_sems.at[a, k - 1],
                    recv_sem=recv_sems.at[a, k - 1], device_id=peer, device_id_type=MESH)
                cp.start()
                sends.append(cp)
        for k in range(1, N_DEV):
            peer, pidx = _peer(k)
            for a in range(n):
                pltpu.make_async_remote_copy(
                    src_ref=src[a].at[:, pidx], dst_ref=dst[a].at[pidx], send_sem=send_sems.at[a, k - 1],
                    recv_sem=recv_sems.at[a, k - 1], device_id=peer, device_id_type=MESH).wait_recv()
        for cp in sends:
            cp.wait_send()
        for cp in local:
            cp.wait()

    return _call(
        body, name=name,
        in_specs=[ANY] * n, out_specs=[ANY] * n,
        out_shape=[jax.ShapeDtypeStruct((N_DEV, DEPTH) + p.shape[2:], p.dtype) for p in partials],
        scratch_shapes=[pltpu.SemaphoreType.DMA((n, N_DEV - 1)), pltpu.SemaphoreType.DMA((n, N_DEV - 1)),
                        pltpu.SemaphoreType.DMA((n,))],
        compiler_params=pltpu.CompilerParams(has_side_effects=True),
    )(*partials)


def _adamw(w, g, m, v):
    m = ADAM_B1 * m + (1.0 - ADAM_B1) * g
    v = ADAM_B2 * v + (1.0 - ADAM_B2) * (g * g)
    m_hat = m / (1.0 - ADAM_B1 ** ADAM_STEP)
    v_hat = v / (1.0 - ADAM_B2 ** ADAM_STEP)
    delta = -ADAM_LR * (m_hat / (jnp.sqrt(v_hat) + ADAM_EPS) + ADAM_WD * w)
    return delta, m, v


def adam_shard(parts, w, m, v, name):
    r, c = w.shape
    tr = 256

    def body(p_ref, w_ref, m_ref, v_ref, g_out, d_out, m_out, v_out):
        g = p_ref[0].astype(F32)
        for e in range(1, N_DEV):
            g = g + p_ref[e].astype(F32)
        delta, mn, vn = _adamw(w_ref[...], g, m_ref[...], v_ref[...])
        g_out[...] = g
        d_out[...] = delta
        m_out[...] = mn
        v_out[...] = vn

    blk = pl.BlockSpec((tr, c), lambda i: (i, 0))
    out = jax.ShapeDtypeStruct((r, c), F32)
    return _call(
        body, name=name, grid=(r // tr,),
        in_specs=[pl.BlockSpec((N_DEV, tr, c), lambda i: (0, i, 0)), blk, blk, blk],
        out_specs=[blk] * 4, out_shape=[out] * 4,
        compiler_params=_params(("parallel",)),
    )(parts, w, m, v)


def small_allreduce_adam(pk, w, m, v, name):
    def body(pk_ref, w_ref, m_ref, v_ref, g_out, d_out, m_out, v_out, all_ref, send_sems, recv_sems):
        _, _, _, me = _me()
        all_ref[me] = pk_ref[...]
        sends = []
        for k in range(1, N_DEV):
            peer, _ = _peer(k)
            cp = pltpu.make_async_remote_copy(
                src_ref=pk_ref, dst_ref=all_ref.at[me], send_sem=send_sems.at[k - 1],
                recv_sem=recv_sems.at[k - 1], device_id=peer, device_id_type=MESH)
            cp.start()
            sends.append(cp)
        for k in range(1, N_DEV):
            peer, pidx = _peer(k)
            pltpu.make_async_remote_copy(
                src_ref=pk_ref, dst_ref=all_ref.at[pidx], send_sem=send_sems.at[k - 1],
                recv_sem=recv_sems.at[k - 1], device_id=peer, device_id_type=MESH).wait_recv()
        for cp in sends:
            cp.wait_send()
        g = all_ref[0]
        for e in range(1, N_DEV):
            g = g + all_ref[e]
        delta, mn, vn = _adamw(w_ref[...], g, m_ref[...], v_ref[...])
        g_out[...] = g
        d_out[...] = delta
        m_out[...] = mn
        v_out[...] = vn

    vm = pl.BlockSpec(memory_space=pltpu.VMEM)
    out = jax.ShapeDtypeStruct((PACK_ROWS, 128), F32)
    return _call(
        body, name=name,
        in_specs=[vm] * 4, out_specs=[vm] * 4, out_shape=[out] * 4,
        scratch_shapes=[pltpu.VMEM((N_DEV, PACK_ROWS, 128), F32), pltpu.SemaphoreType.DMA((N_DEV - 1,)),
                        pltpu.SemaphoreType.DMA((N_DEV - 1,))],
        compiler_params=pltpu.CompilerParams(has_side_effects=True),
    )(pk, w, m, v)


def _pack_small(mix, ffn, qn, kn, rel, loss):
    flat = jnp.concatenate([mix.reshape(-1), ffn.reshape(-1), qn.reshape(-1), kn.reshape(-1), rel.reshape(-1),
                            loss.reshape(-1)])
    return jnp.pad(flat, (0, PACK_LEN - flat.shape[0])).reshape(PACK_ROWS, 128)


def _unpack_small(pk):
    flat = pk.reshape(-1)
    return (flat[OFF_MIX:OFF_FFN].reshape(DEPTH, D_MODEL), flat[OFF_FFN:OFF_QN].reshape(DEPTH, D_MODEL),
            flat[OFF_QN:OFF_KN].reshape(2, HEAD_DIM), flat[OFF_KN:OFF_REL].reshape(2, HEAD_DIM),
            flat[OFF_REL:OFF_LOSS].reshape(2, N_HEADS, N_REL), flat[OFF_LOSS])


def _pair_gain(g):
    return jnp.concatenate([g, g]).reshape(1, PAIR)


def kernel(x, mix_norm, w_qkv, w_o, q_norm, k_norm, rel_bias, ffn_norm, w_up, w_down, loss_target, m_mix_norm, m_w_qkv, m_w_o, m_q_norm, m_k_norm, m_rel_bias, m_ffn_norm, m_w_up, m_w_down, v_mix_norm, v_w_qkv, v_w_o, v_q_norm, v_k_norm, v_rel_bias, v_ffn_norm, v_w_up, v_w_down):
    x0 = x[0]
    target = loss_target[0]
    nb = N_DEV

    g_qkv, g_o, g_up, g_down = gather_weights(
        [w_qkv.astype(BF16), w_o.astype(BF16), w_up.astype(BF16), w_down.astype(BF16)], "gather_weights")
    tri = _tri_mats()

    saved = []
    xin = x0
    for layer in range(DEPTH):
        mixer_b = layer % 2 == 1
        idx = layer // 2
        wq3 = g_qkv[layer]
        wo1 = g_o[layer].reshape(1, D_MODEL, D_MODEL)
        wu3 = g_up[layer]
        wd1 = g_down[layer].reshape(1, D_FF, D_MODEL)
        h0 = rms_fwd(xin, mix_norm[layer:layer + 1], f"rms_mix_{layer}")
        qkv_raw = dense_rowblock(h0, wq3, nt=False, name=f"qkv_{layer}", out_dtype=BF16)
        if mixer_b:
            gq, gk = _pair_gain(q_norm[idx]), _pair_gain(k_norm[idx])
            qkv = headnorm_fwd(qkv_raw, gq, gk, f"headnorm_{layer}")
            rel_pad = jnp.pad(rel_bias[idx], ((0, 0), (0, N_REL_PAD - N_REL)))
            table = jnp.transpose(bias_table(rel_pad, f"bias_table_{layer}"), (1, 0, 2))
            o = ca_fwd(qkv, table, f"ca_fwd_{layer}")
            attn_saved = (qkv_raw, qkv, table)
        else:
            o, ltot = sb_fwd(qkv_raw, tri, f"sb_fwd_{layer}")
            attn_saved = (qkv_raw, ltot)
        x1 = dense_rowblock(o, wo1, nt=False, name=f"attn_out_{layer}", out_dtype=F32, res=xin)
        h1 = rms_fwd(x1, ffn_norm[layer:layer + 1], f"rms_ffn_{layer}")
        pre = dense_rowblock(h1, wu3, nt=False, name=f"up_{layer}", out_dtype=BF16)
        x2 = dense_rowblock(pre, wd1, nt=False, name=f"down_{layer}", out_dtype=F32, relu2_in=True, res=x1)
        saved.append((xin, h0, attn_saved, o, x1, h1, pre))
        xin = x2

    dx, loss_blk = loss_head(xin, target, "loss_head")

    p_qkv, p_o, p_up, p_down = [None] * DEPTH, [None] * DEPTH, [None] * DEPTH, [None] * DEPTH
    d_mix, d_ffn = [None] * DEPTH, [None] * DEPTH
    d_qn, d_kn, d_rel = [None] * 2, [None] * 2, [None] * 2
    for layer in reversed(range(DEPTH)):
        mixer_b = layer % 2 == 1
        idx = layer // 2
        xin, h0, attn_saved, o, x1, h1, pre = saved[layer]
        wq3 = g_qkv[layer]
        wu3 = g_up[layer]
        wo_nt = g_o[layer].reshape(1, D_MODEL, D_MODEL)
        wd_nt = g_down[layer]
        dpre = dense_rowblock(dx, wd_nt, nt=True, name=f"d_pre_{layer}", out_dtype=BF16, relu2_grad=pre)
        p_down[layer] = dense_tn(pre, dx, nb=1, name=f"dw_down_{layer}", relu2_in=True).reshape(nb, D_FF // nb, D_MODEL)
        p_up[layer] = dense_tn(h1, dpre, nb=nb, name=f"dw_up_{layer}")
        dh1 = dense_kacc_nt(dpre, wu3, f"d_h1_{layer}")
        dx, d_ffn[layer] = rms_bwd(dh1, x1, ffn_norm[layer:layer + 1], dx, f"rms_ffn_bwd_{layer}")
        do = dense_rowblock(dx, wo_nt, nt=True, name=f"d_o_{layer}", out_dtype=BF16)
        p_o[layer] = dense_tn(o, dx, nb=1, name=f"dw_o_{layer}").reshape(nb, D_MODEL // nb, D_MODEL)
        if mixer_b:
            qkv_raw, qkv, table = attn_saved
            dq, dk, dv, dtab = ca_bwd(qkv, o, do, table, f"ca_bwd_{layer}")
            dn = jnp.concatenate([dq, dk, dv], axis=1)
            gq, gk = _pair_gain(q_norm[idx]), _pair_gain(k_norm[idx])
            dqkv, dgain = headnorm_bwd(dn, qkv_raw, gq, gk, f"headnorm_bwd_{layer}")
            d_qn[idx] = dgain[0, :HEAD_DIM] + dgain[0, HEAD_DIM:]
            d_kn[idx] = dgain[1, :HEAD_DIM] + dgain[1, HEAD_DIM:]
            d_rel[idx] = bias_fold(jnp.transpose(dtab, (1, 0, 2)), f"bias_fold_{layer}")[:, :N_REL]
        else:
            qkv_raw, ltot = attn_saved
            dq, dk, dv = sb_bwd(qkv_raw, o, do, ltot, tri, f"sb_bwd_{layer}")
            dqkv = jnp.concatenate([dq, dk, dv], axis=1)
        p_qkv[layer] = dense_tn(h0, dqkv, nb=nb, name=f"dw_qkv_{layer}")
        dh0 = dense_kacc_nt(dqkv, wq3, f"d_h0_{layer}")
        dx, d_mix[layer] = rms_bwd(dh0, xin, mix_norm[layer:layer + 1], dx, f"rms_mix_bwd_{layer}")

    grad_x = dx[None]

    r_qkv, r_o, r_up, r_down = scatter_grads(
        [jnp.stack(p_qkv), jnp.stack(p_o), jnp.stack(p_up), jnp.stack(p_down)], "scatter_grads")

    def big(parts, w, m, v, name):
        shp = w.shape
        c = shp[-1]
        outs = adam_shard(parts.reshape(N_DEV, -1, c), w.reshape(-1, c), m.reshape(-1, c), v.reshape(-1, c), name)
        return [t.reshape(shp) for t in outs]

    a_qkv = big(r_qkv, w_qkv, m_w_qkv, v_w_qkv, "adam_qkv")
    a_o = big(r_o, w_o, m_w_o, v_w_o, "adam_o")
    a_up = big(r_up, w_up, m_w_up, v_w_up, "adam_up")
    a_down = big(r_down, w_down, m_w_down, v_w_down, "adam_down")

    pk = _pack_small(jnp.concatenate(d_mix), jnp.concatenate(d_ffn), jnp.stack(d_qn), jnp.stack(d_kn),
                     jnp.stack(d_rel), loss_blk[0, 0])
    zero = jnp.zeros((), F32)
    pw = _pack_small(mix_norm, ffn_norm, q_norm, k_norm, rel_bias, zero)
    pm = _pack_small(m_mix_norm, m_ffn_norm, m_q_norm, m_k_norm, m_rel_bias, zero)
    pv = _pack_small(v_mix_norm, v_ffn_norm, v_q_norm, v_k_norm, v_rel_bias, zero)
    s_g, s_d, s_m, s_v = small_allreduce_adam(pk, pw, pm, pv, "small_allreduce_adam")
    g_mix, g_ffn, g_qn, g_kn, g_rel, loss = _unpack_small(s_g)
    dl_mix, dl_ffn, dl_qn, dl_kn, dl_rel, _ = _unpack_small(s_d)
    nm_mix, nm_ffn, nm_qn, nm_kn, nm_rel, _ = _unpack_small(s_m)
    nv_mix, nv_ffn, nv_qn, nv_kn, nv_rel, _ = _unpack_small(s_v)

    return (loss, grad_x,
            g_mix, a_qkv[0], a_o[0], g_qn, g_kn, g_rel, g_ffn, a_up[0], a_down[0],
            dl_mix, a_qkv[1], a_o[1], dl_qn, dl_kn, dl_rel, dl_ffn, a_up[1], a_down[1],
            nm_mix, a_qkv[2], a_o[2], nm_qn, nm_kn, nm_rel, nm_ffn, a_up[2], a_down[2],
            nv_mix, a_qkv[3], a_o[3], nv_qn, nv_kn, nv_rel, nv_ffn, a_up[3], a_down[3])
```

```python
import jax
import jax.numpy as jnp
from jax import lax
from jax.experimental import pallas as pl
from jax.experimental.pallas import tpu as pltpu

F32 = jnp.float32
BF16 = jnp.bfloat16
MESH = pl.DeviceIdType.MESH

D_MODEL = 1024
N_HEADS = 16
HEAD_DIM = 64
PAIR = 2 * HEAD_DIM
N_PAIRS = N_HEADS // 2
D_FF = 4 * D_MODEL
DEPTH = 4
N_DEV = 8
RMS_EPS = 1e-6
QK_SCALE = HEAD_DIM ** -0.5

SB_TILE = 256
SB_DEAD = 104.0

CHUNK = 64
LEFT_CHUNKS = 8
CA_QBLK = 256
CA_PAD = LEFT_CHUNKS * CHUNK
CA_WIN = CA_QBLK + CA_PAD
CA_RING = 1024
MAX_REL = 256
N_REL = 2 * MAX_REL + 1
N_REL_PAD = 640
NEG_BIG = -1e30

ADAM_LR = 0.001
ADAM_B1 = 0.9
ADAM_B2 = 0.999
ADAM_EPS = 1e-08
ADAM_WD = 0.01
ADAM_STEP = 10

ROW_TILE = 512
DENSE_CHUNK = 512
TN_ROWS = 1024
VMEM_LIMIT = 56 * 1024 * 1024

OFF_MIX = 0
OFF_FFN = OFF_MIX + DEPTH * D_MODEL
OFF_QN = OFF_FFN + DEPTH * D_MODEL
OFF_KN = OFF_QN + 2 * HEAD_DIM
OFF_REL = OFF_KN + 2 * HEAD_DIM
OFF_LOSS = OFF_REL + 2 * N_HEADS * N_REL
PACK_ROWS = 200
PACK_LEN = PACK_ROWS * 128


def _call(body, **kw):
    return pl.pallas_call(body, **kw)


def _params(sem=None, vmem=VMEM_LIMIT):
    if sem is None:
        return pltpu.CompilerParams(vmem_limit_bytes=vmem)
    return pltpu.CompilerParams(dimension_semantics=sem, vmem_limit_bytes=vmem)


def _nt(a, b):
    return lax.dot_general(a, b, (((1,), (1,)), ((), ())), preferred_element_type=F32)


def _tn(a, b):
    return lax.dot_general(a, b, (((0,), (0,)), ((), ())), preferred_element_type=F32)


def _nn(a, b):
    return jnp.dot(a, b, preferred_element_type=F32)


def _split_bf16(v):
    hi = v.astype(BF16)
    lo = (v - hi.astype(F32)).astype(BF16)
    return hi, lo


def rms_fwd(x, g, name):
    s = x.shape[0]
    tm = ROW_TILE

    def body(x_ref, g_ref, o_ref):
        xv = x_ref[...]
        r = lax.rsqrt(jnp.mean(xv * xv, axis=-1, keepdims=True) + RMS_EPS)
        o_ref[...] = (xv * r * g_ref[...]).astype(o_ref.dtype)

    return _call(
        body, name=name, grid=(s // tm,),
        in_specs=[pl.BlockSpec((tm, D_MODEL), lambda i: (i, 0)), pl.BlockSpec((1, D_MODEL), lambda i: (0, 0))],
        out_specs=pl.BlockSpec((tm, D_MODEL), lambda i: (i, 0)),
        out_shape=jax.ShapeDtypeStruct((s, D_MODEL), BF16),
        compiler_params=_params(("parallel",)),
    )(x, g)


def rms_bwd(dh, x, g, dres, name):
    s = x.shape[0]
    tm = ROW_TILE

    def body(dh_ref, x_ref, g_ref, dres_ref, dx_ref, dg_ref):
        i = pl.program_id(0)
        xv = x_ref[...]
        dhv = dh_ref[...].astype(F32)
        r = lax.rsqrt(jnp.mean(xv * xv, axis=-1, keepdims=True) + RMS_EPS)
        xh = xv * r
        dy = dhv * g_ref[...]
        mdot = jnp.mean(dy * xh, axis=-1, keepdims=True)
        dx_ref[...] = dres_ref[...] + r * (dy - xh * mdot)

        @pl.when(i == 0)
        def _():
            dg_ref[...] = jnp.zeros_like(dg_ref)

        dg_ref[0:1, :] += jnp.sum(dhv * xh, axis=0, keepdims=True)

    row = pl.BlockSpec((tm, D_MODEL), lambda i: (i, 0))
    dx, dg = _call(
        body, name=name, grid=(s // tm,),
        in_specs=[row, row, pl.BlockSpec((1, D_MODEL), lambda i: (0, 0)), row],
        out_specs=[row, pl.BlockSpec((8, D_MODEL), lambda i: (0, 0))],
        out_shape=[jax.ShapeDtypeStruct((s, D_MODEL), F32), jax.ShapeDtypeStruct((8, D_MODEL), F32)],
        compiler_params=_params(("arbitrary",)),
    )(dh, x, g, dres)
    return dx, dg[0:1]


def loss_head(y, target, name):
    s = y.shape[0]
    tm = ROW_TILE

    def body(y_ref, t_ref, dy_ref, l_ref):
        i = pl.program_id(0)
        e = y_ref[...] - t_ref[...]
        dy_ref[...] = e * (1.0 / D_MODEL)

        @pl.when(i == 0)
        def _():
            l_ref[...] = jnp.zeros_like(l_ref)

        per_row = jnp.sum(e * e, axis=-1, keepdims=True) * (1.0 / D_MODEL)
        l_ref[...] += jnp.broadcast_to(0.5 * jnp.sum(per_row, axis=0, keepdims=True), l_ref.shape)

    row = pl.BlockSpec((tm, D_MODEL), lambda i: (i, 0))
    return _call(
        body, name=name, grid=(s // tm,),
        in_specs=[row, row],
        out_specs=[row, pl.BlockSpec((8, 128), lambda i: (0, 0))],
        out_shape=[jax.ShapeDtypeStruct((s, D_MODEL), F32), jax.ShapeDtypeStruct((8, 128), F32)],
        compiler_params=_params(("arbitrary",)),
    )(y, target)


def _group_sum_matrix():
    r = lax.broadcasted_iota(jnp.int32, (PAIR, PAIR), 0) // HEAD_DIM
    c = lax.broadcasted_iota(jnp.int32, (PAIR, PAIR), 1) // HEAD_DIM
    return (r == c).astype(BF16)


def _head_mean(v, gmat):
    hi, lo = _split_bf16(v)
    return (_nn(hi, gmat) + _nn(lo, gmat)) * (1.0 / HEAD_DIM)


def headnorm_fwd(qkv, gq, gk, name):
    s = qkv.shape[0]
    tm = ROW_TILE

    def body(x_ref, gq_ref, gk_ref, o_ref):
        j = pl.program_id(1)
        gmat = _group_sum_matrix()
        gain = jnp.where(j == 0, gq_ref[...], gk_ref[...])
        for b in range(N_PAIRS):
            cols = slice(b * PAIR, (b + 1) * PAIR)
            xv = x_ref[:, cols].astype(F32)
            r = lax.rsqrt(_head_mean(xv * xv, gmat) + RMS_EPS)
            o_ref[:, cols] = (xv * r * gain).astype(o_ref.dtype)

    blk = pl.BlockSpec((tm, D_MODEL), lambda i, j: (i, j))
    gspec = pl.BlockSpec((1, PAIR), lambda i, j: (0, 0))
    return _call(
        body, name=name, grid=(s // tm, 2),
        in_specs=[blk, gspec, gspec], out_specs=blk,
        out_shape=jax.ShapeDtypeStruct((s, 2 * D_MODEL), BF16),
        compiler_params=_params(("parallel", "parallel")),
    )(qkv, gq, gk)


def headnorm_bwd(dq, dk, dv, qkv, gq, gk, name):
    s = qkv.shape[0]
    tm = ROW_TILE

    def body(dq_ref, dk_ref, dv_ref, x_ref, gq_ref, gk_ref, dx_ref, dg_ref):
        i = pl.program_id(0)
        j = pl.program_id(1)

        @pl.when((i == 0) & (j == 0))
        def _():
            dg_ref[...] = jnp.zeros_like(dg_ref)

        @pl.when(j == 2)
        def _():
            dx_ref[...] = dv_ref[...]

        for part, (d_ref, g_ref) in enumerate(((dq_ref, gq_ref), (dk_ref, gk_ref))):
            @pl.when(j == part)
            def _(part=part, d_ref=d_ref, g_ref=g_ref):
                gmat = _group_sum_matrix()
                gain = g_ref[...]
                dg = jnp.zeros((1, PAIR), F32)
                for b in range(N_PAIRS):
                    cols = slice(b * PAIR, (b + 1) * PAIR)
                    xv = x_ref[:, cols].astype(F32)
                    dn = d_ref[:, cols].astype(F32)
                    r = lax.rsqrt(_head_mean(xv * xv, gmat) + RMS_EPS)
                    xh = xv * r
                    dy = dn * gain
                    dx_ref[:, cols] = (r * (dy - xh * _head_mean(dy * xh, gmat))).astype(dx_ref.dtype)
                    dg = dg + jnp.sum(dn * xh, axis=0, keepdims=True)
                dg_ref[part:part + 1, :] += dg

    row = pl.BlockSpec((tm, D_MODEL), lambda i, j: (i, 0))
    blk = pl.BlockSpec((tm, D_MODEL), lambda i, j: (i, j))
    gspec = pl.BlockSpec((1, PAIR), lambda i, j: (0, 0))
    dx, dg = _call(
        body, name=name, grid=(s // tm, 3),
        in_specs=[row, row, row, blk, gspec, gspec],
        out_specs=[blk, pl.BlockSpec((8, PAIR), lambda i, j: (0, 0))],
        out_shape=[jax.ShapeDtypeStruct(qkv.shape, BF16), jax.ShapeDtypeStruct((8, PAIR), F32)],
        compiler_params=_params(("arbitrary", "arbitrary")),
    )(dq, dk, dv, qkv, gq, gk)
    return dx, dg[0:2]


def _relu2(a):
    r = jnp.maximum(a.astype(F32), 0.0)
    return r * r


def dense(a, w, *, nt, name, out_dtype, tm, relu2_in=False, relu2_grad=None, res=None):
    s, k = a.shape
    n = w.shape[0] if nt else w.shape[1]
    nc = min(n, DENSE_CHUNK)

    def body(*refs):
        a_ref, w_ref = refs[0], refs[1]
        o_ref = refs[-1]
        extra = list(refs[2:-1])
        g_ref = extra.pop(0) if relu2_grad is not None else None
        r_ref = extra.pop(0) if res is not None else None
        av = a_ref[...]
        av = _relu2(av).astype(BF16) if relu2_in else av.astype(BF16)
        for c0 in range(0, n, nc):
            cols = slice(c0, c0 + nc)
            acc = _nt(av, w_ref[cols, :]) if nt else _nn(av, w_ref[:, cols])
            if g_ref is not None:
                acc = acc * (2.0 * jnp.maximum(g_ref[:, cols].astype(F32), 0.0))
            if r_ref is not None:
                acc = acc + r_ref[:, cols]
            o_ref[:, cols] = acc.astype(o_ref.dtype)

    oblk = pl.BlockSpec((tm, n), lambda i: (i, 0))
    in_specs = [pl.BlockSpec((tm, k), lambda i: (i, 0)), pl.BlockSpec(w.shape, lambda i: (0, 0))]
    args = [a, w]
    for e in (relu2_grad, res):
        if e is not None:
            in_specs.append(oblk)
            args.append(e)
    return _call(
        body, name=name, grid=(s // tm,), in_specs=in_specs, out_specs=oblk,
        out_shape=jax.ShapeDtypeStruct((s, n), out_dtype),
        compiler_params=_params(("parallel",)),
    )(*args)


def dense_tn(a, b, *, tk, bn, name, relu2_in=False):
    s, k = a.shape
    n = b.shape[1]
    ts = TN_ROWS
    ns = s // ts
    nc = min(bn, DENSE_CHUNK)

    def body(a_ref, b_ref, o_ref, acc_ref):
        t = pl.program_id(2)

        @pl.when(t == 0)
        def _():
            acc_ref[...] = jnp.zeros_like(acc_ref)

        av = a_ref[...]
        av = _relu2(av).astype(BF16) if relu2_in else av.astype(BF16)
        at = av.T
        for c0 in range(0, bn, nc):
            cols = slice(c0, c0 + nc)
            acc_ref[:, cols] += _nn(at, b_ref[:, cols].astype(BF16))

        @pl.when(t == ns - 1)
        def _():
            o_ref[...] = acc_ref[...].astype(o_ref.dtype)

    return _call(
        body, name=name, grid=(k // tk, n // bn, ns),
        in_specs=[pl.BlockSpec((ts, tk), lambda kb, j, t: (t, kb)), pl.BlockSpec((ts, bn), lambda kb, j, t: (t, j))],
        out_specs=pl.BlockSpec((tk, bn), lambda kb, j, t: (kb, j)),
        out_shape=jax.ShapeDtypeStruct((k, n), BF16),
        scratch_shapes=[pltpu.VMEM((tk, bn), F32)],
        compiler_params=_params(("parallel", "parallel", "arbitrary")),
    )(a, b)


def _softplus(z):
    return jnp.maximum(z, 0.0) + jnp.log(1.0 + jnp.exp(-jnp.abs(z)))


def _tri_mats():
    t = SB_TILE
    r = jnp.arange(t)[:, None]
    c = jnp.arange(t)[None, :]
    return jnp.stack([(r > c), (r < c), (r <= c)]).astype(BF16)


def sb_fwd(qkv, tri, name):
    s = qkv.shape[0]
    t = SB_TILE

    def body(q_ref, k_ref, v_ref, u_ref, o_ref, lt_ref, cnt_ref, c0_ref, c1_ref, acc0_ref, acc1_ref):
        p = pl.program_id(0)
        i = pl.program_id(1)
        lane = lax.broadcasted_iota(jnp.int32, (t, PAIR), 1)
        causal = lax.broadcasted_iota(jnp.int32, (t, t), 1) < lax.broadcasted_iota(jnp.int32, (t, t), 0)
        qv = q_ref[...]
        u_after = u_ref[0]
        qhs = [jnp.where(lane < HEAD_DIM, qv, 0) * QK_SCALE, jnp.where(lane >= HEAD_DIM, qv, 0) * QK_SCALE]
        cs = [c0_ref, c1_ref]
        accs = [acc0_ref, acc1_ref]
        for r in cs + accs:
            r[...] = jnp.zeros_like(r)

        def tile(j, masked):
            off = pl.multiple_of(j * t, t)
            kj = k_ref[pl.ds(off, t), :]
            vj = v_ref[pl.ds(off, t), :]
            hs = range(2)
            zs = [_nt(qhs[h], kj) for h in hs]
            ls = [-_softplus(z) for z in zs]
            if masked:
                ls = [jnp.where(causal, l, 0.0) for l in ls]
            splits = [_split_bf16(l) for l in ls]
            afters = [_nn(hi, u_after) + _nn(lo, u_after) for hi, lo in splits]
            a_s = [jnp.exp(zs[h] + ls[h] + afters[h] + cs[h][...]) for h in hs]
            if masked:
                a_s = [jnp.where(causal, a, 0.0) for a in a_s]
            for h in hs:
                accs[h][...] += _nn(a_s[h].astype(BF16), vj)
                cs[h][...] += afters[h][:, 0:1] + ls[h][:, 0:1]

        def alive():
            return jnp.max(jnp.maximum(c0_ref[...], c1_ref[...])) > -SB_DEAD

        tile(i, True)

        def cond(st):
            return (st[0] < i) & st[1]

        def step(st):
            tile(i - 1 - st[0], False)
            return st[0] + 1, alive()

        swept, _ = lax.while_loop(cond, step, (jnp.int32(0), alive()))
        cnt_ref[p, i] = swept
        o_ref[...] = jnp.where(lane < HEAD_DIM, acc0_ref[...], acc1_ref[...]).astype(o_ref.dtype)
        lt_ref[...] = jnp.where(lane == 0, c0_ref[...], jnp.where(lane == 1, c1_ref[...], 0.0))

    return _call(
        body, name=name, grid=(N_PAIRS, s // t),
        in_specs=[
            pl.BlockSpec((t, PAIR), lambda p, i: (i, p)),
            pl.BlockSpec((s, PAIR), lambda p, i: (0, N_PAIRS + p)),
            pl.BlockSpec((s, PAIR), lambda p, i: (0, 2 * N_PAIRS + p)),
            pl.BlockSpec((3, t, t), lambda p, i: (0, 0, 0)),
        ],
        out_specs=[pl.BlockSpec((t, PAIR), lambda p, i: (i, p)), pl.BlockSpec((None, t, PAIR), lambda p, i: (p, i, 0)),
                   pl.BlockSpec(memory_space=pltpu.SMEM)],
        out_shape=[jax.ShapeDtypeStruct((s, D_MODEL), BF16), jax.ShapeDtypeStruct((N_PAIRS, s, PAIR), F32),
                   jax.ShapeDtypeStruct((N_PAIRS, s // t), jnp.int32)],
        scratch_shapes=[pltpu.VMEM((t, 1), F32), pltpu.VMEM((t, 1), F32), pltpu.VMEM((t, PAIR), F32),
                        pltpu.VMEM((t, PAIR), F32)],
        compiler_params=_params(("arbitrary", "arbitrary")),
    )(qkv, qkv, qkv, tri)


def sb_bwd(qkv, do, ltot, swept, tri, name):
    s = qkv.shape[0]
    t = SB_TILE
    nq = s // t

    def body(cnt_ref, q_ref, k_ref, v_ref, do_ref, lt_ref, u_ref, dq_ref, dk_ref, dv_ref,
             pl0_ref, pl1_ref, pg0_ref, pg1_ref, dqa0_ref, dqa1_ref, dka_ref, dva_ref):
        p = pl.program_id(0)
        i = pl.program_id(1)

        @pl.when(i == 0)
        def _():
            dka_ref[...] = jnp.zeros_like(dka_ref)
            dva_ref[...] = jnp.zeros_like(dva_ref)

        lane = lax.broadcasted_iota(jnp.int32, (t, PAIR), 1)
        causal = lax.broadcasted_iota(jnp.int32, (t, t), 1) < lax.broadcasted_iota(jnp.int32, (t, t), 0)
        qv = q_ref[...]
        dov = do_ref[...]
        ltv = lt_ref[...]
        u_before = u_ref[1]
        u_upto = u_ref[2]
        hms = [lane < HEAD_DIM, lane >= HEAD_DIM]
        qhs = [jnp.where(hm, qv, 0) * QK_SCALE for hm in hms]
        dohs = [jnp.where(hm, dov, 0) for hm in hms]
        ltots = [jnp.sum(jnp.where(lane == h, ltv, 0.0), axis=1, keepdims=True) for h in range(2)]
        pls = [pl0_ref, pl1_ref]
        pgs = [pg0_ref, pg1_ref]
        dqas = [dqa0_ref, dqa1_ref]
        for r in pls + pgs + dqas:
            r[...] = jnp.zeros_like(r)

        def tile(j, masked):
            off = pl.multiple_of(j * t, t)
            kj = k_ref[pl.ds(off, t), :]
            vj = v_ref[pl.ds(off, t), :]
            hs = range(2)
            zs = [_nt(qhs[h], kj) for h in hs]
            das = [_nt(dohs[h], vj) for h in hs]
            sps = [_softplus(z) for z in zs]
            ls = [-sp for sp in sps]
            if masked:
                ls = [jnp.where(causal, l, 0.0) for l in ls]
            splits = [_split_bf16(l) for l in ls]
            befores = [_nn(hi, u_before) + _nn(lo, u_before) for hi, lo in splits]
            a_s = [jnp.exp(zs[h] + (ltots[h] - pls[h][...] - befores[h])) for h in hs]
            if masked:
                a_s = [jnp.where(causal, a, 0.0) for a in a_s]
            gs = [a_s[h] * das[h] for h in hs]
            uptos = [_nn(g.astype(BF16), u_upto) for g in gs]
            dzs = [gs[h] - jnp.exp(zs[h] - sps[h]) * (pgs[h][...] + uptos[h]) for h in hs]
            if masked:
                dzs = [jnp.where(causal, dz, 0.0) for dz in dzs]
            dzbs = [dz.astype(BF16) for dz in dzs]
            abs_ = [a.astype(BF16) for a in a_s]
            for h in hs:
                dqas[h][...] += _nn(dzbs[h], kj)
            dka_ref[pl.ds(off, t), :] += _tn(dzbs[0], qhs[0]) + _tn(dzbs[1], qhs[1])
            dva_ref[pl.ds(off, t), :] += _tn(abs_[0], dohs[0]) + _tn(abs_[1], dohs[1])
            for h in hs:
                pls[h][...] += befores[h][:, t - 1:t] + ls[h][:, t - 1:t]
                pgs[h][...] += uptos[h][:, t - 1:t]

        def step(j, carry):
            tile(j, False)
            return carry

        lax.fori_loop(i - cnt_ref[p, i], i, step, 0)
        tile(i, True)
        dq_ref[...] = (jnp.where(lane < HEAD_DIM, dqa0_ref[...], dqa1_ref[...]) * QK_SCALE).astype(dq_ref.dtype)

        @pl.when(i == nq - 1)
        def _():
            dk_ref[...] = dka_ref[...].astype(dk_ref.dtype)
            dv_ref[...] = dva_ref[...].astype(dv_ref.dtype)

    qblk = pl.BlockSpec((t, PAIR), lambda p, i: (i, p))
    full = pl.BlockSpec((s, PAIR), lambda p, i: (0, p))
    out = jax.ShapeDtypeStruct((s, D_MODEL), BF16)
    return _call(
        body, name=name, grid=(N_PAIRS, nq),
        in_specs=[
            pl.BlockSpec(memory_space=pltpu.SMEM),
            qblk,
            pl.BlockSpec((s, PAIR), lambda p, i: (0, N_PAIRS + p)),
            pl.BlockSpec((s, PAIR), lambda p, i: (0, 2 * N_PAIRS + p)),
            qblk,
            pl.BlockSpec((None, t, PAIR), lambda p, i: (p, i, 0)),
            pl.BlockSpec((3, t, t), lambda p, i: (0, 0, 0)),
        ],
        out_specs=[qblk, full, full],
        out_shape=[out, out, out],
        scratch_shapes=[pltpu.VMEM((t, 1), F32)] * 4 + [pltpu.VMEM((t, PAIR), F32)] * 2
        + [pltpu.VMEM((s, PAIR), F32), pltpu.VMEM((s, PAIR), F32)],
        compiler_params=_params(("arbitrary", "arbitrary")),
    )(swept, qkv, qkv, qkv, do, ltot, tri)


def _ring_onehot():
    r = lax.broadcasted_iota(jnp.int32, (N_REL_PAD, CA_RING), 0)
    w = lax.broadcasted_iota(jnp.int32, (N_REL_PAD, CA_RING), 1)
    idx = jnp.where(w <= CA_WIN, jnp.clip(CA_WIN - w, 0, 2 * MAX_REL), 2 * MAX_REL)
    return (r == idx).astype(F32)


def bias_table(rel_pad, name):
    def body(rb_ref, o_ref):
        ring = jnp.dot(rb_ref[...], _ring_onehot(), preferred_element_type=F32, precision=lax.Precision.HIGHEST)
        col = lax.broadcasted_iota(jnp.int32, (N_HEADS, CA_WIN), 1)

        def row(tl, carry):
            v = pltpu.roll(ring, tl, 1)[:, :CA_WIN]
            first = (tl // CHUNK) * CHUNK
            ok = (col >= first) & (col < first + (LEFT_CHUNKS + 1) * CHUNK)
            o_ref[tl] = jnp.where(ok, v, NEG_BIG)
            return carry

        lax.fori_loop(0, CA_QBLK, row, 0)

    return _call(
        body, name=name,
        in_specs=[pl.BlockSpec(memory_space=pltpu.VMEM)], out_specs=pl.BlockSpec(memory_space=pltpu.VMEM),
        out_shape=jax.ShapeDtypeStruct((CA_QBLK, N_HEADS, CA_WIN), F32),
        compiler_params=_params(),
    )(rel_pad)


def bias_fold(db, name):
    def body(db_ref, o_ref):
        zeros = jnp.zeros((N_HEADS, CA_RING - CA_WIN), F32)

        def row(tl, acc):
            v = jnp.concatenate([db_ref[tl], zeros], axis=1)
            return acc + pltpu.roll(v, (CA_RING - tl) % CA_RING, 1)

        ring = lax.fori_loop(0, CA_QBLK, row, jnp.zeros((N_HEADS, CA_RING), F32))
        o_ref[...] = lax.dot_general(ring, _ring_onehot(), (((1,), (1,)), ((), ())),
                                     preferred_element_type=F32, precision=lax.Precision.HIGHEST)

    return _call(
        body, name=name,
        in_specs=[pl.BlockSpec(memory_space=pltpu.VMEM)], out_specs=pl.BlockSpec(memory_space=pltpu.VMEM),
        out_shape=jax.ShapeDtypeStruct((N_HEADS, N_REL_PAD), F32),
        compiler_params=_params(),
    )(db)


def _ca_logits(qh, kw, bias_h, first_key_ok):
    sc = _nt(qh, kw) + bias_h
    col = lax.broadcasted_iota(jnp.int32, sc.shape, 1)
    return jnp.where(col >= first_key_ok, sc, NEG_BIG)


def ca_fwd(qkn, qkv, table, name):
    s = qkv.shape[0]
    tq = CA_QBLK

    def body(q_ref, k_ref, v_ref, b_ref, o_ref, kp_ref, vp_ref):
        i = pl.program_id(1)

        @pl.when(i == 0)
        def _():
            kp_ref[0:CA_PAD, :] = jnp.zeros((CA_PAD, PAIR), BF16)
            vp_ref[0:CA_PAD, :] = jnp.zeros((CA_PAD, PAIR), BF16)
            kp_ref[CA_PAD:, :] = k_ref[...]
            vp_ref[CA_PAD:, :] = v_ref[...]

        lane = lax.broadcasted_iota(jnp.int32, (tq, PAIR), 1)
        off = pl.multiple_of(i * tq, tq)
        kw = kp_ref[pl.ds(off, CA_WIN), :]
        vw = vp_ref[pl.ds(off, CA_WIN), :]
        qv = q_ref[...]
        first_ok = CA_PAD - i * tq
        outs = []
        for h in range(2):
            hm = (lane < HEAD_DIM) if h == 0 else (lane >= HEAD_DIM)
            qh = jnp.where(hm, qv, 0) * QK_SCALE
            sc = _ca_logits(qh, kw, b_ref[h], first_ok)
            m = jnp.max(sc, axis=1, keepdims=True)
            e = jnp.exp(sc - m)
            den = jnp.sum(e, axis=1, keepdims=True)
            outs.append(_nn((e / den).astype(BF16), vw))
        o_ref[...] = jnp.where(lane < HEAD_DIM, outs[0], outs[1]).astype(o_ref.dtype)

    return _call(
        body, name=name, grid=(N_PAIRS, s // tq),
        in_specs=[
            pl.BlockSpec((tq, PAIR), lambda p, i: (i, p)),
            pl.BlockSpec((s, PAIR), lambda p, i: (0, N_PAIRS + p)),
            pl.BlockSpec((s, PAIR), lambda p, i: (0, 2 * N_PAIRS + p)),
            pl.BlockSpec((2, tq, CA_WIN), lambda p, i: (p, 0, 0)),
        ],
        out_specs=pl.BlockSpec((tq, PAIR), lambda p, i: (i, p)),
        out_shape=jax.ShapeDtypeStruct((s, D_MODEL), BF16),
        scratch_shapes=[pltpu.VMEM((s + CA_PAD, PAIR), BF16), pltpu.VMEM((s + CA_PAD, PAIR), BF16)],
        compiler_params=_params(("parallel", "arbitrary")),
    )(qkn, qkn, qkv, table)


def ca_bwd(qkn, qkv, o, do, table, name):
    s = qkv.shape[0]
    tq = CA_QBLK
    nq = s // tq

    def body(q_ref, k_ref, v_ref, o_ref, do_ref, b_ref, dq_ref, dk_ref, dv_ref, db_ref,
             kp_ref, vp_ref, dka_ref, dva_ref):
        i = pl.program_id(1)

        @pl.when(i == 0)
        def _():
            kp_ref[0:CA_PAD, :] = jnp.zeros((CA_PAD, PAIR), BF16)
            vp_ref[0:CA_PAD, :] = jnp.zeros((CA_PAD, PAIR), BF16)
            kp_ref[CA_PAD:, :] = k_ref[...]
            vp_ref[CA_PAD:, :] = v_ref[...]
            dka_ref[...] = jnp.zeros_like(dka_ref)
            dva_ref[...] = jnp.zeros_like(dva_ref)
            db_ref[...] = jnp.zeros_like(db_ref)

        lane = lax.broadcasted_iota(jnp.int32, (tq, PAIR), 1)
        off = pl.multiple_of(i * tq, tq)
        kw = kp_ref[pl.ds(off, CA_WIN), :]
        vw = vp_ref[pl.ds(off, CA_WIN), :]
        qv = q_ref[...]
        dov = do_ref[...]
        prod = dov.astype(F32) * o_ref[...].astype(F32)
        first_ok = CA_PAD - i * tq
        dqs = []
        for h in range(2):
            hm = (lane < HEAD_DIM) if h == 0 else (lane >= HEAD_DIM)
            qh = jnp.where(hm, qv, 0) * QK_SCALE
            doh = jnp.where(hm, dov, 0)
            delta = jnp.sum(jnp.where(hm, prod, 0.0), axis=1, keepdims=True)
            sc = _ca_logits(qh, kw, b_ref[h], first_ok)
            m = jnp.max(sc, axis=1, keepdims=True)
            e = jnp.exp(sc - m)
            p = e / jnp.sum(e, axis=1, keepdims=True)
            ds = p * (_nt(doh, vw) - delta)
            db_ref[h] += ds
            dsb = ds.astype(BF16)
            dqs.append(_nn(dsb, kw) * QK_SCALE)
            dka_ref[pl.ds(off, CA_WIN), :] += _tn(dsb, qh)
            dva_ref[pl.ds(off, CA_WIN), :] += _tn(p.astype(BF16), doh)
        dq_ref[...] = jnp.where(lane < HEAD_DIM, dqs[0], dqs[1]).astype(dq_ref.dtype)

        @pl.when(i == nq - 1)
        def _():
            dk_ref[...] = dka_ref[CA_PAD:, :].astype(dk_ref.dtype)
            dv_ref[...] = dva_ref[CA_PAD:, :].astype(dv_ref.dtype)

    qblk = pl.BlockSpec((tq, PAIR), lambda p, i: (i, p))
    full = pl.BlockSpec((s, PAIR), lambda p, i: (0, p))
    tblk = pl.BlockSpec((2, tq, CA_WIN), lambda p, i: (p, 0, 0))
    out = jax.ShapeDtypeStruct((s, D_MODEL), BF16)
    return _call(
        body, name=name, grid=(N_PAIRS, nq),
        in_specs=[
            qblk,
            pl.BlockSpec((s, PAIR), lambda p, i: (0, N_PAIRS + p)),
            pl.BlockSpec((s, PAIR), lambda p, i: (0, 2 * N_PAIRS + p)),
            qblk, qblk, tblk,
        ],
        out_specs=[qblk, full, full, tblk],
        out_shape=[out, out, out, jax.ShapeDtypeStruct((N_HEADS, tq, CA_WIN), F32)],
        scratch_shapes=[pltpu.VMEM((s + CA_PAD, PAIR), BF16), pltpu.VMEM((s + CA_PAD, PAIR), BF16),
                        pltpu.VMEM((s + CA_PAD, PAIR), F32), pltpu.VMEM((s + CA_PAD, PAIR), F32)],
        compiler_params=_params(("parallel", "arbitrary")),
    )(qkn, qkn, qkv, o, do, table)


def _me():
    x, y, c = lax.axis_index("x"), lax.axis_index("y"), lax.axis_index("c")
    return x, y, c, 4 * x + 2 * y + c


def _peer(k):
    x, y, c, _ = _me()
    px, py, pc = x ^ ((k >> 2) & 1), y ^ ((k >> 1) & 1), c ^ (k & 1)
    return (px, py, pc), 4 * px + 2 * py + pc


ANY = pl.BlockSpec(memory_space=pl.ANY)


def _exchange(n, copy, name, in_arrays, out_shape):
    n_in = len(in_arrays)

    def body(*refs):
        data = refs[:n_in + len(out_shape)]
        send_sems, recv_sems, local_sems = refs[n_in + len(out_shape):]
        _, _, _, me = _me()
        local = []
        for a in range(n):
            s_ref, d_ref = copy(data, a, me, me)
            local.append(pltpu.make_async_copy(s_ref, d_ref, local_sems.at[a]))
            local[-1].start()
        sends = []
        for a in range(n):
            for k in range(1, N_DEV):
                peer, pidx = _peer(k)
                s_ref, d_ref = copy(data, a, me, pidx)
                cp = pltpu.make_async_remote_copy(
                    src_ref=s_ref, dst_ref=d_ref, send_sem=send_sems.at[a, k - 1], recv_sem=recv_sems.at[a, k - 1],
                    device_id=peer, device_id_type=MESH)
                cp.start()
                sends.append(cp)
        for a in range(n):
            for k in range(1, N_DEV):
                peer, pidx = _peer(k)
                s_ref, d_ref = copy(data, a, pidx, me)
                pltpu.make_async_remote_copy(
                    src_ref=s_ref, dst_ref=d_ref, send_sem=send_sems.at[a, k - 1], recv_sem=recv_sems.at[a, k - 1],
                    device_id=peer, device_id_type=MESH).wait_recv()
        for cp in sends:
            cp.wait_send()
        for cp in local:
            cp.wait()

    return _call(
        body, name=name,
        in_specs=[ANY] * n_in, out_specs=[ANY] * len(out_shape), out_shape=out_shape,
        scratch_shapes=[pltpu.SemaphoreType.DMA((n, N_DEV - 1)), pltpu.SemaphoreType.DMA((n, N_DEV - 1)),
                        pltpu.SemaphoreType.DMA((n,))],
        compiler_params=pltpu.CompilerParams(has_side_effects=True),
    )(*in_arrays)


def _window(full_ref, shard_shape, idx):
    a, b = shard_shape
    if a == full_ref.shape[0]:
        return full_ref.at[:, pl.ds(pl.multiple_of(idx * b, 128), b)]
    return full_ref.at[pl.ds(pl.multiple_of(idx * a, 8), a), :]


def gather_weights(shards, full_shapes, name):
    nw = len(shards)

    def copy(data, a, src_idx, dst_idx):
        w, layer = divmod(a, DEPTH)
        s_ref = data[w].at[layer]
        return s_ref, _window(data[nw + a], s_ref.shape, src_idx)

    out_shape = [jax.ShapeDtypeStruct(full_shapes[w], shards[w].dtype) for w in range(nw) for _ in range(DEPTH)]
    return _exchange(nw * DEPTH, copy, name, shards, out_shape)


def scatter_grads(partials, shard_shapes, name):
    nw = len(shard_shapes)
    n = len(partials)

    def copy(data, a, src_idx, dst_idx):
        w, layer = divmod(a, DEPTH)
        return _window(data[a], shard_shapes[w], dst_idx), data[n + w].at[src_idx, layer]

    out_shape = [jax.ShapeDtypeStruct((N_DEV, DEPTH) + tuple(shard_shapes[w]), partials[w * DEPTH].dtype)
                 for w in range(nw)]
    return _exchange(n, copy, name, partials, out_shape)


def _adamw(w, g, m, v):
    m = ADAM_B1 * m + (1.0 - ADAM_B1) * g
    v = ADAM_B2 * v + (1.0 - ADAM_B2) * (g * g)
    m_hat = m / (1.0 - ADAM_B1 ** ADAM_STEP)
    v_hat = v / (1.0 - ADAM_B2 ** ADAM_STEP)
    delta = -ADAM_LR * (m_hat / (jnp.sqrt(v_hat) + ADAM_EPS) + ADAM_WD * w)
    return delta, m, v


def adam_shard(parts, w, m, v, name):
    r, c = w.shape
    tr = 256

    def body(p_ref, w_ref, m_ref, v_ref, g_out, d_out, m_out, v_out):
        g = p_ref[0].astype(F32)
        for e in range(1, N_DEV):
            g = g + p_ref[e].astype(F32)
        delta, mn, vn = _adamw(w_ref[...], g, m_ref[...], v_ref[...])
        g_out[...] = g
        d_out[...] = delta
        m_out[...] = mn
        v_out[...] = vn

    blk = pl.BlockSpec((tr, c), lambda i: (i, 0))
    out = jax.ShapeDtypeStruct((r, c), F32)
    return _call(
        body, name=name, grid=(r // tr,),
        in_specs=[pl.BlockSpec((N_DEV, tr, c), lambda i: (0, i, 0)), blk, blk, blk],
        out_specs=[blk] * 4, out_shape=[out] * 4,
        compiler_params=_params(("parallel",)),
    )(parts, w, m, v)


def small_allreduce_adam(pk, w, m, v, name):
    def body(pk_ref, w_ref, m_ref, v_ref, g_out, d_out, m_out, v_out, all_ref, send_sems, recv_sems):
        _, _, _, me = _me()
        all_ref[me] = pk_ref[...]
        sends = []
        for k in range(1, N_DEV):
            peer, _ = _peer(k)
            cp = pltpu.make_async_remote_copy(
                src_ref=pk_ref, dst_ref=all_ref.at[me], send_sem=send_sems.at[k - 1],
                recv_sem=recv_sems.at[k - 1], device_id=peer, device_id_type=MESH)
            cp.start()
            sends.append(cp)
        for k in range(1, N_DEV):
            peer, pidx = _peer(k)
            pltpu.make_async_remote_copy(
                src_ref=pk_ref, dst_ref=all_ref.at[pidx], send_sem=send_sems.at[k - 1],
                recv_sem=recv_sems.at[k - 1], device_id=peer, device_id_type=MESH).wait_recv()
        for cp in sends:
            cp.wait_send()
        g = all_ref[0]
        for e in range(1, N_DEV):
            g = g + all_ref[e]
        delta, mn, vn = _adamw(w_ref[...], g, m_ref[...], v_ref[...])
        g_out[...] = g
        d_out[...] = delta
        m_out[...] = mn
        v_out[...] = vn

    vm = pl.BlockSpec(memory_space=pltpu.VMEM)
    out = jax.ShapeDtypeStruct((PACK_ROWS, 128), F32)
    return _call(
        body, name=name,
        in_specs=[vm] * 4, out_specs=[vm] * 4, out_shape=[out] * 4,
        scratch_shapes=[pltpu.VMEM((N_DEV, PACK_ROWS, 128), F32), pltpu.SemaphoreType.DMA((N_DEV - 1,)),
                        pltpu.SemaphoreType.DMA((N_DEV - 1,))],
        compiler_params=pltpu.CompilerParams(has_side_effects=True),
    )(pk, w, m, v)


def _pack_small(mix, ffn, qn, kn, rel, loss):
    flat = jnp.concatenate([mix.reshape(-1), ffn.reshape(-1), qn.reshape(-1), kn.reshape(-1), rel.reshape(-1),
                            loss.reshape(-1)])
    return jnp.pad(flat, (0, PACK_LEN - flat.shape[0])).reshape(PACK_ROWS, 128)


def _unpack_small(pk):
    flat = pk.reshape(-1)
    return (flat[OFF_MIX:OFF_FFN].reshape(DEPTH, D_MODEL), flat[OFF_FFN:OFF_QN].reshape(DEPTH, D_MODEL),
            flat[OFF_QN:OFF_KN].reshape(2, HEAD_DIM), flat[OFF_KN:OFF_REL].reshape(2, HEAD_DIM),
            flat[OFF_REL:OFF_LOSS].reshape(2, N_HEADS, N_REL), flat[OFF_LOSS])


def _pair_gain(g):
    return jnp.concatenate([g, g]).reshape(1, PAIR)


def kernel(x, mix_norm, w_qkv, w_o, q_norm, k_norm, rel_bias, ffn_norm, w_up, w_down, loss_target, m_mix_norm, m_w_qkv, m_w_o, m_q_norm, m_k_norm, m_rel_bias, m_ffn_norm, m_w_up, m_w_down, v_mix_norm, v_w_qkv, v_w_o, v_q_norm, v_k_norm, v_rel_bias, v_ffn_norm, v_w_up, v_w_down):
    x0 = x[0]
    target = loss_target[0]
    shard_shapes = [w_qkv.shape[1:], w_o.shape[1:], w_up.shape[1:], w_down.shape[1:]]
    full_shapes = [(D_MODEL, 3 * D_MODEL), (D_MODEL, D_MODEL), (D_MODEL, D_FF), (D_FF, D_MODEL)]

    gathered = gather_weights([w_qkv.astype(BF16), w_o.astype(BF16), w_up.astype(BF16), w_down.astype(BF16)],
                              full_shapes, "gather_weights")
    g_qkv, g_o, g_up, g_down = (gathered[w * DEPTH:(w + 1) * DEPTH] for w in range(4))
    tri = _tri_mats()

    saved = []
    xin = x0
    for layer in range(DEPTH):
        mixer_b = layer % 2 == 1
        idx = layer // 2
        h0 = rms_fwd(xin, mix_norm[layer:layer + 1], f"rms_mix_{layer}")
        qkv_raw = dense(h0, g_qkv[layer], nt=False, name=f"qkv_{layer}", out_dtype=BF16, tm=1024)
        if mixer_b:
            gq, gk = _pair_gain(q_norm[idx]), _pair_gain(k_norm[idx])
            qkn = headnorm_fwd(qkv_raw, gq, gk, f"headnorm_{layer}")
            rel_pad = jnp.pad(rel_bias[idx], ((0, 0), (0, N_REL_PAD - N_REL)))
            table = jnp.transpose(bias_table(rel_pad, f"bias_table_{layer}"), (1, 0, 2))
            o = ca_fwd(qkn, qkv_raw, table, f"ca_fwd_{layer}")
            attn_saved = (qkv_raw, qkn, table)
        else:
            o, ltot, swept = sb_fwd(qkv_raw, tri, f"sb_fwd_{layer}")
            attn_saved = (qkv_raw, ltot, swept)
        x1 = dense(o, g_o[layer], nt=False, name=f"attn_out_{layer}", out_dtype=F32, tm=1024, res=xin)
        h1 = rms_fwd(x1, ffn_norm[layer:layer + 1], f"rms_ffn_{layer}")
        pre = dense(h1, g_up[layer], nt=False, name=f"up_{layer}", out_dtype=BF16, tm=1024)
        x2 = dense(pre, g_down[layer], nt=False, name=f"down_{layer}", out_dtype=F32, tm=512, relu2_in=True, res=x1)
        saved.append((xin, h0, attn_saved, o, x1, h1, pre))
        xin = x2

    dx, loss_blk = loss_head(xin, target, "loss_head")

    p_qkv, p_o, p_up, p_down = [None] * DEPTH, [None] * DEPTH, [None] * DEPTH, [None] * DEPTH
    d_mix, d_ffn = [None] * DEPTH, [None] * DEPTH
    d_qn, d_kn, d_rel = [None] * 2, [None] * 2, [None] * 2
    for layer in reversed(range(DEPTH)):
        mixer_b = layer % 2 == 1
        idx = layer // 2
        xin, h0, attn_saved, o, x1, h1, pre = saved[layer]
        dpre = dense(dx, g_down[layer], nt=True, name=f"d_pre_{layer}", out_dtype=BF16, tm=512, relu2_grad=pre)
        p_down[layer] = dense_tn(pre, dx, tk=2048, bn=1024, name=f"dw_down_{layer}", relu2_in=True)
        p_up[layer] = dense_tn(h1, dpre, tk=1024, bn=2048, name=f"dw_up_{layer}")
        dh1 = dense(dpre, g_up[layer], nt=True, name=f"d_h1_{layer}", out_dtype=F32, tm=512)
        dx, d_ffn[layer] = rms_bwd(dh1, x1, ffn_norm[layer:layer + 1], dx, f"rms_ffn_bwd_{layer}")
        do = dense(dx, g_o[layer], nt=True, name=f"d_o_{layer}", out_dtype=BF16, tm=1024)
        p_o[layer] = dense_tn(o, dx, tk=1024, bn=1024, name=f"dw_o_{layer}")
        if mixer_b:
            qkv_raw, qkn, table = attn_saved
            dq, dk, dv, dtab = ca_bwd(qkn, qkv_raw, o, do, table, f"ca_bwd_{layer}")
            gq, gk = _pair_gain(q_norm[idx]), _pair_gain(k_norm[idx])
            dqkv, dgain = headnorm_bwd(dq, dk, dv, qkv_raw, gq, gk, f"headnorm_bwd_{layer}")
            d_qn[idx] = dgain[0, :HEAD_DIM] + dgain[0, HEAD_DIM:]
            d_kn[idx] = dgain[1, :HEAD_DIM] + dgain[1, HEAD_DIM:]
            d_rel[idx] = bias_fold(jnp.transpose(dtab, (1, 0, 2)), f"bias_fold_{layer}")[:, :N_REL]
        else:
            qkv_raw, ltot, swept = attn_saved
            dq, dk, dv = sb_bwd(qkv_raw, do, ltot, swept, tri, f"sb_bwd_{layer}")
            dqkv = jnp.concatenate([dq, dk, dv], axis=1)
        p_qkv[layer] = dense_tn(h0, dqkv, tk=1024, bn=1536, name=f"dw_qkv_{layer}")
        dh0 = dense(dqkv, g_qkv[layer], nt=True, name=f"d_h0_{layer}", out_dtype=F32, tm=1024)
        dx, d_mix[layer] = rms_bwd(dh0, xin, mix_norm[layer:layer + 1], dx, f"rms_mix_bwd_{layer}")

    grad_x = dx[None]

    r_qkv, r_o, r_up, r_down = scatter_grads(p_qkv + p_o + p_up + p_down, shard_shapes, "scatter_grads")

    def big(parts, w, m, v, name):
        shp = w.shape
        c = shp[-1]
        outs = adam_shard(parts.reshape(N_DEV, -1, c), w.reshape(-1, c), m.reshape(-1, c), v.reshape(-1, c), name)
        return [t.reshape(shp) for t in outs]

    a_qkv = big(r_qkv, w_qkv, m_w_qkv, v_w_qkv, "adam_qkv")
    a_o = big(r_o, w_o, m_w_o, v_w_o, "adam_o")
    a_up = big(r_up, w_up, m_w_up, v_w_up, "adam_up")
    a_down = big(r_down, w_down, m_w_down, v_w_down, "adam_down")

    pk = _pack_small(jnp.concatenate(d_mix), jnp.concatenate(d_ffn), jnp.stack(d_qn), jnp.stack(d_kn),
                     jnp.stack(d_rel), loss_blk[0, 0])
    zero = jnp.zeros((), F32)
    pw = _pack_small(mix_norm, ffn_norm, q_norm, k_norm, rel_bias, zero)
    pm = _pack_small(m_mix_norm, m_ffn_norm, m_q_norm, m_k_norm, m_rel_bias, zero)
    pv = _pack_small(v_mix_norm, v_ffn_norm, v_q_norm, v_k_norm, v_rel_bias, zero)
    s_g, s_d, s_m, s_v = small_allreduce_adam(pk, pw, pm, pv, "small_allreduce_adam")
    g_mix, g_ffn, g_qn, g_kn, g_rel, loss = _unpack_small(s_g)
    dl_mix, dl_ffn, dl_qn, dl_kn, dl_rel, _ = _unpack_small(s_d)
    nm_mix, nm_ffn, nm_qn, nm_kn, nm_rel, _ = _unpack_small(s_m)
    nv_mix, nv_ffn, nv_qn, nv_kn, nv_rel, _ = _unpack_small(s_v)

    return (loss, grad_x,
            g_mix, a_qkv[0], a_o[0], g_qn, g_kn, g_rel, g_ffn, a_up[0], a_down[0],
            dl_mix, a_qkv[1], a_o[1], dl_qn, dl_kn, dl_rel, dl_ffn, a_up[1], a_down[1],
            nm_mix, a_qkv[2], a_o[2], nm_qn, nm_kn, nm_rel, nm_ffn, a_up[2], a_down[2],
            nv_mix, a_qkv[3], a_o[3], nv_qn, nv_kn, nv_rel, nv_ffn, a_up[3], a_down[3])
```

```python
import jax
import jax.numpy as jnp
from jax import lax
from jax.experimental import pallas as pl
from jax.experimental.pallas import tpu as pltpu

F32 = jnp.float32
BF16 = jnp.bfloat16
MESH = pl.DeviceIdType.MESH

D_MODEL = 1024
N_HEADS = 16
HEAD_DIM = 64
PAIR = 2 * HEAD_DIM
N_PAIRS = N_HEADS // 2
D_FF = 4 * D_MODEL
DEPTH = 4
N_DEV = 8
RMS_EPS = 1e-6
QK_SCALE = HEAD_DIM ** -0.5

SB_TILE = 256
SB_DEAD = 104.0

CHUNK = 64
LEFT_CHUNKS = 8
CA_QBLK = 256
CA_PAD = LEFT_CHUNKS * CHUNK
CA_WIN = CA_QBLK + CA_PAD
CA_SUB = 128
CA_SUBWIN = CA_SUB + CA_PAD
CA_RING = 1024
MAX_REL = 256
N_REL = 2 * MAX_REL + 1
N_REL_PAD = 640
NEG_BIG = -1e30

ADAM_LR = 0.001
ADAM_B1 = 0.9
ADAM_B2 = 0.999
ADAM_EPS = 1e-08
ADAM_WD = 0.01
ADAM_STEP = 10

ROW_TILE = 512
DENSE_CHUNK = 512
TN_ROWS = 1024
VMEM_LIMIT = 56 * 1024 * 1024

OFF_MIX = 0
OFF_FFN = OFF_MIX + DEPTH * D_MODEL
OFF_QN = OFF_FFN + DEPTH * D_MODEL
OFF_KN = OFF_QN + 2 * HEAD_DIM
OFF_REL = OFF_KN + 2 * HEAD_DIM
OFF_LOSS = OFF_REL + 2 * N_HEADS * N_REL
PACK_ROWS = 200
PACK_LEN = PACK_ROWS * 128


def _call(body, **kw):
    return pl.pallas_call(body, **kw)


def _params(sem=None, vmem=VMEM_LIMIT):
    if sem is None:
        return pltpu.CompilerParams(vmem_limit_bytes=vmem)
    return pltpu.CompilerParams(dimension_semantics=sem, vmem_limit_bytes=vmem)


def _nt(a, b):
    return lax.dot_general(a, b, (((1,), (1,)), ((), ())), preferred_element_type=F32)


def _tn(a, b):
    return lax.dot_general(a, b, (((0,), (0,)), ((), ())), preferred_element_type=F32)


def _nn(a, b):
    return jnp.dot(a, b, preferred_element_type=F32)


def _split_bf16(v):
    hi = v.astype(BF16)
    lo = (v - hi.astype(F32)).astype(BF16)
    return hi, lo


def rms_fwd(x, g, name):
    s = x.shape[0]
    tm = ROW_TILE

    def body(x_ref, g_ref, o_ref):
        xv = x_ref[...]
        r = lax.rsqrt(jnp.mean(xv * xv, axis=-1, keepdims=True) + RMS_EPS)
        o_ref[...] = (xv * r * g_ref[...]).astype(o_ref.dtype)

    return _call(
        body, name=name, grid=(s // tm,),
        in_specs=[pl.BlockSpec((tm, D_MODEL), lambda i: (i, 0)), pl.BlockSpec((1, D_MODEL), lambda i: (0, 0))],
        out_specs=pl.BlockSpec((tm, D_MODEL), lambda i: (i, 0)),
        out_shape=jax.ShapeDtypeStruct((s, D_MODEL), BF16),
        compiler_params=_params(("parallel",)),
    )(x, g)


def loss_head(y, target, name):
    s = y.shape[0]
    tm = ROW_TILE

    def body(y_ref, t_ref, dy_ref, l_ref):
        i = pl.program_id(0)
        e = y_ref[...] - t_ref[...]
        dy_ref[...] = e * (1.0 / D_MODEL)

        @pl.when(i == 0)
        def _():
            l_ref[...] = jnp.zeros_like(l_ref)

        per_row = jnp.sum(e * e, axis=-1, keepdims=True) * (1.0 / D_MODEL)
        l_ref[...] += jnp.broadcast_to(0.5 * jnp.sum(per_row, axis=0, keepdims=True), l_ref.shape)

    row = pl.BlockSpec((tm, D_MODEL), lambda i: (i, 0))
    return _call(
        body, name=name, grid=(s // tm,),
        in_specs=[row, row],
        out_specs=[row, pl.BlockSpec((8, 128), lambda i: (0, 0))],
        out_shape=[jax.ShapeDtypeStruct((s, D_MODEL), F32), jax.ShapeDtypeStruct((8, 128), F32)],
        compiler_params=_params(("arbitrary",)),
    )(y, target)


def _group_sum_matrix():
    r = lax.broadcasted_iota(jnp.int32, (PAIR, PAIR), 0) // HEAD_DIM
    c = lax.broadcasted_iota(jnp.int32, (PAIR, PAIR), 1) // HEAD_DIM
    return (r == c).astype(BF16)


def _head_mean(v, gmat):
    hi, lo = _split_bf16(v)
    return (_nn(hi, gmat) + _nn(lo, gmat)) * (1.0 / HEAD_DIM)


def headnorm_fwd(qkv, gq, gk, name):
    s = qkv.shape[0]
    tm = ROW_TILE

    def body(x_ref, gq_ref, gk_ref, o_ref):
        j = pl.program_id(1)
        gmat = _group_sum_matrix()
        gain = jnp.where(j == 0, gq_ref[...], gk_ref[...])
        for b in range(N_PAIRS):
            cols = slice(b * PAIR, (b + 1) * PAIR)
            xv = x_ref[:, cols].astype(F32)
            r = lax.rsqrt(_head_mean(xv * xv, gmat) + RMS_EPS)
            o_ref[:, cols] = (xv * r * gain).astype(o_ref.dtype)

    blk = pl.BlockSpec((tm, D_MODEL), lambda i, j: (i, j))
    gspec = pl.BlockSpec((1, PAIR), lambda i, j: (0, 0))
    return _call(
        body, name=name, grid=(s // tm, 2),
        in_specs=[blk, gspec, gspec], out_specs=blk,
        out_shape=jax.ShapeDtypeStruct((s, 2 * D_MODEL), BF16),
        compiler_params=_params(("parallel", "parallel")),
    )(qkv, gq, gk)


def headnorm_bwd(dq, dk, dv, qkv, gq, gk, name):
    s = qkv.shape[0]
    tm = ROW_TILE

    def body(dq_ref, dk_ref, dv_ref, x_ref, gq_ref, gk_ref, dx_ref, dg_ref):
        i = pl.program_id(0)
        j = pl.program_id(1)

        @pl.when((i == 0) & (j == 0))
        def _():
            dg_ref[...] = jnp.zeros_like(dg_ref)

        @pl.when(j == 2)
        def _():
            dx_ref[...] = dv_ref[...]

        for part, (d_ref, g_ref) in enumerate(((dq_ref, gq_ref), (dk_ref, gk_ref))):
            @pl.when(j == part)
            def _(part=part, d_ref=d_ref, g_ref=g_ref):
                gmat = _group_sum_matrix()
                gain = g_ref[...]
                dg = jnp.zeros((1, PAIR), F32)
                for b in range(N_PAIRS):
                    cols = slice(b * PAIR, (b + 1) * PAIR)
                    xv = x_ref[:, cols].astype(F32)
                    dn = d_ref[:, cols].astype(F32)
                    r = lax.rsqrt(_head_mean(xv * xv, gmat) + RMS_EPS)
                    xh = xv * r
                    dy = dn * gain
                    dx_ref[:, cols] = (r * (dy - xh * _head_mean(dy * xh, gmat))).astype(dx_ref.dtype)
                    dg = dg + jnp.sum(dn * xh, axis=0, keepdims=True)
                dg_ref[part:part + 1, :] += dg

    row = pl.BlockSpec((tm, D_MODEL), lambda i, j: (i, 0))
    blk = pl.BlockSpec((tm, D_MODEL), lambda i, j: (i, j))
    gspec = pl.BlockSpec((1, PAIR), lambda i, j: (0, 0))
    dx, dg = _call(
        body, name=name, grid=(s // tm, 3),
        in_specs=[row, row, row, blk, gspec, gspec],
        out_specs=[blk, pl.BlockSpec((8, PAIR), lambda i, j: (0, 0))],
        out_shape=[jax.ShapeDtypeStruct(qkv.shape, BF16), jax.ShapeDtypeStruct((8, PAIR), F32)],
        compiler_params=_params(("arbitrary", "arbitrary")),
    )(dq, dk, dv, qkv, gq, gk)
    return dx, dg[0:2]


def _relu2(a):
    r = jnp.maximum(a.astype(F32), 0.0)
    return r * r


def dense(a, w, *, nt, name, out_dtype, tm, relu2_in=False, relu2_grad=None, res=None, norm_gain=None,
          rms_back=None):
    s, k = a.shape
    n = w.shape[0] if nt else w.shape[1]
    nc = min(n, DENSE_CHUNK)

    def body(*refs):
        it = iter(refs)
        a_ref, w_ref = next(it), next(it)
        g_ref = next(it) if relu2_grad is not None else None
        r_ref = next(it) if res is not None else None
        ng_ref = next(it) if norm_gain is not None else None
        if rms_back is not None:
            x_ref, gain_ref, dres_ref = next(it), next(it), next(it)
        o_ref = next(it)
        h_ref = next(it) if norm_gain is not None else None
        dg_ref = next(it) if rms_back is not None else None
        av = a_ref[...]
        av = _relu2(av).astype(BF16) if relu2_in else av.astype(BF16)
        for c0 in range(0, n, nc):
            cols = slice(c0, c0 + nc)
            acc = _nt(av, w_ref[cols, :]) if nt else _nn(av, w_ref[:, cols])
            if g_ref is not None:
                acc = acc * (2.0 * jnp.maximum(g_ref[:, cols].astype(F32), 0.0))
            if r_ref is not None:
                acc = acc + r_ref[:, cols]
            o_ref[:, cols] = acc.astype(o_ref.dtype)
        if norm_gain is not None:
            xv = o_ref[...]
            r = lax.rsqrt(jnp.mean(xv * xv, axis=-1, keepdims=True) + RMS_EPS)
            h_ref[...] = (xv * r * ng_ref[...]).astype(h_ref.dtype)
        if rms_back is not None:
            dhv = o_ref[...]
            xv = x_ref[...]
            r = lax.rsqrt(jnp.mean(xv * xv, axis=-1, keepdims=True) + RMS_EPS)
            xh = xv * r
            dy = dhv * gain_ref[...]
            mdot = jnp.mean(dy * xh, axis=-1, keepdims=True)
            o_ref[...] = dres_ref[...] + r * (dy - xh * mdot)

            @pl.when(pl.program_id(0) == 0)
            def _():
                dg_ref[...] = jnp.zeros_like(dg_ref)

            dg_ref[0:1, :] += jnp.sum(dhv * xh, axis=0, keepdims=True)

    oblk = pl.BlockSpec((tm, n), lambda i: (i, 0))
    gblk = pl.BlockSpec((1, n), lambda i: (0, 0))
    in_specs = [pl.BlockSpec((tm, k), lambda i: (i, 0)), pl.BlockSpec(w.shape, lambda i: (0, 0))]
    args = [a, w]
    for e in (relu2_grad, res):
        if e is not None:
            in_specs.append(oblk)
            args.append(e)
    out_specs, out_shape = [oblk], [jax.ShapeDtypeStruct((s, n), out_dtype)]
    if norm_gain is not None:
        in_specs.append(gblk)
        args.append(norm_gain)
        out_specs.append(oblk)
        out_shape.append(jax.ShapeDtypeStruct((s, n), BF16))
    if rms_back is not None:
        in_specs += [oblk, gblk, oblk]
        args += list(rms_back)
        out_specs.append(pl.BlockSpec((8, n), lambda i: (0, 0)))
        out_shape.append(jax.ShapeDtypeStruct((8, n), F32))
    outs = _call(
        body, name=name, grid=(s // tm,), in_specs=in_specs, out_specs=out_specs, out_shape=out_shape,
        compiler_params=_params(("arbitrary",) if rms_back is not None else ("parallel",)),
    )(*args)
    if rms_back is not None:
        return outs[0], outs[1][0:1]
    return outs if norm_gain is not None else outs[0]


def dense_tn(a, b, *, tk, bn, name, relu2_in=False):
    s, k = a.shape
    n = b.shape[1]
    ts = TN_ROWS
    ns = s // ts
    nc = min(bn, DENSE_CHUNK)

    def body(a_ref, b_ref, o_ref, acc_ref):
        t = pl.program_id(2)

        @pl.when(t == 0)
        def _():
            acc_ref[...] = jnp.zeros_like(acc_ref)

        av = a_ref[...]
        av = _relu2(av).astype(BF16) if relu2_in else av.astype(BF16)
        at = av.T
        for c0 in range(0, bn, nc):
            cols = slice(c0, c0 + nc)
            acc_ref[:, cols] += _nn(at, b_ref[:, cols].astype(BF16))

        @pl.when(t == ns - 1)
        def _():
            o_ref[...] = acc_ref[...].astype(o_ref.dtype)

    return _call(
        body, name=name, grid=(k // tk, n // bn, ns),
        in_specs=[pl.BlockSpec((ts, tk), lambda kb, j, t: (t, kb)), pl.BlockSpec((ts, bn), lambda kb, j, t: (t, j))],
        out_specs=pl.BlockSpec((tk, bn), lambda kb, j, t: (kb, j)),
        out_shape=jax.ShapeDtypeStruct((k, n), BF16),
        scratch_shapes=[pltpu.VMEM((tk, bn), F32)],
        compiler_params=_params(("parallel", "parallel", "arbitrary")),
    )(a, b)


def _softplus(z):
    return jnp.maximum(z, 0.0) + jnp.log(1.0 + jnp.exp(-jnp.abs(z)))


def _tri_mats():
    t = SB_TILE
    r = jnp.arange(2 * t)[:, None] % t
    c = jnp.arange(t)[None, :]
    return jnp.stack([(r > c), (r < c), (r <= c)]).astype(BF16)


def _split_sum(v, u2):
    hi, lo = _split_bf16(v)
    return _nn(jnp.concatenate([hi, lo], axis=1), u2)


def sb_fwd(qkv, tri, name, rider=None):
    s = qkv.shape[0]
    t = SB_TILE

    def body(q_ref, k_ref, v_ref, u_ref, o_ref, lt_ref, cnt_ref, c0_ref, c1_ref, acc0_ref, acc1_ref):
        p = pl.program_id(0)
        i = pl.program_id(1)
        lane = lax.broadcasted_iota(jnp.int32, (t, PAIR), 1)
        causal = lax.broadcasted_iota(jnp.int32, (t, t), 1) < lax.broadcasted_iota(jnp.int32, (t, t), 0)
        qv = q_ref[...]
        u_after = u_ref[0]
        qhs = [jnp.where(lane < HEAD_DIM, qv, 0) * QK_SCALE, jnp.where(lane >= HEAD_DIM, qv, 0) * QK_SCALE]
        cs = [c0_ref, c1_ref]
        accs = [acc0_ref, acc1_ref]
        for r in cs + accs:
            r[...] = jnp.zeros_like(r)

        def tile(j, masked):
            off = pl.multiple_of(j * t, t)
            kj = k_ref[pl.ds(off, t), :]
            vj = v_ref[pl.ds(off, t), :]
            hs = range(2)
            zs = [_nt(qhs[h], kj) for h in hs]
            ls = [-_softplus(z) for z in zs]
            if masked:
                ls = [jnp.where(causal, l, 0.0) for l in ls]
            afters = [_split_sum(l, u_after) for l in ls]
            a_s = [jnp.exp(zs[h] + ls[h] + afters[h] + cs[h][...]) for h in hs]
            if masked:
                a_s = [jnp.where(causal, a, 0.0) for a in a_s]
            for h in hs:
                accs[h][...] += _nn(a_s[h].astype(BF16), vj)
                cs[h][...] += afters[h][:, 0:1] + ls[h][:, 0:1]

        def alive():
            return jnp.max(jnp.maximum(c0_ref[...], c1_ref[...])) > -SB_DEAD

        tile(i, True)

        def cond(st):
            return (st[0] < i) & st[1]

        def step(st):
            tile(i - 1 - st[0], False)
            return st[0] + 1, alive()

        swept, _ = lax.while_loop(cond, step, (jnp.int32(0), alive()))
        cnt_ref[p, i] = swept
        o_ref[...] = jnp.where(lane < HEAD_DIM, acc0_ref[...], acc1_ref[...]).astype(o_ref.dtype)
        lt_ref[...] = jnp.where(lane == 0, c0_ref[...], jnp.where(lane == 1, c1_ref[...], 0.0))

    def first_last():
        p, i = pl.program_id(0), pl.program_id(1)
        return (p == 0) & (i == 0), (p == N_PAIRS - 1) & (i == s // t - 1)

    return hosted_call(
        body, first_last, rider, name=name, grid=(N_PAIRS, s // t),
        in_specs=[
            pl.BlockSpec((t, PAIR), lambda p, i: (i, p)),
            pl.BlockSpec((s, PAIR), lambda p, i: (0, N_PAIRS + p)),
            pl.BlockSpec((s, PAIR), lambda p, i: (0, 2 * N_PAIRS + p)),
            pl.BlockSpec((3, 2 * t, t), lambda p, i: (0, 0, 0)),
        ],
        out_specs=[pl.BlockSpec((t, PAIR), lambda p, i: (i, p)), pl.BlockSpec((None, t, PAIR), lambda p, i: (p, i, 0)),
                   pl.BlockSpec(memory_space=pltpu.SMEM)],
        out_shape=[jax.ShapeDtypeStruct((s, D_MODEL), BF16), jax.ShapeDtypeStruct((N_PAIRS, s, PAIR), F32),
                   jax.ShapeDtypeStruct((N_PAIRS, s // t), jnp.int32)],
        scratch_shapes=[pltpu.VMEM((t, 1), F32), pltpu.VMEM((t, 1), F32), pltpu.VMEM((t, PAIR), F32),
                        pltpu.VMEM((t, PAIR), F32)],
        args=(qkv, qkv, qkv, tri),
        compiler_params=_params(("arbitrary", "arbitrary")),
    )


def sb_bwd(qkv, do, ltot, swept, tri, name):
    s = qkv.shape[0]
    t = SB_TILE
    nq = s // t

    def body(cnt_ref, q_ref, k_ref, v_ref, do_ref, lt_ref, u_ref, dq_ref, dk_ref, dv_ref,
             pl0_ref, pl1_ref, pg0_ref, pg1_ref, dqa0_ref, dqa1_ref, dka_ref, dva_ref):
        p = pl.program_id(0)
        i = pl.program_id(1)

        @pl.when(i == 0)
        def _():
            dka_ref[...] = jnp.zeros_like(dka_ref)
            dva_ref[...] = jnp.zeros_like(dva_ref)

        lane = lax.broadcasted_iota(jnp.int32, (t, PAIR), 1)
        causal = lax.broadcasted_iota(jnp.int32, (t, t), 1) < lax.broadcasted_iota(jnp.int32, (t, t), 0)
        qv = q_ref[...]
        dov = do_ref[...]
        ltv = lt_ref[...]
        u_before = u_ref[1]
        u_upto = u_ref[2, 0:t, :]
        hms = [lane < HEAD_DIM, lane >= HEAD_DIM]
        qhs = [jnp.where(hm, qv, 0) * QK_SCALE for hm in hms]
        dohs = [jnp.where(hm, dov, 0) for hm in hms]
        ltots = [jnp.sum(jnp.where(lane == h, ltv, 0.0), axis=1, keepdims=True) for h in range(2)]
        pls = [pl0_ref, pl1_ref]
        pgs = [pg0_ref, pg1_ref]
        dqas = [dqa0_ref, dqa1_ref]
        for r in pls + pgs + dqas:
            r[...] = jnp.zeros_like(r)

        def tile(j, masked):
            off = pl.multiple_of(j * t, t)
            kj = k_ref[pl.ds(off, t), :]
            vj = v_ref[pl.ds(off, t), :]
            hs = range(2)
            zs = [_nt(qhs[h], kj) for h in hs]
            das = [_nt(dohs[h], vj) for h in hs]
            sps = [_softplus(z) for z in zs]
            ls = [-sp for sp in sps]
            if masked:
                ls = [jnp.where(causal, l, 0.0) for l in ls]
            befores = [_split_sum(l, u_before) for l in ls]
            a_s = [jnp.exp(zs[h] + (ltots[h] - pls[h][...] - befores[h])) for h in hs]
            if masked:
                a_s = [jnp.where(causal, a, 0.0) for a in a_s]
            gs = [a_s[h] * das[h] for h in hs]
            uptos = [_nn(g.astype(BF16), u_upto) for g in gs]
            dzs = [gs[h] - jnp.exp(zs[h] - sps[h]) * (pgs[h][...] + uptos[h]) for h in hs]
            if masked:
                dzs = [jnp.where(causal, dz, 0.0) for dz in dzs]
            dzbs = [dz.astype(BF16) for dz in dzs]
            abs_ = [a.astype(BF16) for a in a_s]
            for h in hs:
                dqas[h][...] += _nn(dzbs[h], kj)
            dka_ref[pl.ds(off, t), :] += _tn(dzbs[0], qhs[0]) + _tn(dzbs[1], qhs[1])
            dva_ref[pl.ds(off, t), :] += _tn(abs_[0], dohs[0]) + _tn(abs_[1], dohs[1])
            for h in hs:
                pls[h][...] += befores[h][:, t - 1:t] + ls[h][:, t - 1:t]
                pgs[h][...] += uptos[h][:, t - 1:t]

        def step(j, carry):
            tile(j, False)
            return carry

        lax.fori_loop(i - cnt_ref[p, i], i, step, 0)
        tile(i, True)
        dq_ref[...] = (jnp.where(lane < HEAD_DIM, dqa0_ref[...], dqa1_ref[...]) * QK_SCALE).astype(dq_ref.dtype)

        @pl.when(i == nq - 1)
        def _():
            dk_ref[...] = dka_ref[...].astype(dk_ref.dtype)
            dv_ref[...] = dva_ref[...].astype(dv_ref.dtype)

    qblk = pl.BlockSpec((t, PAIR), lambda p, i: (i, p))
    full = pl.BlockSpec((s, PAIR), lambda p, i: (0, p))
    out = jax.ShapeDtypeStruct((s, D_MODEL), BF16)
    return _call(
        body, name=name, grid=(N_PAIRS, nq),
        in_specs=[
            pl.BlockSpec(memory_space=pltpu.SMEM),
            qblk,
            pl.BlockSpec((s, PAIR), lambda p, i: (0, N_PAIRS + p)),
            pl.BlockSpec((s, PAIR), lambda p, i: (0, 2 * N_PAIRS + p)),
            qblk,
            pl.BlockSpec((None, t, PAIR), lambda p, i: (p, i, 0)),
            pl.BlockSpec((3, 2 * t, t), lambda p, i: (0, 0, 0)),
        ],
        out_specs=[qblk, full, full],
        out_shape=[out, out, out],
        scratch_shapes=[pltpu.VMEM((t, 1), F32)] * 4 + [pltpu.VMEM((t, PAIR), F32)] * 2
        + [pltpu.VMEM((s, PAIR), F32), pltpu.VMEM((s, PAIR), F32)],
        compiler_params=_params(("arbitrary", "arbitrary")),
    )(swept, qkv, qkv, qkv, do, ltot, tri)


def _ring_onehot():
    r = lax.broadcasted_iota(jnp.int32, (N_REL_PAD, CA_RING), 0)
    w = lax.broadcasted_iota(jnp.int32, (N_REL_PAD, CA_RING), 1)
    idx = jnp.where(w <= CA_WIN, jnp.clip(CA_WIN - w, 0, 2 * MAX_REL), 2 * MAX_REL)
    return (r == idx).astype(F32)


def bias_table(rel_pad, name):
    def body(rb_ref, o_ref):
        ring = jnp.dot(rb_ref[...], _ring_onehot(), preferred_element_type=F32, precision=lax.Precision.HIGHEST)
        col = lax.broadcasted_iota(jnp.int32, (N_HEADS, CA_WIN), 1)

        def row(tl, carry):
            v = pltpu.roll(ring, tl, 1)[:, :CA_WIN]
            first = (tl // CHUNK) * CHUNK
            ok = (col >= first) & (col < first + (LEFT_CHUNKS + 1) * CHUNK)
            o_ref[tl] = jnp.where(ok, v, NEG_BIG)
            return carry

        lax.fori_loop(0, CA_QBLK, row, 0)

    return _call(
        body, name=name,
        in_specs=[pl.BlockSpec(memory_space=pltpu.VMEM)], out_specs=pl.BlockSpec(memory_space=pltpu.VMEM),
        out_shape=jax.ShapeDtypeStruct((CA_QBLK, N_HEADS, CA_WIN), F32),
        compiler_params=_params(),
    )(rel_pad)


def bias_fold(db, name):
    def body(db_ref, o_ref):
        zeros = jnp.zeros((N_HEADS, CA_RING - CA_WIN), F32)

        def row(tl, acc):
            v = jnp.concatenate([db_ref[tl], zeros], axis=1)
            return acc + pltpu.roll(v, (CA_RING - tl) % CA_RING, 1)

        ring = lax.fori_loop(0, CA_QBLK, row, jnp.zeros((N_HEADS, CA_RING), F32))
        o_ref[...] = lax.dot_general(ring, _ring_onehot(), (((1,), (1,)), ((), ())),
                                     preferred_element_type=F32, precision=lax.Precision.HIGHEST)

    return _call(
        body, name=name,
        in_specs=[pl.BlockSpec(memory_space=pltpu.VMEM)], out_specs=pl.BlockSpec(memory_space=pltpu.VMEM),
        out_shape=jax.ShapeDtypeStruct((N_HEADS, N_REL_PAD), F32),
        compiler_params=_params(),
    )(db)


def _ca_logits(qh, kw, bias_h, first_key_ok):
    sc = _nt(qh, kw) + bias_h
    col = lax.broadcasted_iota(jnp.int32, sc.shape, 1)
    return jnp.where(col >= first_key_ok, sc, NEG_BIG)


def ca_fwd(qkn, qkv, table, name):
    s = qkv.shape[0]
    tq = CA_QBLK

    def body(q_ref, k_ref, v_ref, b_ref, o_ref, kp_ref, vp_ref):
        i = pl.program_id(1)

        @pl.when(i == 0)
        def _():
            kp_ref[0:CA_PAD, :] = jnp.zeros((CA_PAD, PAIR), BF16)
            vp_ref[0:CA_PAD, :] = jnp.zeros((CA_PAD, PAIR), BF16)
            kp_ref[CA_PAD:, :] = k_ref[...]
            vp_ref[CA_PAD:, :] = v_ref[...]

        lane = lax.broadcasted_iota(jnp.int32, (CA_SUB, PAIR), 1)
        hms = [lane < HEAD_DIM, lane >= HEAD_DIM]
        for r in range(tq // CA_SUB):
            rows = slice(r * CA_SUB, (r + 1) * CA_SUB)
            cols = slice(r * CA_SUB, r * CA_SUB + CA_SUBWIN)
            off = pl.multiple_of(i * tq + r * CA_SUB, CA_SUB)
            kw = kp_ref[pl.ds(off, CA_SUBWIN), :]
            vw = vp_ref[pl.ds(off, CA_SUBWIN), :]
            qv = q_ref[rows, :]
            first_ok = CA_PAD - i * tq - r * CA_SUB
            hs = range(2)
            qhs = [jnp.where(hm, qv, 0) * QK_SCALE for hm in hms]
            scs = [_ca_logits(qhs[h], kw, b_ref[h, rows, cols], first_ok) for h in hs]
            es = [jnp.exp(sc - jnp.max(sc, axis=1, keepdims=True)) for sc in scs]
            ps = [(e / jnp.sum(e, axis=1, keepdims=True)).astype(BF16) for e in es]
            outs = [_nn(p, vw) for p in ps]
            o_ref[rows, :] = jnp.where(lane < HEAD_DIM, outs[0], outs[1]).astype(o_ref.dtype)

    return _call(
        body, name=name, grid=(N_PAIRS, s // tq),
        in_specs=[
            pl.BlockSpec((tq, PAIR), lambda p, i: (i, p)),
            pl.BlockSpec((s, PAIR), lambda p, i: (0, N_PAIRS + p)),
            pl.BlockSpec((s, PAIR), lambda p, i: (0, 2 * N_PAIRS + p)),
            pl.BlockSpec((2, tq, CA_WIN), lambda p, i: (p, 0, 0)),
        ],
        out_specs=pl.BlockSpec((tq, PAIR), lambda p, i: (i, p)),
        out_shape=jax.ShapeDtypeStruct((s, D_MODEL), BF16),
        scratch_shapes=[pltpu.VMEM((s + CA_PAD, PAIR), BF16), pltpu.VMEM((s + CA_PAD, PAIR), BF16)],
        compiler_params=_params(("parallel", "arbitrary")),
    )(qkn, qkn, qkv, table)


def ca_bwd(qkn, qkv, o, do, table, name, rider=None):
    s = qkv.shape[0]
    tq = CA_QBLK
    nq = s // tq

    def body(q_ref, k_ref, v_ref, o_ref, do_ref, b_ref, dq_ref, dk_ref, dv_ref, db_ref,
             kp_ref, vp_ref, dka_ref, dva_ref):
        i = pl.program_id(1)

        @pl.when(i == 0)
        def _():
            kp_ref[0:CA_PAD, :] = jnp.zeros((CA_PAD, PAIR), BF16)
            vp_ref[0:CA_PAD, :] = jnp.zeros((CA_PAD, PAIR), BF16)
            kp_ref[CA_PAD:, :] = k_ref[...]
            vp_ref[CA_PAD:, :] = v_ref[...]
            dka_ref[...] = jnp.zeros_like(dka_ref)
            dva_ref[...] = jnp.zeros_like(dva_ref)
            db_ref[...] = jnp.zeros_like(db_ref)

        lane = lax.broadcasted_iota(jnp.int32, (CA_SUB, PAIR), 1)
        hms = [lane < HEAD_DIM, lane >= HEAD_DIM]
        for r in range(tq // CA_SUB):
            rows = slice(r * CA_SUB, (r + 1) * CA_SUB)
            cols = slice(r * CA_SUB, r * CA_SUB + CA_SUBWIN)
            off = pl.multiple_of(i * tq + r * CA_SUB, CA_SUB)
            kw = kp_ref[pl.ds(off, CA_SUBWIN), :]
            vw = vp_ref[pl.ds(off, CA_SUBWIN), :]
            qv = q_ref[rows, :]
            dov = do_ref[rows, :]
            prod = dov.astype(F32) * o_ref[rows, :].astype(F32)
            first_ok = CA_PAD - i * tq - r * CA_SUB
            hs = range(2)
            qhs = [jnp.where(hm, qv, 0) * QK_SCALE for hm in hms]
            dohs = [jnp.where(hm, dov, 0) for hm in hms]
            deltas = [jnp.sum(jnp.where(hm, prod, 0.0), axis=1, keepdims=True) for hm in hms]
            scs = [_ca_logits(qhs[h], kw, b_ref[h, rows, cols], first_ok) for h in hs]
            dps = [_nt(dohs[h], vw) for h in hs]
            es = [jnp.exp(sc - jnp.max(sc, axis=1, keepdims=True)) for sc in scs]
            ps = [e / jnp.sum(e, axis=1, keepdims=True) for e in es]
            dss = [ps[h] * (dps[h] - deltas[h]) for h in hs]
            for h in hs:
                db_ref[h, rows, cols] += dss[h]
            dsbs = [ds.astype(BF16) for ds in dss]
            pbs = [p.astype(BF16) for p in ps]
            dqs = [_nn(dsbs[h], kw) for h in hs]
            dq_ref[rows, :] = (jnp.where(lane < HEAD_DIM, dqs[0], dqs[1]) * QK_SCALE).astype(dq_ref.dtype)
            dka_ref[pl.ds(off, CA_SUBWIN), :] += _tn(dsbs[0], qhs[0]) + _tn(dsbs[1], qhs[1])
            dva_ref[pl.ds(off, CA_SUBWIN), :] += _tn(pbs[0], dohs[0]) + _tn(pbs[1], dohs[1])

        @pl.when(i == nq - 1)
        def _():
            dk_ref[...] = dka_ref[CA_PAD:, :].astype(dk_ref.dtype)
            dv_ref[...] = dva_ref[CA_PAD:, :].astype(dv_ref.dtype)

    qblk = pl.BlockSpec((tq, PAIR), lambda p, i: (i, p))
    full = pl.BlockSpec((s, PAIR), lambda p, i: (0, p))
    tblk = pl.BlockSpec((2, tq, CA_WIN), lambda p, i: (p, 0, 0))
    out = jax.ShapeDtypeStruct((s, D_MODEL), BF16)

    def first_last():
        p, i = pl.program_id(0), pl.program_id(1)
        return (p == 0) & (i == 0), (p == N_PAIRS - 1) & (i == nq - 1)

    return hosted_call(
        body, first_last, rider, name=name, grid=(N_PAIRS, nq),
        in_specs=[
            qblk,
            pl.BlockSpec((s, PAIR), lambda p, i: (0, N_PAIRS + p)),
            pl.BlockSpec((s, PAIR), lambda p, i: (0, 2 * N_PAIRS + p)),
            qblk, qblk, tblk,
        ],
        out_specs=[qblk, full, full, tblk],
        out_shape=[out, out, out, jax.ShapeDtypeStruct((N_HEADS, tq, CA_WIN), F32)],
        scratch_shapes=[pltpu.VMEM((s + CA_PAD, PAIR), BF16), pltpu.VMEM((s + CA_PAD, PAIR), BF16),
                        pltpu.VMEM((s + CA_PAD, PAIR), F32), pltpu.VMEM((s + CA_PAD, PAIR), F32)],
        args=(qkn, qkn, qkv, o, do, table),
        compiler_params=_params(("arbitrary", "arbitrary")),
    )


def _me():
    x, y, c = lax.axis_index("x"), lax.axis_index("y"), lax.axis_index("c")
    return x, y, c, 4 * x + 2 * y + c


def _peer(k):
    x, y, c, _ = _me()
    px, py, pc = x ^ ((k >> 2) & 1), y ^ ((k >> 1) & 1), c ^ (k & 1)
    return (px, py, pc), 4 * px + 2 * py + pc


ANY = pl.BlockSpec(memory_space=pl.ANY)


class Exchange:
    def __init__(self, n, copy, in_arrays, out_shape):
        self.n, self.copy, self.in_arrays, self.out_shape = n, copy, list(in_arrays), list(out_shape)
        self.sems = [pltpu.SemaphoreType.DMA((n, N_DEV - 1)), pltpu.SemaphoreType.DMA((n, N_DEV - 1)),
                     pltpu.SemaphoreType.DMA((n,))]

    def _copies(self, data, sems):
        send_sems, recv_sems, local_sems = sems
        _, _, _, me = _me()
        local, sends, recvs = [], [], []
        for a in range(self.n):
            s_ref, d_ref = self.copy(data, a, me, me)
            local.append(pltpu.make_async_copy(s_ref, d_ref, local_sems.at[a]))
            for k in range(1, N_DEV):
                peer, pidx = _peer(k)
                s_ref, d_ref = self.copy(data, a, me, pidx)
                sends.append(pltpu.make_async_remote_copy(
                    src_ref=s_ref, dst_ref=d_ref, send_sem=send_sems.at[a, k - 1], recv_sem=recv_sems.at[a, k - 1],
                    device_id=peer, device_id_type=MESH))
                s_ref, d_ref = self.copy(data, a, pidx, me)
                recvs.append(pltpu.make_async_remote_copy(
                    src_ref=s_ref, dst_ref=d_ref, send_sem=send_sems.at[a, k - 1], recv_sem=recv_sems.at[a, k - 1],
                    device_id=peer, device_id_type=MESH))
        return local, sends, recvs

    def start(self, data, sems):
        local, sends, _ = self._copies(data, sems)
        for cp in local + sends:
            cp.start()

    def finish(self, data, sems):
        local, sends, recvs = self._copies(data, sems)
        for cp in recvs:
            cp.wait_recv()
        for cp in sends:
            cp.wait_send()
        for cp in local:
            cp.wait()


def run_exchange(ex, name):
    n_data = len(ex.in_arrays) + len(ex.out_shape)

    def body(*refs):
        ex.start(refs[:n_data], refs[n_data:])
        ex.finish(refs[:n_data], refs[n_data:])

    return _call(
        body, name=name,
        in_specs=[ANY] * len(ex.in_arrays), out_specs=[ANY] * len(ex.out_shape), out_shape=ex.out_shape,
        scratch_shapes=ex.sems, compiler_params=pltpu.CompilerParams(has_side_effects=True),
    )(*ex.in_arrays)


def hosted_call(core_body, first_last, rider, *, in_specs, out_specs, out_shape, scratch_shapes, args, **kw):
    if rider is None:
        return _call(core_body, in_specs=in_specs, out_specs=out_specs, out_shape=out_shape,
                     scratch_shapes=scratch_shapes, **kw)(*args), []
    n_in, n_out, n_scr = len(in_specs), len(out_specs), len(scratch_shapes)
    r_in, r_out = len(rider.in_arrays), len(rider.out_shape)

    def body(*refs):
        it = iter(refs)
        take = lambda m: [next(it) for _ in range(m)]
        c_in, x_in, c_out, x_out, c_scr, sems = take(n_in), take(r_in), take(n_out), take(r_out), take(n_scr), take(3)
        first, last = first_last()

        @pl.when(first)
        def _():
            rider.start(x_in + x_out, sems)

        core_body(*c_in, *c_out, *c_scr)

        @pl.when(last)
        def _():
            rider.finish(x_in + x_out, sems)

    outs = _call(
        body, in_specs=list(in_specs) + [ANY] * r_in, out_specs=list(out_specs) + [ANY] * r_out,
        out_shape=list(out_shape) + rider.out_shape, scratch_shapes=list(scratch_shapes) + rider.sems, **kw,
    )(*args, *rider.in_arrays)
    return outs[:n_out], outs[n_out:]


def _window(full_ref, shard_shape, idx):
    a, b = shard_shape
    if a == full_ref.shape[0]:
        return full_ref.at[:, pl.ds(pl.multiple_of(idx * b, 128), b)]
    return full_ref.at[pl.ds(pl.multiple_of(idx * a, 8), a), :]


def gather_exchange(shards, full_shapes, layers):
    nw = len(shards)

    def copy(data, a, src_idx, dst_idx):
        li, w = divmod(a, nw)
        s_ref = data[w].at[layers[li]]
        return s_ref, _window(data[nw + a], s_ref.shape, src_idx)

    out_shape = [jax.ShapeDtypeStruct(full_shapes[w], shards[w].dtype) for _ in layers for w in range(nw)]
    return Exchange(nw * len(layers), copy, shards, out_shape)


def scatter_exchange(partials, shard_shapes):
    n = len(partials)

    def copy(data, a, src_idx, dst_idx):
        return _window(data[a], shard_shapes[a], dst_idx), data[n + a].at[src_idx]

    out_shape = [jax.ShapeDtypeStruct((N_DEV,) + tuple(ss), p.dtype) for ss, p in zip(shard_shapes, partials)]
    return Exchange(n, copy, partials, out_shape)


def _adamw(w, g, m, v):
    m = ADAM_B1 * m + (1.0 - ADAM_B1) * g
    v = ADAM_B2 * v + (1.0 - ADAM_B2) * (g * g)
    m_hat = m / (1.0 - ADAM_B1 ** ADAM_STEP)
    v_hat = v / (1.0 - ADAM_B2 ** ADAM_STEP)
    delta = -ADAM_LR * (m_hat / (jnp.sqrt(v_hat) + ADAM_EPS) + ADAM_WD * w)
    return delta, m, v


def adam_shard(parts, w, m, v, name):
    _, r, c = w.shape
    tr = min(r, 256)
    nr = r // tr

    def body(*refs):
        p_refs = refs[:DEPTH]
        w_ref, m_ref, v_ref, g_out, d_out, m_out, v_out = refs[DEPTH:]
        layer = pl.program_id(0)
        for li in range(DEPTH):
            @pl.when(layer == li)
            def _(p_ref=p_refs[li]):
                g = p_ref[0].astype(F32)
                for e in range(1, N_DEV):
                    g = g + p_ref[e].astype(F32)
                delta, mn, vn = _adamw(w_ref[...], g, m_ref[...], v_ref[...])
                g_out[...] = g
                d_out[...] = delta
                m_out[...] = mn
                v_out[...] = vn

    def part_spec(li):
        return pl.BlockSpec((N_DEV, tr, c), lambda layer, i: (0, jnp.where(layer == li, i, 0), 0))

    blk = pl.BlockSpec((None, tr, c), lambda layer, i: (layer, i, 0))
    out = jax.ShapeDtypeStruct(w.shape, F32)
    return _call(
        body, name=name, grid=(DEPTH, nr),
        in_specs=[part_spec(li) for li in range(DEPTH)] + [blk, blk, blk],
        out_specs=[blk] * 4, out_shape=[out] * 4,
        compiler_params=_params(("arbitrary", "arbitrary")),
    )(*parts, w, m, v)


def small_allreduce_adam(pk, w, m, v, name):
    def body(pk_ref, w_ref, m_ref, v_ref, g_out, d_out, m_out, v_out, all_ref, send_sems, recv_sems):
        _, _, _, me = _me()
        all_ref[me] = pk_ref[...]
        sends = []
        for k in range(1, N_DEV):
            peer, _ = _peer(k)
            cp = pltpu.make_async_remote_copy(
                src_ref=pk_ref, dst_ref=all_ref.at[me], send_sem=send_sems.at[k - 1],
                recv_sem=recv_sems.at[k - 1], device_id=peer, device_id_type=MESH)
            cp.start()
            sends.append(cp)
        for k in range(1, N_DEV):
            peer, pidx = _peer(k)
            pltpu.make_async_remote_copy(
                src_ref=pk_ref, dst_ref=all_ref.at[pidx], send_sem=send_sems.at[k - 1],
                recv_sem=recv_sems.at[k - 1], device_id=peer, device_id_type=MESH).wait_recv()
        for cp in sends:
            cp.wait_send()
        g = all_ref[0]
        for e in range(1, N_DEV):
            g = g + all_ref[e]
        delta, mn, vn = _adamw(w_ref[...], g, m_ref[...], v_ref[...])
        g_out[...] = g
        d_out[...] = delta
        m_out[...] = mn
        v_out[...] = vn

    vm = pl.BlockSpec(memory_space=pltpu.VMEM)
    out = jax.ShapeDtypeStruct((PACK_ROWS, 128), F32)
    return _call(
        body, name=name,
        in_specs=[vm] * 4, out_specs=[vm] * 4, out_shape=[out] * 4,
        scratch_shapes=[pltpu.VMEM((N_DEV, PACK_ROWS, 128), F32), pltpu.SemaphoreType.DMA((N_DEV - 1,)),
                        pltpu.SemaphoreType.DMA((N_DEV - 1,))],
        compiler_params=pltpu.CompilerParams(has_side_effects=True),
    )(pk, w, m, v)


def _pack_small(mix, ffn, qn, kn, rel, loss):
    flat = jnp.concatenate([mix.reshape(-1), ffn.reshape(-1), qn.reshape(-1), kn.reshape(-1), rel.reshape(-1),
                            loss.reshape(-1)])
    return jnp.pad(flat, (0, PACK_LEN - flat.shape[0])).reshape(PACK_ROWS, 128)


def _unpack_small(pk):
    flat = pk.reshape(-1)
    return (flat[OFF_MIX:OFF_FFN].reshape(DEPTH, D_MODEL), flat[OFF_FFN:OFF_QN].reshape(DEPTH, D_MODEL),
            flat[OFF_QN:OFF_KN].reshape(2, HEAD_DIM), flat[OFF_KN:OFF_REL].reshape(2, HEAD_DIM),
            flat[OFF_REL:OFF_LOSS].reshape(2, N_HEADS, N_REL), flat[OFF_LOSS])


def _pair_gain(g):
    return jnp.concatenate([g, g]).reshape(1, PAIR)


def kernel(x, mix_norm, w_qkv, w_o, q_norm, k_norm, rel_bias, ffn_norm, w_up, w_down, loss_target, m_mix_norm, m_w_qkv, m_w_o, m_q_norm, m_k_norm, m_rel_bias, m_ffn_norm, m_w_up, m_w_down, v_mix_norm, v_w_qkv, v_w_o, v_q_norm, v_k_norm, v_rel_bias, v_ffn_norm, v_w_up, v_w_down):
    x0 = x[0]
    target = loss_target[0]
    shard_shapes = [w_qkv.shape[1:], w_o.shape[1:], w_up.shape[1:], w_down.shape[1:]]
    full_shapes = [(D_MODEL, 3 * D_MODEL), (D_MODEL, D_MODEL), (D_MODEL, D_FF), (D_FF, D_MODEL)]
    shards = [w_qkv.astype(BF16), w_o.astype(BF16), w_up.astype(BF16), w_down.astype(BF16)]
    nw = len(shards)

    weights = [run_exchange(gather_exchange(shards, full_shapes, [0]), "gather_layer0")]
    later_weights = gather_exchange(shards, full_shapes, list(range(1, DEPTH)))
    tri = _tri_mats()

    saved = []
    xin = x0
    h0 = rms_fwd(x0, mix_norm[0:1], "rms_mix_0")
    for layer in range(DEPTH):
        mixer_b = layer % 2 == 1
        idx = layer // 2
        g_qkv, g_o, g_up, g_down = weights[layer]
        qkv_raw = dense(h0, g_qkv, nt=False, name=f"qkv_{layer}", out_dtype=BF16, tm=1024)
        if mixer_b:
            gq, gk = _pair_gain(q_norm[idx]), _pair_gain(k_norm[idx])
            qkn = headnorm_fwd(qkv_raw, gq, gk, f"headnorm_{layer}")
            rel_pad = jnp.pad(rel_bias[idx], ((0, 0), (0, N_REL_PAD - N_REL)))
            table = jnp.transpose(bias_table(rel_pad, f"bias_table_{layer}"), (1, 0, 2))
            o = ca_fwd(qkn, qkv_raw, table, f"ca_fwd_{layer}")
            attn_saved = (qkv_raw, qkn, table)
        else:
            (o, ltot, swept), arrived = sb_fwd(qkv_raw, tri, f"sb_fwd_{layer}", rider=later_weights if layer == 0 else None)
            if layer == 0:
                weights += [arrived[li * nw:(li + 1) * nw] for li in range(DEPTH - 1)]
            attn_saved = (qkv_raw, ltot, swept)
        x1, h1 = dense(o, g_o, nt=False, name=f"attn_out_{layer}", out_dtype=F32, tm=1024, res=xin,
                       norm_gain=ffn_norm[layer:layer + 1])
        pre = dense(h1, g_up, nt=False, name=f"up_{layer}", out_dtype=BF16, tm=1024)
        saved.append((xin, h0, attn_saved, o, x1, h1, pre))
        if layer + 1 < DEPTH:
            xin, h0 = dense(pre, g_down, nt=False, name=f"down_{layer}", out_dtype=F32, tm=512, relu2_in=True, res=x1,
                            norm_gain=mix_norm[layer + 1:layer + 2])
        else:
            xin = dense(pre, g_down, nt=False, name=f"down_{layer}", out_dtype=F32, tm=512, relu2_in=True, res=x1)

    dx, loss_blk = loss_head(xin, target, "loss_head")

    partial = {}
    received = {}
    d_mix, d_ffn = [None] * DEPTH, [None] * DEPTH
    d_qn, d_kn, d_rel = [None] * 2, [None] * 2, [None] * 2

    def exchange_of(keys):
        return scatter_exchange([partial[key] for key in keys], [shard_shapes[key[0]] for key in keys])

    for layer in reversed(range(DEPTH)):
        mixer_b = layer % 2 == 1
        idx = layer // 2
        xin, h0, attn_saved, o, x1, h1, pre = saved[layer]
        g_qkv, g_o, g_up, g_down = weights[layer]
        dpre = dense(dx, g_down, nt=True, name=f"d_pre_{layer}", out_dtype=BF16, tm=512, relu2_grad=pre)
        partial[3, layer] = dense_tn(pre, dx, tk=2048, bn=1024, name=f"dw_down_{layer}", relu2_in=True)
        partial[2, layer] = dense_tn(h1, dpre, tk=1024, bn=2048, name=f"dw_up_{layer}")
        dx, d_ffn[layer] = dense(dpre, g_up, nt=True, name=f"d_h1_{layer}", out_dtype=F32, tm=512,
                                 rms_back=(x1, ffn_norm[layer:layer + 1], dx))
        do = dense(dx, g_o, nt=True, name=f"d_o_{layer}", out_dtype=BF16, tm=1024)
        partial[1, layer] = dense_tn(o, dx, tk=1024, bn=1024, name=f"dw_o_{layer}")
        if mixer_b:
            qkv_raw, qkn, table = attn_saved
            keys = [key for key in sorted(partial) if key not in received] if layer == 1 else []
            (dq, dk, dv, dtab), arrived = ca_bwd(qkn, qkv_raw, o, do, table, f"ca_bwd_{layer}",
                                                 rider=exchange_of(keys) if keys else None)
            received.update(zip(keys, arrived))
            gq, gk = _pair_gain(q_norm[idx]), _pair_gain(k_norm[idx])
            dqkv, dgain = headnorm_bwd(dq, dk, dv, qkv_raw, gq, gk, f"headnorm_bwd_{layer}")
            d_qn[idx] = dgain[0, :HEAD_DIM] + dgain[0, HEAD_DIM:]
            d_kn[idx] = dgain[1, :HEAD_DIM] + dgain[1, HEAD_DIM:]
            d_rel[idx] = bias_fold(jnp.transpose(dtab, (1, 0, 2)), f"bias_fold_{layer}")[:, :N_REL]
        else:
            qkv_raw, ltot, swept = attn_saved
            dq, dk, dv = sb_bwd(qkv_raw, do, ltot, swept, tri, f"sb_bwd_{layer}")
            dqkv = jnp.concatenate([dq, dk, dv], axis=1)
        partial[0, layer] = dense_tn(h0, dqkv, tk=1024, bn=1536, name=f"dw_qkv_{layer}")
        dx, d_mix[layer] = dense(dqkv, g_qkv, nt=True, name=f"d_h0_{layer}", out_dtype=F32, tm=512,
                                 rms_back=(xin, mix_norm[layer:layer + 1], dx))

    grad_x = dx[None]

    keys = [key for key in sorted(partial) if key not in received]
    received.update(zip(keys, run_exchange(exchange_of(keys), "scatter_rest")))

    def big(wi, w, m, v, name):
        return adam_shard([received[wi, layer] for layer in range(DEPTH)], w, m, v, name)

    a_qkv = big(0, w_qkv, m_w_qkv, v_w_qkv, "adam_qkv")
    a_o = big(1, w_o, m_w_o, v_w_o, "adam_o")
    a_up = big(2, w_up, m_w_up, v_w_up, "adam_up")
    a_down = big(3, w_down, m_w_down, v_w_down, "adam_down")

    pk = _pack_small(jnp.concatenate(d_mix), jnp.concatenate(d_ffn), jnp.stack(d_qn), jnp.stack(d_kn),
                     jnp.stack(d_rel), loss_blk[0, 0])
    zero = jnp.zeros((), F32)
    pw = _pack_small(mix_norm, ffn_norm, q_norm, k_norm, rel_bias, zero)
    pm = _pack_small(m_mix_norm, m_ffn_norm, m_q_norm, m_k_norm, m_rel_bias, zero)
    pv = _pack_small(v_mix_norm, v_ffn_norm, v_q_norm, v_k_norm, v_rel_bias, zero)
    s_g, s_d, s_m, s_v = small_allreduce_adam(pk, pw, pm, pv, "small_allreduce_adam")
    g_mix, g_ffn, g_qn, g_kn, g_rel, loss = _unpack_small(s_g)
    dl_mix, dl_ffn, dl_qn, dl_kn, dl_rel, _ = _unpack_small(s_d)
    nm_mix, nm_ffn, nm_qn, nm_kn, nm_rel, _ = _unpack_small(s_m)
    nv_mix, nv_ffn, nv_qn, nv_kn, nv_rel, _ = _unpack_small(s_v)

    return (loss, grad_x,
            g_mix, a_qkv[0], a_o[0], g_qn, g_kn, g_rel, g_ffn, a_up[0], a_down[0],
            dl_mix, a_qkv[1], a_o[1], dl_qn, dl_kn, dl_rel, dl_ffn, a_up[1], a_down[1],
            nm_mix, a_qkv[2], a_o[2], nm_qn, nm_kn, nm_rel, nm_ffn, a_up[2], a_down[2],
            nv_mix, a_qkv[3], a_o[3], nv_qn, nv_kn, nv_rel, nv_ffn, a_up[3], a_down[3])
```

```python
import jax
import jax.numpy as jnp
from jax import lax
from jax.experimental import pallas as pl
from jax.experimental.pallas import tpu as pltpu

F32 = jnp.float32
BF16 = jnp.bfloat16
MESH = pl.DeviceIdType.MESH

D_MODEL = 1024
N_HEADS = 16
HEAD_DIM = 64
PAIR = 2 * HEAD_DIM
N_PAIRS = N_HEADS // 2
D_FF = 4 * D_MODEL
DEPTH = 4
N_DEV = 8
RMS_EPS = 1e-6
QK_SCALE = HEAD_DIM ** -0.5

SB_TILE = 256
SB_DEAD = 104.0

CHUNK = 64
LEFT_CHUNKS = 8
CA_QBLK = 256
CA_PAD = LEFT_CHUNKS * CHUNK
CA_WIN = CA_QBLK + CA_PAD
CA_SUB = 128
CA_SUBWIN = CA_SUB + CA_PAD
CA_RING = 1024
MAX_REL = 256
N_REL = 2 * MAX_REL + 1
N_REL_PAD = 640
NEG_BIG = -1e30

ADAM_LR = 0.001
ADAM_B1 = 0.9
ADAM_B2 = 0.999
ADAM_EPS = 1e-08
ADAM_WD = 0.01
ADAM_STEP = 10

ROW_TILE = 512
DENSE_CHUNK = 512
TN_ROWS = 1024
VMEM_LIMIT = 56 * 1024 * 1024

OFF_MIX = 0
OFF_FFN = OFF_MIX + DEPTH * D_MODEL
OFF_QN = OFF_FFN + DEPTH * D_MODEL
OFF_KN = OFF_QN + 2 * HEAD_DIM
OFF_REL = OFF_KN + 2 * HEAD_DIM
OFF_LOSS = OFF_REL + 2 * N_HEADS * N_REL
PACK_ROWS = 200
PACK_LEN = PACK_ROWS * 128


def _call(body, **kw):
    return pl.pallas_call(body, **kw)


def _params(sem=None, vmem=VMEM_LIMIT):
    if sem is None:
        return pltpu.CompilerParams(vmem_limit_bytes=vmem)
    return pltpu.CompilerParams(dimension_semantics=sem, vmem_limit_bytes=vmem)


def _nt(a, b):
    return lax.dot_general(a, b, (((1,), (1,)), ((), ())), preferred_element_type=F32)


def _tn(a, b):
    return lax.dot_general(a, b, (((0,), (0,)), ((), ())), preferred_element_type=F32)


def _nn(a, b):
    return jnp.dot(a, b, preferred_element_type=F32)


def _split_bf16(v):
    hi = v.astype(BF16)
    lo = (v - hi.astype(F32)).astype(BF16)
    return hi, lo


def rms_fwd(x, g, name):
    s = x.shape[0]
    tm = ROW_TILE

    def body(x_ref, g_ref, o_ref):
        xv = x_ref[...]
        r = lax.rsqrt(jnp.mean(xv * xv, axis=-1, keepdims=True) + RMS_EPS)
        o_ref[...] = (xv * r * g_ref[...]).astype(o_ref.dtype)

    return _call(
        body, name=name, grid=(s // tm,),
        in_specs=[pl.BlockSpec((tm, D_MODEL), lambda i: (i, 0)), pl.BlockSpec((1, D_MODEL), lambda i: (0, 0))],
        out_specs=pl.BlockSpec((tm, D_MODEL), lambda i: (i, 0)),
        out_shape=jax.ShapeDtypeStruct((s, D_MODEL), BF16),
        compiler_params=_params(("parallel",)),
    )(x, g)


def loss_head(y, target, name):
    s = y.shape[0]
    tm = ROW_TILE

    def body(y_ref, t_ref, dy_ref, l_ref):
        i = pl.program_id(0)
        e = y_ref[...] - t_ref[...]
        dy_ref[...] = e * (1.0 / D_MODEL)

        @pl.when(i == 0)
        def _():
            l_ref[...] = jnp.zeros_like(l_ref)

        per_row = jnp.sum(e * e, axis=-1, keepdims=True) * (1.0 / D_MODEL)
        l_ref[...] += jnp.broadcast_to(0.5 * jnp.sum(per_row, axis=0, keepdims=True), l_ref.shape)

    row = pl.BlockSpec((tm, D_MODEL), lambda i: (i, 0))
    return _call(
        body, name=name, grid=(s // tm,),
        in_specs=[row, row],
        out_specs=[row, pl.BlockSpec((8, 128), lambda i: (0, 0))],
        out_shape=[jax.ShapeDtypeStruct((s, D_MODEL), F32), jax.ShapeDtypeStruct((8, 128), F32)],
        compiler_params=_params(("arbitrary",)),
    )(y, target)


def _group_sum_matrix():
    r = lax.broadcasted_iota(jnp.int32, (PAIR, PAIR), 0) // HEAD_DIM
    c = lax.broadcasted_iota(jnp.int32, (PAIR, PAIR), 1) // HEAD_DIM
    return (r == c).astype(BF16)


def _head_mean(v, gmat):
    hi, lo = _split_bf16(v)
    return (_nn(hi, gmat) + _nn(lo, gmat)) * (1.0 / HEAD_DIM)


def headnorm_fwd(qkv, gq, gk, name):
    s = qkv.shape[0]
    tm = ROW_TILE

    def body(x_ref, gq_ref, gk_ref, o_ref):
        j = pl.program_id(1)
        gmat = _group_sum_matrix()
        gain = jnp.where(j == 0, gq_ref[...], gk_ref[...])
        for b in range(N_PAIRS):
            cols = slice(b * PAIR, (b + 1) * PAIR)
            xv = x_ref[:, cols].astype(F32)
            r = lax.rsqrt(_head_mean(xv * xv, gmat) + RMS_EPS)
            o_ref[:, cols] = (xv * r * gain).astype(o_ref.dtype)

    blk = pl.BlockSpec((tm, D_MODEL), lambda i, j: (i, j))
    gspec = pl.BlockSpec((1, PAIR), lambda i, j: (0, 0))
    return _call(
        body, name=name, grid=(s // tm, 2),
        in_specs=[blk, gspec, gspec], out_specs=blk,
        out_shape=jax.ShapeDtypeStruct((s, 2 * D_MODEL), BF16),
        compiler_params=_params(("parallel", "parallel")),
    )(qkv, gq, gk)


def headnorm_bwd(dq, dk, dv, qkv, gq, gk, name):
    s = qkv.shape[0]
    tm = ROW_TILE

    def body(dq_ref, dk_ref, dv_ref, x_ref, gq_ref, gk_ref, dx_ref, dg_ref):
        i = pl.program_id(0)
        j = pl.program_id(1)

        @pl.when((i == 0) & (j == 0))
        def _():
            dg_ref[...] = jnp.zeros_like(dg_ref)

        @pl.when(j == 2)
        def _():
            dx_ref[...] = dv_ref[...]

        for part, (d_ref, g_ref) in enumerate(((dq_ref, gq_ref), (dk_ref, gk_ref))):
            @pl.when(j == part)
            def _(part=part, d_ref=d_ref, g_ref=g_ref):
                gmat = _group_sum_matrix()
                gain = g_ref[...]
                dg = jnp.zeros((1, PAIR), F32)
                for b in range(N_PAIRS):
                    cols = slice(b * PAIR, (b + 1) * PAIR)
                    xv = x_ref[:, cols].astype(F32)
                    dn = d_ref[:, cols].astype(F32)
                    r = lax.rsqrt(_head_mean(xv * xv, gmat) + RMS_EPS)
                    xh = xv * r
                    dy = dn * gain
                    dx_ref[:, cols] = (r * (dy - xh * _head_mean(dy * xh, gmat))).astype(dx_ref.dtype)
                    dg = dg + jnp.sum(dn * xh, axis=0, keepdims=True)
                dg_ref[part:part + 1, :] += dg

    row = pl.BlockSpec((tm, D_MODEL), lambda i, j: (i, 0))
    blk = pl.BlockSpec((tm, D_MODEL), lambda i, j: (i, j))
    gspec = pl.BlockSpec((1, PAIR), lambda i, j: (0, 0))
    dx, dg = _call(
        body, name=name, grid=(s // tm, 3),
        in_specs=[row, row, row, blk, gspec, gspec],
        out_specs=[blk, pl.BlockSpec((8, PAIR), lambda i, j: (0, 0))],
        out_shape=[jax.ShapeDtypeStruct(qkv.shape, BF16), jax.ShapeDtypeStruct((8, PAIR), F32)],
        compiler_params=_params(("arbitrary", "arbitrary")),
    )(dq, dk, dv, qkv, gq, gk)
    return dx, dg[0:2]


def _relu2(a):
    r = jnp.maximum(a.astype(F32), 0.0)
    return r * r


def dense(a, w, *, nt, name, out_dtype, tm, relu2_in=False, relu2_grad=None, res=None, norm_gain=None,
          rms_back=None):
    s, k = a.shape
    n = w.shape[0] if nt else w.shape[1]
    nc = min(n, DENSE_CHUNK)

    def body(*refs):
        it = iter(refs)
        a_ref, w_ref = next(it), next(it)
        g_ref = next(it) if relu2_grad is not None else None
        r_ref = next(it) if res is not None else None
        ng_ref = next(it) if norm_gain is not None else None
        if rms_back is not None:
            x_ref, gain_ref, dres_ref = next(it), next(it), next(it)
        o_ref = next(it)
        h_ref = next(it) if norm_gain is not None else None
        dg_ref = next(it) if rms_back is not None else None
        av = a_ref[...]
        av = _relu2(av).astype(BF16) if relu2_in else av.astype(BF16)
        for c0 in range(0, n, nc):
            cols = slice(c0, c0 + nc)
            acc = _nt(av, w_ref[cols, :]) if nt else _nn(av, w_ref[:, cols])
            if g_ref is not None:
                acc = acc * (2.0 * jnp.maximum(g_ref[:, cols].astype(F32), 0.0))
            if r_ref is not None:
                acc = acc + r_ref[:, cols]
            o_ref[:, cols] = acc.astype(o_ref.dtype)
        if norm_gain is not None:
            xv = o_ref[...]
            r = lax.rsqrt(jnp.mean(xv * xv, axis=-1, keepdims=True) + RMS_EPS)
            h_ref[...] = (xv * r * ng_ref[...]).astype(h_ref.dtype)
        if rms_back is not None:
            dhv = o_ref[...]
            xv = x_ref[...]
            r = lax.rsqrt(jnp.mean(xv * xv, axis=-1, keepdims=True) + RMS_EPS)
            xh = xv * r
            dy = dhv * gain_ref[...]
            mdot = jnp.mean(dy * xh, axis=-1, keepdims=True)
            o_ref[...] = dres_ref[...] + r * (dy - xh * mdot)

            @pl.when(pl.program_id(0) == 0)
            def _():
                dg_ref[...] = jnp.zeros_like(dg_ref)

            dg_ref[0:1, :] += jnp.sum(dhv * xh, axis=0, keepdims=True)

    oblk = pl.BlockSpec((tm, n), lambda i: (i, 0))
    gblk = pl.BlockSpec((1, n), lambda i: (0, 0))
    in_specs = [pl.BlockSpec((tm, k), lambda i: (i, 0)), pl.BlockSpec(w.shape, lambda i: (0, 0))]
    args = [a, w]
    for e in (relu2_grad, res):
        if e is not None:
            in_specs.append(oblk)
            args.append(e)
    out_specs, out_shape = [oblk], [jax.ShapeDtypeStruct((s, n), out_dtype)]
    if norm_gain is not None:
        in_specs.append(gblk)
        args.append(norm_gain)
        out_specs.append(oblk)
        out_shape.append(jax.ShapeDtypeStruct((s, n), BF16))
    if rms_back is not None:
        in_specs += [oblk, gblk, oblk]
        args += list(rms_back)
        out_specs.append(pl.BlockSpec((8, n), lambda i: (0, 0)))
        out_shape.append(jax.ShapeDtypeStruct((8, n), F32))
    outs = _call(
        body, name=name, grid=(s // tm,), in_specs=in_specs, out_specs=out_specs, out_shape=out_shape,
        compiler_params=_params(("arbitrary",) if rms_back is not None else ("parallel",)),
    )(*args)
    if rms_back is not None:
        return outs[0], outs[1][0:1]
    return outs if norm_gain is not None else outs[0]


def dense_tn(a, b, *, tk, bn, name, relu2_in=False):
    s, k = a.shape
    n = b.shape[1]
    ts = TN_ROWS
    ns = s // ts
    nc = min(bn, DENSE_CHUNK)

    def body(a_ref, b_ref, o_ref, acc_ref):
        t = pl.program_id(2)

        @pl.when(t == 0)
        def _():
            acc_ref[...] = jnp.zeros_like(acc_ref)

        av = a_ref[...]
        av = _relu2(av).astype(BF16) if relu2_in else av.astype(BF16)
        at = av.T
        for c0 in range(0, bn, nc):
            cols = slice(c0, c0 + nc)
            acc_ref[:, cols] += _nn(at, b_ref[:, cols].astype(BF16))

        @pl.when(t == ns - 1)
        def _():
            o_ref[...] = acc_ref[...].astype(o_ref.dtype)

    return _call(
        body, name=name, grid=(k // tk, n // bn, ns),
        in_specs=[pl.BlockSpec((ts, tk), lambda kb, j, t: (t, kb)), pl.BlockSpec((ts, bn), lambda kb, j, t: (t, j))],
        out_specs=pl.BlockSpec((tk, bn), lambda kb, j, t: (kb, j)),
        out_shape=jax.ShapeDtypeStruct((k, n), BF16),
        scratch_shapes=[pltpu.VMEM((tk, bn), F32)],
        compiler_params=_params(("parallel", "parallel", "arbitrary")),
    )(a, b)


def _softplus(z):
    return jnp.maximum(z, 0.0) + jnp.log(1.0 + jnp.exp(-jnp.abs(z)))


def _tri_mats():
    t = SB_TILE
    r = jnp.arange(2 * t)[:, None] % t
    c = jnp.arange(t)[None, :]
    return jnp.stack([(r > c), (r < c), (r <= c)]).astype(BF16)


def _split_sum(v, u2):
    hi, lo = _split_bf16(v)
    return _nn(jnp.concatenate([hi, lo], axis=1), u2)


def sb_fwd(qkv, tri, name, rider=None):
    s = qkv.shape[0]
    t = SB_TILE

    def body(q_ref, k_ref, v_ref, u_ref, o_ref, lt_ref, c0_ref, c1_ref, acc0_ref, acc1_ref):
        i = pl.program_id(1)
        lane = lax.broadcasted_iota(jnp.int32, (t, PAIR), 1)
        causal = lax.broadcasted_iota(jnp.int32, (t, t), 1) < lax.broadcasted_iota(jnp.int32, (t, t), 0)
        qv = q_ref[...]
        u_after = u_ref[0]
        qhs = [jnp.where(lane < HEAD_DIM, qv, 0) * QK_SCALE, jnp.where(lane >= HEAD_DIM, qv, 0) * QK_SCALE]
        cs = [c0_ref, c1_ref]
        accs = [acc0_ref, acc1_ref]
        for r in cs + accs:
            r[...] = jnp.zeros_like(r)

        half = t // 2
        chains = [(h, slice(r * half, (r + 1) * half)) for h in range(2) for r in range(2)]

        def sweep(tiles):
            kvs = []
            for j, _ in tiles:
                off = pl.multiple_of(j * t, t)
                kvs.append((k_ref[pl.ds(off, t), :], v_ref[pl.ds(off, t), :]))
            work = [(k, h, rows) for k in range(len(tiles)) for h, rows in chains]
            zs = [_nt(qhs[h][rows, :], kvs[k][0]) for k, h, rows in work]
            ls = [-_softplus(z) for z in zs]
            ls = [jnp.where(causal[rows, :], l, 0.0) if tiles[k][1] else l for l, (k, h, rows) in zip(ls, work)]
            afters = [_split_sum(l, u_after) for l in ls]
            run = {}
            for w, (k, h, rows) in enumerate(work):
                c = run.get((h, rows.start), cs[h][rows, :])
                a = jnp.exp(zs[w] + ls[w] + afters[w] + c)
                if tiles[k][1]:
                    a = jnp.where(causal[rows, :], a, 0.0)
                av = _nn(a.astype(BF16), kvs[k][1])
                run[h, rows.start] = c + (afters[w][:, 0:1] + ls[w][:, 0:1])
                run["acc", h, rows.start] = run.get(("acc", h, rows.start), 0.0) + av
            for h, rows in chains:
                accs[h][rows, :] += run["acc", h, rows.start]
                cs[h][rows, :] = run[h, rows.start]

        def alive():
            return jnp.max(jnp.maximum(c0_ref[...], c1_ref[...])) > -SB_DEAD

        @pl.when(i == 0)
        def _():
            sweep([(i, True)])

        @pl.when(i > 0)
        def _():
            sweep([(i, True), (i - 1, False)])

        def cond(st):
            return (st[0] < i) & st[1]

        def step(st):
            sweep([(i - 1 - st[0], False)])
            return st[0] + 1, alive()

        swept, _ = lax.while_loop(cond, step, (jnp.minimum(i, 1), alive()))
        o_ref[...] = jnp.where(lane < HEAD_DIM, acc0_ref[...], acc1_ref[...]).astype(o_ref.dtype)
        lt_ref[...] = jnp.where(lane == 0, c0_ref[...], jnp.where(lane == 1, c1_ref[...],
                                jnp.where(lane == 2, swept.astype(F32), 0.0)))

    def first_last():
        p, i = pl.program_id(0), pl.program_id(1)
        return (p == 0) & (i == 0), (p == N_PAIRS - 1) & (i == s // t - 1)

    return hosted_call(
        body, first_last, rider, name=name, grid=(N_PAIRS, s // t),
        in_specs=[
            pl.BlockSpec((t, PAIR), lambda p, i: (i, p)),
            pl.BlockSpec((s, PAIR), lambda p, i: (0, N_PAIRS + p)),
            pl.BlockSpec((s, PAIR), lambda p, i: (0, 2 * N_PAIRS + p)),
            pl.BlockSpec((3, 2 * t, t), lambda p, i: (0, 0, 0)),
        ],
        out_specs=[pl.BlockSpec((t, PAIR), lambda p, i: (i, p)), pl.BlockSpec((None, t, PAIR), lambda p, i: (p, i, 0))],
        out_shape=[jax.ShapeDtypeStruct((s, D_MODEL), BF16), jax.ShapeDtypeStruct((N_PAIRS, s, PAIR), F32)],
        scratch_shapes=[pltpu.VMEM((t, 1), F32), pltpu.VMEM((t, 1), F32), pltpu.VMEM((t, PAIR), F32),
                        pltpu.VMEM((t, PAIR), F32)],
        args=(qkv, qkv, qkv, tri),
        compiler_params=_params(("arbitrary", "arbitrary")),
    )


def sb_bwd(qkv, do, ltot, tri, name, rider=None):
    s = qkv.shape[0]
    t = SB_TILE
    nq = s // t

    def body(q_ref, k_ref, v_ref, do_ref, lt_ref, u_ref, dq_ref, dk_ref, dv_ref,
             pl0_ref, pl1_ref, pg0_ref, pg1_ref, dqa0_ref, dqa1_ref, dka_ref, dva_ref):
        i = pl.program_id(1)

        @pl.when(i == 0)
        def _():
            dka_ref[...] = jnp.zeros_like(dka_ref)
            dva_ref[...] = jnp.zeros_like(dva_ref)

        lane = lax.broadcasted_iota(jnp.int32, (t, PAIR), 1)
        causal = lax.broadcasted_iota(jnp.int32, (t, t), 1) < lax.broadcasted_iota(jnp.int32, (t, t), 0)
        qv = q_ref[...]
        dov = do_ref[...]
        ltv = lt_ref[...]
        u_before = u_ref[1]
        u_upto = u_ref[2, 0:t, :]
        hms = [lane < HEAD_DIM, lane >= HEAD_DIM]
        qhs = [jnp.where(hm, qv, 0) * QK_SCALE for hm in hms]
        dohs = [jnp.where(hm, dov, 0) for hm in hms]
        ltots = [jnp.sum(jnp.where(lane == h, ltv, 0.0), axis=1, keepdims=True) for h in range(2)]
        pls = [pl0_ref, pl1_ref]
        pgs = [pg0_ref, pg1_ref]
        dqas = [dqa0_ref, dqa1_ref]
        for r in pls + pgs + dqas:
            r[...] = jnp.zeros_like(r)

        half = t // 2
        chains = [(h, slice(r * half, (r + 1) * half)) for h in range(2) for r in range(2)]

        def sweep(tiles):
            offs = [pl.multiple_of(j * t, t) for j, _ in tiles]
            kvs = [(k_ref[pl.ds(off, t), :], v_ref[pl.ds(off, t), :]) for off in offs]
            work = [(k, h, rows) for k in range(len(tiles)) for h, rows in chains]
            qcs = [qhs[h][rows, :] for k, h, rows in work]
            docs = [dohs[h][rows, :] for k, h, rows in work]
            zs = [_nt(qcs[w], kvs[k][0]) for w, (k, h, rows) in enumerate(work)]
            das = [_nt(docs[w], kvs[k][1]) for w, (k, h, rows) in enumerate(work)]
            sps = [_softplus(z) for z in zs]
            ls = [jnp.where(causal[rows, :], -sp, 0.0) if tiles[k][1] else -sp for sp, (k, h, rows) in zip(sps, work)]
            befores = [_split_sum(l, u_before) for l in ls]
            run = {}
            dzbs, abs_ = [], []
            for w, (k, h, rows) in enumerate(work):
                key = (h, rows.start)
                pl_c = run.get(("pl",) + key, pls[h][rows, :])
                pg_c = run.get(("pg",) + key, pgs[h][rows, :])
                a = jnp.exp(zs[w] + (ltots[h][rows, :] - pl_c - befores[w]))
                if tiles[k][1]:
                    a = jnp.where(causal[rows, :], a, 0.0)
                g = a * das[w]
                upto = _nn(g.astype(BF16), u_upto)
                dz = g - jnp.exp(zs[w] - sps[w]) * (pg_c + upto)
                if tiles[k][1]:
                    dz = jnp.where(causal[rows, :], dz, 0.0)
                dzbs.append(dz.astype(BF16))
                abs_.append(a.astype(BF16))
                run[("pl",) + key] = pl_c + (befores[w][:, t - 1:t] + ls[w][:, t - 1:t])
                run[("pg",) + key] = pg_c + upto[:, t - 1:t]
                run[("dq",) + key] = run.get(("dq",) + key, 0.0) + _nn(dzbs[w], kvs[k][0])
            for k in range(len(tiles)):
                mine = [w for w, wk in enumerate(work) if wk[0] == k]
                dka_ref[pl.ds(offs[k], t), :] += sum(_tn(dzbs[w], qcs[w]) for w in mine)
                dva_ref[pl.ds(offs[k], t), :] += sum(_tn(abs_[w], docs[w]) for w in mine)
            for h, rows in chains:
                key = (h, rows.start)
                dqas[h][rows, :] += run[("dq",) + key]
                pls[h][rows, :] = run[("pl",) + key]
                pgs[h][rows, :] = run[("pg",) + key]

        def step(j, carry):
            sweep([(j, False)])
            return carry

        swept = jnp.max(jnp.where(lane == 2, ltv, 0.0)).astype(jnp.int32)
        lax.fori_loop(i - swept, i, step, 0)
        sweep([(i, True)])

        dq_ref[...] = (jnp.where(lane < HEAD_DIM, dqa0_ref[...], dqa1_ref[...]) * QK_SCALE).astype(dq_ref.dtype)

        @pl.when(i == nq - 1)
        def _():
            dk_ref[...] = dka_ref[...].astype(dk_ref.dtype)
            dv_ref[...] = dva_ref[...].astype(dv_ref.dtype)

    qblk = pl.BlockSpec((t, PAIR), lambda p, i: (i, p))
    full = pl.BlockSpec((s, PAIR), lambda p, i: (0, p))
    out = jax.ShapeDtypeStruct((s, D_MODEL), BF16)

    def first_last():
        p, i = pl.program_id(0), pl.program_id(1)
        return (p == 0) & (i == 0), (p == N_PAIRS - 1) & (i == nq - 1)

    return hosted_call(
        body, first_last, rider, name=name, grid=(N_PAIRS, nq),
        in_specs=[
            qblk,
            pl.BlockSpec((s, PAIR), lambda p, i: (0, N_PAIRS + p)),
            pl.BlockSpec((s, PAIR), lambda p, i: (0, 2 * N_PAIRS + p)),
            qblk,
            pl.BlockSpec((None, t, PAIR), lambda p, i: (p, i, 0)),
            pl.BlockSpec((3, 2 * t, t), lambda p, i: (0, 0, 0)),
        ],
        out_specs=[qblk, full, full],
        out_shape=[out, out, out],
        scratch_shapes=[pltpu.VMEM((t, 1), F32)] * 4 + [pltpu.VMEM((t, PAIR), F32)] * 2
        + [pltpu.VMEM((s, PAIR), F32), pltpu.VMEM((s, PAIR), F32)],
        args=(qkv, qkv, qkv, do, ltot, tri),
        compiler_params=_params(("arbitrary", "arbitrary")),
    )


def _ring_onehot():
    r = lax.broadcasted_iota(jnp.int32, (N_REL_PAD, CA_RING), 0)
    w = lax.broadcasted_iota(jnp.int32, (N_REL_PAD, CA_RING), 1)
    idx = jnp.where(w <= CA_WIN, jnp.clip(CA_WIN - w, 0, 2 * MAX_REL), 2 * MAX_REL)
    return (r == idx).astype(F32)


def bias_table(rel_pad, name):
    def body(rb_ref, o_ref):
        ring = jnp.dot(rb_ref[...], _ring_onehot(), preferred_element_type=F32, precision=lax.Precision.HIGHEST)
        col = lax.broadcasted_iota(jnp.int32, (N_HEADS, CA_WIN), 1)

        def row(tl, carry):
            v = pltpu.roll(ring, tl, 1)[:, :CA_WIN]
            first = (tl // CHUNK) * CHUNK
            ok = (col >= first) & (col < first + (LEFT_CHUNKS + 1) * CHUNK)
            o_ref[tl] = jnp.where(ok, v, NEG_BIG)
            return carry

        lax.fori_loop(0, CA_QBLK, row, 0)

    return _call(
        body, name=name,
        in_specs=[pl.BlockSpec(memory_space=pltpu.VMEM)], out_specs=pl.BlockSpec(memory_space=pltpu.VMEM),
        out_shape=jax.ShapeDtypeStruct((CA_QBLK, N_HEADS, CA_WIN), F32),
        compiler_params=_params(),
    )(rel_pad)


def bias_fold(db, name):
    def body(db_ref, o_ref):
        zeros = jnp.zeros((N_HEADS, CA_RING - CA_WIN), F32)

        def row(tl, acc):
            v = jnp.concatenate([db_ref[tl], zeros], axis=1)
            return acc + pltpu.roll(v, (CA_RING - tl) % CA_RING, 1)

        ring = lax.fori_loop(0, CA_QBLK, row, jnp.zeros((N_HEADS, CA_RING), F32))
        o_ref[...] = lax.dot_general(ring, _ring_onehot(), (((1,), (1,)), ((), ())),
                                     preferred_element_type=F32, precision=lax.Precision.HIGHEST)

    return _call(
        body, name=name,
        in_specs=[pl.BlockSpec(memory_space=pltpu.VMEM)], out_specs=pl.BlockSpec(memory_space=pltpu.VMEM),
        out_shape=jax.ShapeDtypeStruct((N_HEADS, N_REL_PAD), F32),
        compiler_params=_params(),
    )(db)


def _ca_logits(qh, kw, bias_h, first_key_ok):
    sc = _nt(qh, kw) + bias_h
    col = lax.broadcasted_iota(jnp.int32, sc.shape, 1)
    return jnp.where(col >= first_key_ok, sc, NEG_BIG)


def ca_fwd(qkn, qkv, table, name, rider=None):
    s = qkv.shape[0]
    tq = CA_QBLK

    def body(q_ref, k_ref, v_ref, b_ref, o_ref, kp_ref, vp_ref):
        i = pl.program_id(1)

        @pl.when(i == 0)
        def _():
            kp_ref[0:CA_PAD, :] = jnp.zeros((CA_PAD, PAIR), BF16)
            vp_ref[0:CA_PAD, :] = jnp.zeros((CA_PAD, PAIR), BF16)
            kp_ref[CA_PAD:, :] = k_ref[...]
            vp_ref[CA_PAD:, :] = v_ref[...]

        lane = lax.broadcasted_iota(jnp.int32, (CA_SUB, PAIR), 1)
        hms = [lane < HEAD_DIM, lane >= HEAD_DIM]
        subs = range(tq // CA_SUB)
        rows = [slice(r * CA_SUB, (r + 1) * CA_SUB) for r in subs]
        cols = [slice(r * CA_SUB, r * CA_SUB + CA_SUBWIN) for r in subs]
        offs = [pl.multiple_of(i * tq + r * CA_SUB, CA_SUB) for r in subs]
        kws = [kp_ref[pl.ds(off, CA_SUBWIN), :] for off in offs]
        vws = [vp_ref[pl.ds(off, CA_SUBWIN), :] for off in offs]
        chains = [(r, h) for r in subs for h in range(2)]
        qhs = [jnp.where(hms[h], q_ref[rows[r], :], 0) * QK_SCALE for r, h in chains]
        scs = [_ca_logits(qhs[c], kws[r], b_ref[h, rows[r], cols[r]], CA_PAD - i * tq - r * CA_SUB)
               for c, (r, h) in enumerate(chains)]
        es = [jnp.exp(sc - jnp.max(sc, axis=1, keepdims=True)) for sc in scs]
        ps = [(e / jnp.sum(e, axis=1, keepdims=True)).astype(BF16) for e in es]
        outs = [_nn(ps[c], vws[r]) for c, (r, h) in enumerate(chains)]
        for r in subs:
            o_ref[rows[r], :] = jnp.where(lane < HEAD_DIM, outs[2 * r], outs[2 * r + 1]).astype(o_ref.dtype)

    def first_last():
        p, i = pl.program_id(0), pl.program_id(1)
        return (p == 0) & (i == 0), (p == N_PAIRS - 1) & (i == s // tq - 1)

    return hosted_call(
        body, first_last, rider, name=name, grid=(N_PAIRS, s // tq),
        in_specs=[
            pl.BlockSpec((tq, PAIR), lambda p, i: (i, p)),
            pl.BlockSpec((s, PAIR), lambda p, i: (0, N_PAIRS + p)),
            pl.BlockSpec((s, PAIR), lambda p, i: (0, 2 * N_PAIRS + p)),
            pl.BlockSpec((2, tq, CA_WIN), lambda p, i: (p, 0, 0)),
        ],
        out_specs=[pl.BlockSpec((tq, PAIR), lambda p, i: (i, p))],
        out_shape=[jax.ShapeDtypeStruct((s, D_MODEL), BF16)],
        scratch_shapes=[pltpu.VMEM((s + CA_PAD, PAIR), BF16), pltpu.VMEM((s + CA_PAD, PAIR), BF16)],
        args=(qkn, qkn, qkv, table),
        compiler_params=_params(("arbitrary", "arbitrary")),
    )


def ca_bwd(qkn, qkv, o, do, table, name, rider=None):
    s = qkv.shape[0]
    tq = CA_QBLK
    nq = s // tq

    def body(q_ref, k_ref, v_ref, o_ref, do_ref, b_ref, dq_ref, dk_ref, dv_ref, db_ref,
             kp_ref, vp_ref, dka_ref, dva_ref):
        i = pl.program_id(1)

        @pl.when(i == 0)
        def _():
            kp_ref[0:CA_PAD, :] = jnp.zeros((CA_PAD, PAIR), BF16)
            vp_ref[0:CA_PAD, :] = jnp.zeros((CA_PAD, PAIR), BF16)
            kp_ref[CA_PAD:, :] = k_ref[...]
            vp_ref[CA_PAD:, :] = v_ref[...]
            dka_ref[...] = jnp.zeros_like(dka_ref)
            dva_ref[...] = jnp.zeros_like(dva_ref)
            db_ref[...] = jnp.zeros_like(db_ref)

        lane = lax.broadcasted_iota(jnp.int32, (CA_SUB, PAIR), 1)
        hms = [lane < HEAD_DIM, lane >= HEAD_DIM]
        subs = range(tq // CA_SUB)
        rows = [slice(r * CA_SUB, (r + 1) * CA_SUB) for r in subs]
        cols = [slice(r * CA_SUB, r * CA_SUB + CA_SUBWIN) for r in subs]
        offs = [pl.multiple_of(i * tq + r * CA_SUB, CA_SUB) for r in subs]
        kws = [kp_ref[pl.ds(off, CA_SUBWIN), :] for off in offs]
        vws = [vp_ref[pl.ds(off, CA_SUBWIN), :] for off in offs]
        dovs = [do_ref[rows[r], :] for r in subs]
        prods = [dovs[r].astype(F32) * o_ref[rows[r], :].astype(F32) for r in subs]
        chains = [(r, h) for r in subs for h in range(2)]
        qhs = [jnp.where(hms[h], q_ref[rows[r], :], 0) * QK_SCALE for r, h in chains]
        dohs = [jnp.where(hms[h], dovs[r], 0) for r, h in chains]
        deltas = [jnp.sum(jnp.where(hms[h], prods[r], 0.0), axis=1, keepdims=True) for r, h in chains]
        scs = [_ca_logits(qhs[c], kws[r], b_ref[h, rows[r], cols[r]], CA_PAD - i * tq - r * CA_SUB)
               for c, (r, h) in enumerate(chains)]
        dps = [_nt(dohs[c], vws[r]) for c, (r, h) in enumerate(chains)]
        es = [jnp.exp(sc - jnp.max(sc, axis=1, keepdims=True)) for sc in scs]
        ps = [e / jnp.sum(e, axis=1, keepdims=True) for e in es]
        dss = [ps[c] * (dps[c] - deltas[c]) for c in range(len(chains))]
        for c, (r, h) in enumerate(chains):
            db_ref[h, rows[r], cols[r]] += dss[c]
        dsbs = [ds.astype(BF16) for ds in dss]
        pbs = [p.astype(BF16) for p in ps]
        dqs = [_nn(dsbs[c], kws[r]) for c, (r, h) in enumerate(chains)]
        for r in subs:
            dq_ref[rows[r], :] = (jnp.where(lane < HEAD_DIM, dqs[2 * r], dqs[2 * r + 1]) * QK_SCALE).astype(dq_ref.dtype)
        for r in subs:
            dka_ref[pl.ds(offs[r], CA_SUBWIN), :] += _tn(dsbs[2 * r], qhs[2 * r]) + _tn(dsbs[2 * r + 1], qhs[2 * r + 1])
            dva_ref[pl.ds(offs[r], CA_SUBWIN), :] += _tn(pbs[2 * r], dohs[2 * r]) + _tn(pbs[2 * r + 1], dohs[2 * r + 1])

        @pl.when(i == nq - 1)
        def _():
            dk_ref[...] = dka_ref[CA_PAD:, :].astype(dk_ref.dtype)
            dv_ref[...] = dva_ref[CA_PAD:, :].astype(dv_ref.dtype)

    qblk = pl.BlockSpec((tq, PAIR), lambda p, i: (i, p))
    full = pl.BlockSpec((s, PAIR), lambda p, i: (0, p))
    tblk = pl.BlockSpec((2, tq, CA_WIN), lambda p, i: (p, 0, 0))
    out = jax.ShapeDtypeStruct((s, D_MODEL), BF16)

    def first_last():
        p, i = pl.program_id(0), pl.program_id(1)
        return (p == 0) & (i == 0), (p == N_PAIRS - 1) & (i == nq - 1)

    return hosted_call(
        body, first_last, rider, name=name, grid=(N_PAIRS, nq),
        in_specs=[
            qblk,
            pl.BlockSpec((s, PAIR), lambda p, i: (0, N_PAIRS + p)),
            pl.BlockSpec((s, PAIR), lambda p, i: (0, 2 * N_PAIRS + p)),
            qblk, qblk, tblk,
        ],
        out_specs=[qblk, full, full, tblk],
        out_shape=[out, out, out, jax.ShapeDtypeStruct((N_HEADS, tq, CA_WIN), F32)],
        scratch_shapes=[pltpu.VMEM((s + CA_PAD, PAIR), BF16), pltpu.VMEM((s + CA_PAD, PAIR), BF16),
                        pltpu.VMEM((s + CA_PAD, PAIR), F32), pltpu.VMEM((s + CA_PAD, PAIR), F32)],
        args=(qkn, qkn, qkv, o, do, table),
        compiler_params=_params(("arbitrary", "arbitrary")),
    )


def _me():
    x, y, c = lax.axis_index("x"), lax.axis_index("y"), lax.axis_index("c")
    return x, y, c, 4 * x + 2 * y + c


def _peer(k):
    x, y, c, _ = _me()
    px, py, pc = x ^ ((k >> 2) & 1), y ^ ((k >> 1) & 1), c ^ (k & 1)
    return (px, py, pc), 4 * px + 2 * py + pc


ANY = pl.BlockSpec(memory_space=pl.ANY)


class Exchange:
    def __init__(self, n, copy, in_arrays, out_shape):
        self.n, self.copy, self.in_arrays, self.out_shape = n, copy, list(in_arrays), list(out_shape)
        self.sems = [pltpu.SemaphoreType.DMA((n, N_DEV - 1)), pltpu.SemaphoreType.DMA((n, N_DEV - 1)),
                     pltpu.SemaphoreType.DMA((n,))]

    def _copies(self, data, sems):
        send_sems, recv_sems, local_sems = sems
        _, _, _, me = _me()
        local, sends, recvs = [], [], []
        for a in range(self.n):
            s_ref, d_ref = self.copy(data, a, me, me)
            local.append(pltpu.make_async_copy(s_ref, d_ref, local_sems.at[a]))
            for k in range(1, N_DEV):
                peer, pidx = _peer(k)
                s_ref, d_ref = self.copy(data, a, me, pidx)
                sends.append(pltpu.make_async_remote_copy(
                    src_ref=s_ref, dst_ref=d_ref, send_sem=send_sems.at[a, k - 1], recv_sem=recv_sems.at[a, k - 1],
                    device_id=peer, device_id_type=MESH))
                s_ref, d_ref = self.copy(data, a, pidx, me)
                recvs.append(pltpu.make_async_remote_copy(
                    src_ref=s_ref, dst_ref=d_ref, send_sem=send_sems.at[a, k - 1], recv_sem=recv_sems.at[a, k - 1],
                    device_id=peer, device_id_type=MESH))
        return local, sends, recvs

    def start(self, data, sems):
        local, sends, _ = self._copies(data, sems)
        for cp in local + sends:
            cp.start()

    def finish(self, data, sems):
        local, sends, recvs = self._copies(data, sems)
        for cp in recvs:
            cp.wait_recv()
        for cp in sends:
            cp.wait_send()
        for cp in local:
            cp.wait()


def run_exchange(ex, name):
    n_data = len(ex.in_arrays) + len(ex.out_shape)

    def body(*refs):
        ex.start(refs[:n_data], refs[n_data:])
        ex.finish(refs[:n_data], refs[n_data:])

    return _call(
        body, name=name,
        in_specs=[ANY] * len(ex.in_arrays), out_specs=[ANY] * len(ex.out_shape), out_shape=ex.out_shape,
        scratch_shapes=ex.sems, compiler_params=pltpu.CompilerParams(has_side_effects=True),
    )(*ex.in_arrays)


def hosted_call(core_body, first_last, rider, *, in_specs, out_specs, out_shape, scratch_shapes, args, **kw):
    if rider is None:
        return _call(core_body, in_specs=in_specs, out_specs=out_specs, out_shape=out_shape,
                     scratch_shapes=scratch_shapes, **kw)(*args), []
    n_in, n_out, n_scr = len(in_specs), len(out_specs), len(scratch_shapes)
    r_in, r_out = len(rider.in_arrays), len(rider.out_shape)

    def body(*refs):
        it = iter(refs)
        take = lambda m: [next(it) for _ in range(m)]
        c_in, x_in, c_out, x_out, c_scr, sems = take(n_in), take(r_in), take(n_out), take(r_out), take(n_scr), take(3)
        first, last = first_last()

        @pl.when(first)
        def _():
            rider.start(x_in + x_out, sems)

        core_body(*c_in, *c_out, *c_scr)

        @pl.when(last)
        def _():
            rider.finish(x_in + x_out, sems)

    outs = _call(
        body, in_specs=list(in_specs) + [ANY] * r_in, out_specs=list(out_specs) + [ANY] * r_out,
        out_shape=list(out_shape) + rider.out_shape, scratch_shapes=list(scratch_shapes) + rider.sems, **kw,
    )(*args, *rider.in_arrays)
    return outs[:n_out], outs[n_out:]


def _window(full_ref, shard_shape, idx):
    a, b = shard_shape
    if a == full_ref.shape[0]:
        return full_ref.at[:, pl.ds(pl.multiple_of(idx * b, 128), b)]
    return full_ref.at[pl.ds(pl.multiple_of(idx * a, 8), a), :]


def gather_exchange(shards, full_shapes, keys):
    nw = len(shards)

    def copy(data, a, src_idx, dst_idx):
        w, layer = keys[a]
        s_ref = data[w].at[layer]
        return s_ref, _window(data[nw + a], s_ref.shape, src_idx)

    out_shape = [jax.ShapeDtypeStruct(full_shapes[w], shards[w].dtype) for w, _ in keys]
    return Exchange(len(keys), copy, shards, out_shape)


def scatter_exchange(partials, shard_shapes):
    n = len(partials)

    def copy(data, a, src_idx, dst_idx):
        return _window(data[a], shard_shapes[a], dst_idx), data[n + a].at[src_idx]

    out_shape = [jax.ShapeDtypeStruct((N_DEV,) + tuple(ss), p.dtype) for ss, p in zip(shard_shapes, partials)]
    return Exchange(n, copy, partials, out_shape)


def _adamw(w, g, m, v):
    m = ADAM_B1 * m + (1.0 - ADAM_B1) * g
    v = ADAM_B2 * v + (1.0 - ADAM_B2) * (g * g)
    m_hat = m / (1.0 - ADAM_B1 ** ADAM_STEP)
    v_hat = v / (1.0 - ADAM_B2 ** ADAM_STEP)
    delta = -ADAM_LR * (m_hat / (jnp.sqrt(v_hat) + ADAM_EPS) + ADAM_WD * w)
    return delta, m, v


def adam_shard(parts, w, m, v, name):
    _, r, c = w.shape
    tr = min(r, 256)
    nr = r // tr

    def body(*refs):
        p_refs = refs[:DEPTH]
        w_ref, m_ref, v_ref, g_out, d_out, m_out, v_out = refs[DEPTH:]
        layer = pl.program_id(0)
        for li in range(DEPTH):
            @pl.when(layer == li)
            def _(p_ref=p_refs[li]):
                g = p_ref[0].astype(F32)
                for e in range(1, N_DEV):
                    g = g + p_ref[e].astype(F32)
                delta, mn, vn = _adamw(w_ref[...], g, m_ref[...], v_ref[...])
                g_out[...] = g
                d_out[...] = delta
                m_out[...] = mn
                v_out[...] = vn

    def part_spec(li):
        return pl.BlockSpec((N_DEV, tr, c), lambda layer, i: (0, jnp.where(layer == li, i, 0), 0))

    blk = pl.BlockSpec((None, tr, c), lambda layer, i: (layer, i, 0))
    out = jax.ShapeDtypeStruct(w.shape, F32)
    return _call(
        body, name=name, grid=(DEPTH, nr),
        in_specs=[part_spec(li) for li in range(DEPTH)] + [blk, blk, blk],
        out_specs=[blk] * 4, out_shape=[out] * 4,
        compiler_params=_params(("arbitrary", "arbitrary")),
    )(*parts, w, m, v)


def small_allreduce_adam(pk, w, m, v, name):
    def body(pk_ref, w_ref, m_ref, v_ref, g_out, d_out, m_out, v_out, all_ref, send_sems, recv_sems):
        _, _, _, me = _me()
        all_ref[me] = pk_ref[...]
        sends = []
        for k in range(1, N_DEV):
            peer, _ = _peer(k)
            cp = pltpu.make_async_remote_copy(
                src_ref=pk_ref, dst_ref=all_ref.at[me], send_sem=send_sems.at[k - 1],
                recv_sem=recv_sems.at[k - 1], device_id=peer, device_id_type=MESH)
            cp.start()
            sends.append(cp)
        for k in range(1, N_DEV):
            peer, pidx = _peer(k)
            pltpu.make_async_remote_copy(
                src_ref=pk_ref, dst_ref=all_ref.at[pidx], send_sem=send_sems.at[k - 1],
                recv_sem=recv_sems.at[k - 1], device_id=peer, device_id_type=MESH).wait_recv()
        for cp in sends:
            cp.wait_send()
        g = all_ref[0]
        for e in range(1, N_DEV):
            g = g + all_ref[e]
        delta, mn, vn = _adamw(w_ref[...], g, m_ref[...], v_ref[...])
        g_out[...] = g
        d_out[...] = delta
        m_out[...] = mn
        v_out[...] = vn

    vm = pl.BlockSpec(memory_space=pltpu.VMEM)
    out = jax.ShapeDtypeStruct((PACK_ROWS, 128), F32)
    return _call(
        body, name=name,
        in_specs=[vm] * 4, out_specs=[vm] * 4, out_shape=[out] * 4,
        scratch_shapes=[pltpu.VMEM((N_DEV, PACK_ROWS, 128), F32), pltpu.SemaphoreType.DMA((N_DEV - 1,)),
                        pltpu.SemaphoreType.DMA((N_DEV - 1,))],
        compiler_params=pltpu.CompilerParams(has_side_effects=True),
    )(pk, w, m, v)


def _pack_small(mix, ffn, qn, kn, rel, loss):
    flat = jnp.concatenate([mix.reshape(-1), ffn.reshape(-1), qn.reshape(-1), kn.reshape(-1), rel.reshape(-1),
                            loss.reshape(-1)])
    return jnp.pad(flat, (0, PACK_LEN - flat.shape[0])).reshape(PACK_ROWS, 128)


def _unpack_small(pk):
    flat = pk.reshape(-1)
    return (flat[OFF_MIX:OFF_FFN].reshape(DEPTH, D_MODEL), flat[OFF_FFN:OFF_QN].reshape(DEPTH, D_MODEL),
            flat[OFF_QN:OFF_KN].reshape(2, HEAD_DIM), flat[OFF_KN:OFF_REL].reshape(2, HEAD_DIM),
            flat[OFF_REL:OFF_LOSS].reshape(2, N_HEADS, N_REL), flat[OFF_LOSS])


def _pair_gain(g):
    return jnp.concatenate([g, g]).reshape(1, PAIR)


def kernel(x, mix_norm, w_qkv, w_o, q_norm, k_norm, rel_bias, ffn_norm, w_up, w_down, loss_target, m_mix_norm, m_w_qkv, m_w_o, m_q_norm, m_k_norm, m_rel_bias, m_ffn_norm, m_w_up, m_w_down, v_mix_norm, v_w_qkv, v_w_o, v_q_norm, v_k_norm, v_rel_bias, v_ffn_norm, v_w_up, v_w_down):
    x0 = x[0]
    target = loss_target[0]
    shard_shapes = [w_qkv.shape[1:], w_o.shape[1:], w_up.shape[1:], w_down.shape[1:]]
    full_shapes = [(D_MODEL, 3 * D_MODEL), (D_MODEL, D_MODEL), (D_MODEL, D_FF), (D_FF, D_MODEL)]
    shards = [w_qkv.astype(BF16), w_o.astype(BF16), w_up.astype(BF16), w_down.astype(BF16)]
    nw = len(shards)
    weight = {}

    def fetch(keys, host, *args):
        outs, arrived = host(*args, rider=gather_exchange(shards, full_shapes, keys))
        weight.update(zip(keys, arrived))
        return outs

    first = [(0, 0)]
    weight.update(zip(first, run_exchange(gather_exchange(shards, full_shapes, first), "gather_first")))
    riding = {0: [(w, 0) for w in range(1, nw)] + [(w, 1) for w in range(nw)],
              1: [(w, 2) for w in range(nw)], 2: [(w, 3) for w in range(nw)]}
    tri = _tri_mats()

    saved = []
    xin = x0
    h0 = rms_fwd(x0, mix_norm[0:1], "rms_mix_0")
    for layer in range(DEPTH):
        mixer_b = layer % 2 == 1
        idx = layer // 2
        qkv_raw = dense(h0, weight[0, layer], nt=False, name=f"qkv_{layer}", out_dtype=BF16, tm=1024)
        if mixer_b:
            gq, gk = _pair_gain(q_norm[idx]), _pair_gain(k_norm[idx])
            qkn = headnorm_fwd(qkv_raw, gq, gk, f"headnorm_{layer}")
            rel_pad = jnp.pad(rel_bias[idx], ((0, 0), (0, N_REL_PAD - N_REL)))
            table = jnp.transpose(bias_table(rel_pad, f"bias_table_{layer}"), (1, 0, 2))
            if layer in riding:
                (o,) = fetch(riding[layer], ca_fwd, qkn, qkv_raw, table, f"ca_fwd_{layer}")
            else:
                (o,), _ = ca_fwd(qkn, qkv_raw, table, f"ca_fwd_{layer}")
            attn_saved = (qkv_raw, qkn, table)
        else:
            o, ltot = fetch(riding[layer], sb_fwd, qkv_raw, tri, f"sb_fwd_{layer}")
            attn_saved = (qkv_raw, ltot)
        x1, h1 = dense(o, weight[1, layer], nt=False, name=f"attn_out_{layer}", out_dtype=F32, tm=1024, res=xin,
                       norm_gain=ffn_norm[layer:layer + 1])
        pre = dense(h1, weight[2, layer], nt=False, name=f"up_{layer}", out_dtype=BF16, tm=1024)
        saved.append((xin, h0, attn_saved, o, x1, h1, pre))
        if layer + 1 < DEPTH:
            xin, h0 = dense(pre, weight[3, layer], nt=False, name=f"down_{layer}", out_dtype=F32, tm=512, relu2_in=True,
                            res=x1, norm_gain=mix_norm[layer + 1:layer + 2])
        else:
            xin = dense(pre, weight[3, layer], nt=False, name=f"down_{layer}", out_dtype=F32, tm=512, relu2_in=True,
                        res=x1)

    dx, loss_blk = loss_head(xin, target, "loss_head")

    partial = {}
    received = {}
    d_mix, d_ffn = [None] * DEPTH, [None] * DEPTH
    d_qn, d_kn, d_rel = [None] * 2, [None] * 2, [None] * 2

    def leaving():
        keys = [key for key in sorted(partial) if key not in received]
        return keys, scatter_exchange([partial[key] for key in keys], [shard_shapes[key[0]] for key in keys])

    def send(host, *args):
        keys, ex = leaving()
        outs, arrived = host(*args, rider=ex)
        received.update(zip(keys, arrived))
        return outs

    for layer in reversed(range(DEPTH)):
        mixer_b = layer % 2 == 1
        idx = layer // 2
        xin, h0, attn_saved, o, x1, h1, pre = saved[layer]
        g_qkv, g_o, g_up, g_down = (weight[w, layer] for w in range(nw))
        dpre = dense(dx, g_down, nt=True, name=f"d_pre_{layer}", out_dtype=BF16, tm=512, relu2_grad=pre)
        partial[3, layer] = dense_tn(pre, dx, tk=2048, bn=1024, name=f"dw_down_{layer}", relu2_in=True)
        partial[2, layer] = dense_tn(h1, dpre, tk=1024, bn=2048, name=f"dw_up_{layer}")
        dx, d_ffn[layer] = dense(dpre, g_up, nt=True, name=f"d_h1_{layer}", out_dtype=F32, tm=512,
                                 rms_back=(x1, ffn_norm[layer:layer + 1], dx))
        do = dense(dx, g_o, nt=True, name=f"d_o_{layer}", out_dtype=BF16, tm=1024)
        partial[1, layer] = dense_tn(o, dx, tk=1024, bn=1024, name=f"dw_o_{layer}")
        if mixer_b:
            qkv_raw, qkn, table = attn_saved
            dq, dk, dv, dtab = send(ca_bwd, qkn, qkv_raw, o, do, table, f"ca_bwd_{layer}")
            gq, gk = _pair_gain(q_norm[idx]), _pair_gain(k_norm[idx])
            dqkv, dgain = headnorm_bwd(dq, dk, dv, qkv_raw, gq, gk, f"headnorm_bwd_{layer}")
            d_qn[idx] = dgain[0, :HEAD_DIM] + dgain[0, HEAD_DIM:]
            d_kn[idx] = dgain[1, :HEAD_DIM] + dgain[1, HEAD_DIM:]
            d_rel[idx] = bias_fold(jnp.transpose(dtab, (1, 0, 2)), f"bias_fold_{layer}")[:, :N_REL]
        else:
            qkv_raw, ltot = attn_saved
            dq, dk, dv = send(sb_bwd, qkv_raw, do, ltot, tri, f"sb_bwd_{layer}")
            dqkv = jnp.concatenate([dq, dk, dv], axis=1)
        partial[0, layer] = dense_tn(h0, dqkv, tk=1024, bn=1536, name=f"dw_qkv_{layer}")
        dx, d_mix[layer] = dense(dqkv, g_qkv, nt=True, name=f"d_h0_{layer}", out_dtype=F32, tm=512,
                                 rms_back=(xin, mix_norm[layer:layer + 1], dx))

    grad_x = dx[None]

    keys, ex = leaving()
    received.update(zip(keys, run_exchange(ex, "scatter_rest")))

    def big(wi, w, m, v, name):
        return adam_shard([received[wi, layer] for layer in range(DEPTH)], w, m, v, name)

    a_qkv = big(0, w_qkv, m_w_qkv, v_w_qkv, "adam_qkv")
    a_o = big(1, w_o, m_w_o, v_w_o, "adam_o")
    a_up = big(2, w_up, m_w_up, v_w_up, "adam_up")
    a_down = big(3, w_down, m_w_down, v_w_down, "adam_down")

    pk = _pack_small(jnp.concatenate(d_mix), jnp.concatenate(d_ffn), jnp.stack(d_qn), jnp.stack(d_kn),
                     jnp.stack(d_rel), loss_blk[0, 0])
    zero = jnp.zeros((), F32)
    pw = _pack_small(mix_norm, ffn_norm, q_norm, k_norm, rel_bias, zero)
    pm = _pack_small(m_mix_norm, m_ffn_norm, m_q_norm, m_k_norm, m_rel_bias, zero)
    pv = _pack_small(v_mix_norm, v_ffn_norm, v_q_norm, v_k_norm, v_rel_bias, zero)
    s_g, s_d, s_m, s_v = small_allreduce_adam(pk, pw, pm, pv, "small_allreduce_adam")
    g_mix, g_ffn, g_qn, g_kn, g_rel, loss = _unpack_small(s_g)
    dl_mix, dl_ffn, dl_qn, dl_kn, dl_rel, _ = _unpack_small(s_d)
    nm_mix, nm_ffn, nm_qn, nm_kn, nm_rel, _ = _unpack_small(s_m)
    nv_mix, nv_ffn, nv_qn, nv_kn, nv_rel, _ = _unpack_small(s_v)

    return (loss, grad_x,
            g_mix, a_qkv[0], a_o[0], g_qn, g_kn, g_rel, g_ffn, a_up[0], a_down[0],
            dl_mix, a_qkv[1], a_o[1], dl_qn, dl_kn, dl_rel, dl_ffn, a_up[1], a_down[1],
            nm_mix, a_qkv[2], a_o[2], nm_qn, nm_kn, nm_rel, nm_ffn, a_up[2], a_down[2],
            nv_mix, a_qkv[3], a_o[3], nv_qn, nv_kn, nv_rel, nv_ffn, a_up[3], a_down[3])
```

```python
import jax
import jax.numpy as jnp
from jax import lax
from jax.experimental import pallas as pl
from jax.experimental.pallas import tpu as pltpu

F32 = jnp.float32
BF16 = jnp.bfloat16
MESH = pl.DeviceIdType.MESH

D_MODEL = 1024
N_HEADS = 16
HEAD_DIM = 64
PAIR = 2 * HEAD_DIM
N_PAIRS = N_HEADS // 2
D_FF = 4 * D_MODEL
DEPTH = 4
N_DEV = 8
RMS_EPS = 1e-6
QK_SCALE = HEAD_DIM ** -0.5

SB_TILE = 256
SB_DEAD = 104.0
Q_SUBS = 2

CHUNK = 64
LEFT_CHUNKS = 8
CA_QBLK = 256
CA_PAD = LEFT_CHUNKS * CHUNK
CA_WIN = CA_QBLK + CA_PAD
CA_SUB = 128
CA_SUBWIN = CA_SUB + CA_PAD
CA_RING = 1024
MAX_REL = 256
N_REL = 2 * MAX_REL + 1
N_REL_PAD = 640
NEG_BIG = -1e30

ADAM_LR = 0.001
ADAM_B1 = 0.9
ADAM_B2 = 0.999
ADAM_EPS = 1e-08
ADAM_WD = 0.01
ADAM_STEP = 10

ROW_TILE = 512
DENSE_CHUNK = 512
TN_ROWS = 1024
VMEM_LIMIT = 56 * 1024 * 1024

OFF_MIX = 0
OFF_FFN = OFF_MIX + DEPTH * D_MODEL
OFF_QN = OFF_FFN + DEPTH * D_MODEL
OFF_KN = OFF_QN + 2 * HEAD_DIM
OFF_REL = OFF_KN + 2 * HEAD_DIM
OFF_LOSS = OFF_REL + 2 * N_HEADS * N_REL
PACK_ROWS = 200
PACK_LEN = PACK_ROWS * 128


def _call(body, **kw):
    return pl.pallas_call(body, **kw)


def _params(sem=None, vmem=VMEM_LIMIT):
    if sem is None:
        return pltpu.CompilerParams(vmem_limit_bytes=vmem)
    return pltpu.CompilerParams(dimension_semantics=sem, vmem_limit_bytes=vmem)


def _nt(a, b):
    return lax.dot_general(a, b, (((1,), (1,)), ((), ())), preferred_element_type=F32)


def _tn(a, b):
    return lax.dot_general(a, b, (((0,), (0,)), ((), ())), preferred_element_type=F32)


def _nn(a, b):
    return jnp.dot(a, b, preferred_element_type=F32)


def _split_bf16(v):
    hi = v.astype(BF16)
    lo = (v - hi.astype(F32)).astype(BF16)
    return hi, lo


def rms_fwd(x, g, name):
    s = x.shape[0]
    tm = ROW_TILE

    def body(x_ref, g_ref, o_ref):
        xv = x_ref[...]
        r = lax.rsqrt(jnp.mean(xv * xv, axis=-1, keepdims=True) + RMS_EPS)
        o_ref[...] = (xv * r * g_ref[...]).astype(o_ref.dtype)

    return _call(
        body, name=name, grid=(s // tm,),
        in_specs=[pl.BlockSpec((tm, D_MODEL), lambda i: (i, 0)), pl.BlockSpec((1, D_MODEL), lambda i: (0, 0))],
        out_specs=pl.BlockSpec((tm, D_MODEL), lambda i: (i, 0)),
        out_shape=jax.ShapeDtypeStruct((s, D_MODEL), BF16),
        compiler_params=_params(("parallel",)),
    )(x, g)


def loss_head(y, target, name):
    s = y.shape[0]
    tm = ROW_TILE

    def body(y_ref, t_ref, dy_ref, l_ref):
        i = pl.program_id(0)
        e = y_ref[...] - t_ref[...]
        dy_ref[...] = e * (1.0 / D_MODEL)

        @pl.when(i == 0)
        def _():
            l_ref[...] = jnp.zeros_like(l_ref)

        per_row = jnp.sum(e * e, axis=-1, keepdims=True) * (1.0 / D_MODEL)
        l_ref[...] += jnp.broadcast_to(0.5 * jnp.sum(per_row, axis=0, keepdims=True), l_ref.shape)

    row = pl.BlockSpec((tm, D_MODEL), lambda i: (i, 0))
    return _call(
        body, name=name, grid=(s // tm,),
        in_specs=[row, row],
        out_specs=[row, pl.BlockSpec((8, 128), lambda i: (0, 0))],
        out_shape=[jax.ShapeDtypeStruct((s, D_MODEL), F32), jax.ShapeDtypeStruct((8, 128), F32)],
        compiler_params=_params(("arbitrary",)),
    )(y, target)


def _group_sum_matrix():
    r = lax.broadcasted_iota(jnp.int32, (PAIR, PAIR), 0) // HEAD_DIM
    c = lax.broadcasted_iota(jnp.int32, (PAIR, PAIR), 1) // HEAD_DIM
    return (r == c).astype(BF16)


def _head_mean(v, gmat):
    hi, lo = _split_bf16(v)
    return (_nn(hi, gmat) + _nn(lo, gmat)) * (1.0 / HEAD_DIM)


def headnorm_fwd(qkv, gq, gk, name):
    s = qkv.shape[0]
    tm = ROW_TILE

    def body(x_ref, gq_ref, gk_ref, o_ref):
        j = pl.program_id(1)
        gmat = _group_sum_matrix()
        gain = jnp.where(j == 0, gq_ref[...], gk_ref[...])
        for b in range(N_PAIRS):
            cols = slice(b * PAIR, (b + 1) * PAIR)
            xv = x_ref[:, cols].astype(F32)
            r = lax.rsqrt(_head_mean(xv * xv, gmat) + RMS_EPS)
            o_ref[:, cols] = (xv * r * gain).astype(o_ref.dtype)

    blk = pl.BlockSpec((tm, D_MODEL), lambda i, j: (i, j))
    gspec = pl.BlockSpec((1, PAIR), lambda i, j: (0, 0))
    return _call(
        body, name=name, grid=(s // tm, 2),
        in_specs=[blk, gspec, gspec], out_specs=blk,
        out_shape=jax.ShapeDtypeStruct((s, 2 * D_MODEL), BF16),
        compiler_params=_params(("parallel", "parallel")),
    )(qkv, gq, gk)


def headnorm_bwd(dq, dk, dv, qkv, gq, gk, name):
    s = qkv.shape[0]
    tm = ROW_TILE

    def body(dq_ref, dk_ref, dv_ref, x_ref, gq_ref, gk_ref, dx_ref, dg_ref):
        i = pl.program_id(0)
        j = pl.program_id(1)

        @pl.when((i == 0) & (j == 0))
        def _():
            dg_ref[...] = jnp.zeros_like(dg_ref)

        @pl.when(j == 2)
        def _():
            dx_ref[...] = dv_ref[...]

        for part, (d_ref, g_ref) in enumerate(((dq_ref, gq_ref), (dk_ref, gk_ref))):
            @pl.when(j == part)
            def _(part=part, d_ref=d_ref, g_ref=g_ref):
                gmat = _group_sum_matrix()
                gain = g_ref[...]
                dg = jnp.zeros((1, PAIR), F32)
                for b in range(N_PAIRS):
                    cols = slice(b * PAIR, (b + 1) * PAIR)
                    xv = x_ref[:, cols].astype(F32)
                    dn = d_ref[:, cols].astype(F32)
                    r = lax.rsqrt(_head_mean(xv * xv, gmat) + RMS_EPS)
                    xh = xv * r
                    dy = dn * gain
                    dx_ref[:, cols] = (r * (dy - xh * _head_mean(dy * xh, gmat))).astype(dx_ref.dtype)
                    dg = dg + jnp.sum(dn * xh, axis=0, keepdims=True)
                dg_ref[part:part + 1, :] += dg

    row = pl.BlockSpec((tm, D_MODEL), lambda i, j: (i, 0))
    blk = pl.BlockSpec((tm, D_MODEL), lambda i, j: (i, j))
    gspec = pl.BlockSpec((1, PAIR), lambda i, j: (0, 0))
    dx, dg = _call(
        body, name=name, grid=(s // tm, 3),
        in_specs=[row, row, row, blk, gspec, gspec],
        out_specs=[blk, pl.BlockSpec((8, PAIR), lambda i, j: (0, 0))],
        out_shape=[jax.ShapeDtypeStruct(qkv.shape, BF16), jax.ShapeDtypeStruct((8, PAIR), F32)],
        compiler_params=_params(("arbitrary", "arbitrary")),
    )(dq, dk, dv, qkv, gq, gk)
    return dx, dg[0:2]


def _relu2(a):
    r = jnp.maximum(a.astype(F32), 0.0)
    return r * r


def dense(a, w, *, nt, name, out_dtype, tm, relu2_in=False, relu2_grad=None, res=None, norm_gain=None,
          rms_back=None):
    s, k = a.shape
    n = w.shape[0] if nt else w.shape[1]
    nc = min(n, DENSE_CHUNK)

    def body(*refs):
        it = iter(refs)
        a_ref, w_ref = next(it), next(it)
        g_ref = next(it) if relu2_grad is not None else None
        r_ref = next(it) if res is not None else None
        ng_ref = next(it) if norm_gain is not None else None
        if rms_back is not None:
            x_ref, gain_ref, dres_ref = next(it), next(it), next(it)
        o_ref = next(it)
        h_ref = next(it) if norm_gain is not None else None
        dg_ref = next(it) if rms_back is not None else None
        av = a_ref[...]
        av = _relu2(av).astype(BF16) if relu2_in else av.astype(BF16)
        for c0 in range(0, n, nc):
            cols = slice(c0, c0 + nc)
            acc = _nt(av, w_ref[cols, :]) if nt else _nn(av, w_ref[:, cols])
            if g_ref is not None:
                acc = acc * (2.0 * jnp.maximum(g_ref[:, cols].astype(F32), 0.0))
            if r_ref is not None:
                acc = acc + r_ref[:, cols]
            o_ref[:, cols] = acc.astype(o_ref.dtype)
        if norm_gain is not None:
            xv = o_ref[...]
            r = lax.rsqrt(jnp.mean(xv * xv, axis=-1, keepdims=True) + RMS_EPS)
            h_ref[...] = (xv * r * ng_ref[...]).astype(h_ref.dtype)
        if rms_back is not None:
            dhv = o_ref[...]
            xv = x_ref[...]
            r = lax.rsqrt(jnp.mean(xv * xv, axis=-1, keepdims=True) + RMS_EPS)
            xh = xv * r
            dy = dhv * gain_ref[...]
            mdot = jnp.mean(dy * xh, axis=-1, keepdims=True)
            o_ref[...] = dres_ref[...] + r * (dy - xh * mdot)

            @pl.when(pl.program_id(0) == 0)
            def _():
                dg_ref[...] = jnp.zeros_like(dg_ref)

            dg_ref[0:1, :] += jnp.sum(dhv * xh, axis=0, keepdims=True)

    oblk = pl.BlockSpec((tm, n), lambda i: (i, 0))
    gblk = pl.BlockSpec((1, n), lambda i: (0, 0))
    in_specs = [pl.BlockSpec((tm, k), lambda i: (i, 0)), pl.BlockSpec(w.shape, lambda i: (0, 0))]
    args = [a, w]
    for e in (relu2_grad, res):
        if e is not None:
            in_specs.append(oblk)
            args.append(e)
    out_specs, out_shape = [oblk], [jax.ShapeDtypeStruct((s, n), out_dtype)]
    if norm_gain is not None:
        in_specs.append(gblk)
        args.append(norm_gain)
        out_specs.append(oblk)
        out_shape.append(jax.ShapeDtypeStruct((s, n), BF16))
    if rms_back is not None:
        in_specs += [oblk, gblk, oblk]
        args += list(rms_back)
        out_specs.append(pl.BlockSpec((8, n), lambda i: (0, 0)))
        out_shape.append(jax.ShapeDtypeStruct((8, n), F32))
    outs = _call(
        body, name=name, grid=(s // tm,), in_specs=in_specs, out_specs=out_specs, out_shape=out_shape,
        compiler_params=_params(("arbitrary",) if rms_back is not None else ("parallel",)),
    )(*args)
    if rms_back is not None:
        return outs[0], outs[1][0:1]
    return outs if norm_gain is not None else outs[0]


def dense_tn(a, b, *, tk, bn, name, relu2_in=False):
    s, k = a.shape
    n = b.shape[1]
    ts = TN_ROWS
    ns = s // ts
    nc = min(bn, DENSE_CHUNK)

    def body(a_ref, b_ref, o_ref, acc_ref):
        t = pl.program_id(2)

        @pl.when(t == 0)
        def _():
            acc_ref[...] = jnp.zeros_like(acc_ref)

        av = a_ref[...]
        av = _relu2(av).astype(BF16) if relu2_in else av.astype(BF16)
        at = av.T
        for c0 in range(0, bn, nc):
            cols = slice(c0, c0 + nc)
            acc_ref[:, cols] += _nn(at, b_ref[:, cols].astype(BF16))

        @pl.when(t == ns - 1)
        def _():
            o_ref[...] = acc_ref[...].astype(o_ref.dtype)

    return _call(
        body, name=name, grid=(k // tk, n // bn, ns),
        in_specs=[pl.BlockSpec((ts, tk), lambda kb, j, t: (t, kb)), pl.BlockSpec((ts, bn), lambda kb, j, t: (t, j))],
        out_specs=pl.BlockSpec((tk, bn), lambda kb, j, t: (kb, j)),
        out_shape=jax.ShapeDtypeStruct((k, n), BF16),
        scratch_shapes=[pltpu.VMEM((tk, bn), F32)],
        compiler_params=_params(("parallel", "parallel", "arbitrary")),
    )(a, b)


def _softplus(z):
    return jnp.maximum(z, 0.0) + jnp.log(1.0 + jnp.exp(-jnp.abs(z)))


def _tri_mats():
    t = SB_TILE
    r = jnp.arange(2 * t)[:, None] % t
    c = jnp.arange(t)[None, :]
    return jnp.stack([(r > c), (r < c), (r <= c)]).astype(BF16)


def _split_sum(v, u2):
    hi, lo = _split_bf16(v)
    return _nn(jnp.concatenate([hi, lo], axis=1), u2)


def sb_fwd(qkv, tri, name, rider=None):
    s = qkv.shape[0]
    t = SB_TILE
    tb = Q_SUBS * t

    def block(i, q_ref, k_ref, v_ref, u_ref, o_ref, lt_ref, c0_ref, c1_ref, acc0_ref, acc1_ref):
        lane = lax.broadcasted_iota(jnp.int32, (t, PAIR), 1)
        causal = lax.broadcasted_iota(jnp.int32, (t, t), 1) < lax.broadcasted_iota(jnp.int32, (t, t), 0)
        qv = q_ref[...]
        u_after = u_ref[0]
        qhs = [jnp.where(lane < HEAD_DIM, qv, 0) * QK_SCALE, jnp.where(lane >= HEAD_DIM, qv, 0) * QK_SCALE]
        cs = [c0_ref, c1_ref]
        accs = [acc0_ref, acc1_ref]
        for r in cs + accs:
            r[...] = jnp.zeros_like(r)

        half = t // 2
        chains = [(h, slice(r * half, (r + 1) * half)) for h in range(2) for r in range(2)]

        def sweep(tiles):
            kvs = []
            for j, _ in tiles:
                off = pl.multiple_of(j * t, t)
                kvs.append((k_ref[pl.ds(off, t), :], v_ref[pl.ds(off, t), :]))
            work = [(k, h, rows) for k in range(len(tiles)) for h, rows in chains]
            zs = [_nt(qhs[h][rows, :], kvs[k][0]) for k, h, rows in work]
            ls = [-_softplus(z) for z in zs]
            ls = [jnp.where(causal[rows, :], l, 0.0) if tiles[k][1] else l for l, (k, h, rows) in zip(ls, work)]
            afters = [_split_sum(l, u_after) for l in ls]
            run = {}
            for w, (k, h, rows) in enumerate(work):
                c = run.get((h, rows.start), cs[h][rows, :])
                a = jnp.exp(zs[w] + ls[w] + afters[w] + c)
                if tiles[k][1]:
                    a = jnp.where(causal[rows, :], a, 0.0)
                av = _nn(a.astype(BF16), kvs[k][1])
                run[h, rows.start] = c + (afters[w][:, 0:1] + ls[w][:, 0:1])
                run["acc", h, rows.start] = run.get(("acc", h, rows.start), 0.0) + av
            for h, rows in chains:
                accs[h][rows, :] += run["acc", h, rows.start]
                cs[h][rows, :] = run[h, rows.start]

        def alive():
            return jnp.max(jnp.maximum(c0_ref[...], c1_ref[...])) > -SB_DEAD

        @pl.when(i == 0)
        def _():
            sweep([(i, True)])

        @pl.when(i > 0)
        def _():
            sweep([(i, True), (i - 1, False)])

        def cond(st):
            return (st[0] < i) & st[1]

        def step(st):
            sweep([(i - 1 - st[0], False)])
            return st[0] + 1, alive()

        swept, _ = lax.while_loop(cond, step, (jnp.minimum(i, 1), alive()))
        o_ref[...] = jnp.where(lane < HEAD_DIM, acc0_ref[...], acc1_ref[...]).astype(o_ref.dtype)
        lt_ref[...] = jnp.where(lane == 0, c0_ref[...], jnp.where(lane == 1, c1_ref[...],
                                jnp.where(lane == 2, swept.astype(F32), 0.0)))

    def body(q_ref, k_ref, v_ref, u_ref, o_ref, lt_ref, *scratch):
        for sub in range(Q_SUBS):
            rows = pl.ds(sub * t, t)
            block(pl.program_id(1) * Q_SUBS + sub, q_ref.at[rows], k_ref, v_ref, u_ref, o_ref.at[rows],
                  lt_ref.at[rows], *scratch)

    def first_last():
        p, i = pl.program_id(0), pl.program_id(1)
        return (p == 0) & (i == 0), (p == N_PAIRS - 1) & (i == s // tb - 1)

    return hosted_call(
        body, first_last, rider, name=name, grid=(N_PAIRS, s // tb),
        in_specs=[
            pl.BlockSpec((tb, PAIR), lambda p, i: (i, p)),
            pl.BlockSpec((s, PAIR), lambda p, i: (0, N_PAIRS + p)),
            pl.BlockSpec((s, PAIR), lambda p, i: (0, 2 * N_PAIRS + p)),
            pl.BlockSpec((3, 2 * t, t), lambda p, i: (0, 0, 0)),
        ],
        out_specs=[pl.BlockSpec((tb, PAIR), lambda p, i: (i, p)), pl.BlockSpec((None, tb, PAIR), lambda p, i: (p, i, 0))],
        out_shape=[jax.ShapeDtypeStruct((s, D_MODEL), BF16), jax.ShapeDtypeStruct((N_PAIRS, s, PAIR), F32)],
        scratch_shapes=[pltpu.VMEM((t, 1), F32), pltpu.VMEM((t, 1), F32), pltpu.VMEM((t, PAIR), F32),
                        pltpu.VMEM((t, PAIR), F32)],
        args=(qkv, qkv, qkv, tri),
        compiler_params=_params(("arbitrary", "arbitrary")),
    )


def sb_bwd(qkv, do, ltot, tri, name, rider=None):
    s = qkv.shape[0]
    t = SB_TILE
    nq = s // t

    def block(i, q_ref, k_ref, v_ref, do_ref, lt_ref, u_ref, dq_ref, dk_ref, dv_ref,
              pl0_ref, pl1_ref, pg0_ref, pg1_ref, dqa0_ref, dqa1_ref, dka_ref, dva_ref):

        @pl.when(i == 0)
        def _():
            dka_ref[...] = jnp.zeros_like(dka_ref)
            dva_ref[...] = jnp.zeros_like(dva_ref)

        lane = lax.broadcasted_iota(jnp.int32, (t, PAIR), 1)
        causal = lax.broadcasted_iota(jnp.int32, (t, t), 1) < lax.broadcasted_iota(jnp.int32, (t, t), 0)
        qv = q_ref[...]
        dov = do_ref[...]
        ltv = lt_ref[...]
        u_before = u_ref[1]
        u_upto = u_ref[2, 0:t, :]
        hms = [lane < HEAD_DIM, lane >= HEAD_DIM]
        qhs = [jnp.where(hm, qv, 0) * QK_SCALE for hm in hms]
        dohs = [jnp.where(hm, dov, 0) for hm in hms]
        ltots = [jnp.sum(jnp.where(lane == h, ltv, 0.0), axis=1, keepdims=True) for h in range(2)]
        pls = [pl0_ref, pl1_ref]
        pgs = [pg0_ref, pg1_ref]
        dqas = [dqa0_ref, dqa1_ref]
        for r in pls + pgs + dqas:
            r[...] = jnp.zeros_like(r)

        half = t // 2
        chains = [(h, slice(r * half, (r + 1) * half)) for h in range(2) for r in range(2)]

        def sweep(tiles):
            offs = [pl.multiple_of(j * t, t) for j, _ in tiles]
            kvs = [(k_ref[pl.ds(off, t), :], v_ref[pl.ds(off, t), :]) for off in offs]
            work = [(k, h, rows) for k in range(len(tiles)) for h, rows in chains]
            qcs = [qhs[h][rows, :] for k, h, rows in work]
            docs = [dohs[h][rows, :] for k, h, rows in work]
            zs = [_nt(qcs[w], kvs[k][0]) for w, (k, h, rows) in enumerate(work)]
            das = [_nt(docs[w], kvs[k][1]) for w, (k, h, rows) in enumerate(work)]
            sps = [_softplus(z) for z in zs]
            ls = [jnp.where(causal[rows, :], -sp, 0.0) if tiles[k][1] else -sp for sp, (k, h, rows) in zip(sps, work)]
            befores = [_split_sum(l, u_before) for l in ls]
            run = {}
            dzbs, abs_ = [], []
            for w, (k, h, rows) in enumerate(work):
                key = (h, rows.start)
                pl_c = run.get(("pl",) + key, pls[h][rows, :])
                pg_c = run.get(("pg",) + key, pgs[h][rows, :])
                a = jnp.exp(zs[w] + (ltots[h][rows, :] - pl_c - befores[w]))
                if tiles[k][1]:
                    a = jnp.where(causal[rows, :], a, 0.0)
                g = a * das[w]
                upto = _nn(g.astype(BF16), u_upto)
                dz = g - jnp.exp(zs[w] - sps[w]) * (pg_c + upto)
                if tiles[k][1]:
                    dz = jnp.where(causal[rows, :], dz, 0.0)
                dzbs.append(dz.astype(BF16))
                abs_.append(a.astype(BF16))
                run[("pl",) + key] = pl_c + (befores[w][:, t - 1:t] + ls[w][:, t - 1:t])
                run[("pg",) + key] = pg_c + upto[:, t - 1:t]
                run[("dq",) + key] = run.get(("dq",) + key, 0.0) + _nn(dzbs[w], kvs[k][0])
            for k in range(len(tiles)):
                mine = [w for w, wk in enumerate(work) if wk[0] == k]
                dka_ref[pl.ds(offs[k], t), :] += sum(_tn(dzbs[w], qcs[w]) for w in mine)
                dva_ref[pl.ds(offs[k], t), :] += sum(_tn(abs_[w], docs[w]) for w in mine)
            for h, rows in chains:
                key = (h, rows.start)
                dqas[h][rows, :] += run[("dq",) + key]
                pls[h][rows, :] = run[("pl",) + key]
                pgs[h][rows, :] = run[("pg",) + key]

        def step(j, carry):
            sweep([(j, False)])
            return carry

        swept = jnp.max(jnp.where(lane == 2, ltv, 0.0)).astype(jnp.int32)
        lax.fori_loop(i - swept, i, step, 0)
        sweep([(i, True)])

        dq_ref[...] = (jnp.where(lane < HEAD_DIM, dqa0_ref[...], dqa1_ref[...]) * QK_SCALE).astype(dq_ref.dtype)

        @pl.when(i == nq - 1)
        def _():
            dk_ref[...] = dka_ref[...].astype(dk_ref.dtype)
            dv_ref[...] = dva_ref[...].astype(dv_ref.dtype)

    qblk = pl.BlockSpec((Q_SUBS * t, PAIR), lambda p, i: (i, p))
    full = pl.BlockSpec((s, PAIR), lambda p, i: (0, p))
    out = jax.ShapeDtypeStruct((s, D_MODEL), BF16)

    def body(q_ref, k_ref, v_ref, do_ref, lt_ref, u_ref, dq_ref, dk_ref, dv_ref, *scratch):
        for sub in range(Q_SUBS):
            rows = pl.ds(sub * t, t)
            block(pl.program_id(1) * Q_SUBS + sub, q_ref.at[rows], k_ref, v_ref, do_ref.at[rows], lt_ref.at[rows],
                  u_ref, dq_ref.at[rows], dk_ref, dv_ref, *scratch)

    def first_last():
        p, i = pl.program_id(0), pl.program_id(1)
        return (p == 0) & (i == 0), (p == N_PAIRS - 1) & (i == nq // Q_SUBS - 1)

    return hosted_call(
        body, first_last, rider, name=name, grid=(N_PAIRS, nq // Q_SUBS),
        in_specs=[
            qblk,
            pl.BlockSpec((s, PAIR), lambda p, i: (0, N_PAIRS + p)),
            pl.BlockSpec((s, PAIR), lambda p, i: (0, 2 * N_PAIRS + p)),
            qblk,
            pl.BlockSpec((None, Q_SUBS * t, PAIR), lambda p, i: (p, i, 0)),
            pl.BlockSpec((3, 2 * t, t), lambda p, i: (0, 0, 0)),
        ],
        out_specs=[qblk, full, full],
        out_shape=[out, out, out],
        scratch_shapes=[pltpu.VMEM((t, 1), F32)] * 4 + [pltpu.VMEM((t, PAIR), F32)] * 2
        + [pltpu.VMEM((s, PAIR), F32), pltpu.VMEM((s, PAIR), F32)],
        args=(qkv, qkv, qkv, do, ltot, tri),
        compiler_params=_params(("arbitrary", "arbitrary")),
    )


def _ring_onehot():
    r = lax.broadcasted_iota(jnp.int32, (N_REL_PAD, CA_RING), 0)
    w = lax.broadcasted_iota(jnp.int32, (N_REL_PAD, CA_RING), 1)
    idx = jnp.where(w <= CA_WIN, jnp.clip(CA_WIN - w, 0, 2 * MAX_REL), 2 * MAX_REL)
    return (r == idx).astype(F32)


def bias_table(rel_pad, name):
    def body(rb_ref, o_ref):
        ring = jnp.dot(rb_ref[...], _ring_onehot(), preferred_element_type=F32, precision=lax.Precision.HIGHEST)
        col = lax.broadcasted_iota(jnp.int32, (N_HEADS, CA_WIN), 1)

        def row(tl, carry):
            v = pltpu.roll(ring, tl, 1)[:, :CA_WIN]
            first = (tl // CHUNK) * CHUNK
            ok = (col >= first) & (col < first + (LEFT_CHUNKS + 1) * CHUNK)
            o_ref[tl] = jnp.where(ok, v, NEG_BIG)
            return carry

        lax.fori_loop(0, CA_QBLK, row, 0, unroll=8)

    return _call(
        body, name=name,
        in_specs=[pl.BlockSpec(memory_space=pltpu.VMEM)], out_specs=pl.BlockSpec(memory_space=pltpu.VMEM),
        out_shape=jax.ShapeDtypeStruct((CA_QBLK, N_HEADS, CA_WIN), F32),
        compiler_params=_params(),
    )(rel_pad)


def bias_fold(db, name):
    def body(db_ref, o_ref):
        zeros = jnp.zeros((N_HEADS, CA_RING - CA_WIN), F32)

        def row(tl, acc):
            v = jnp.concatenate([db_ref[tl], zeros], axis=1)
            return acc + pltpu.roll(v, (CA_RING - tl) % CA_RING, 1)

        ring = lax.fori_loop(0, CA_QBLK, row, jnp.zeros((N_HEADS, CA_RING), F32))
        o_ref[...] = lax.dot_general(ring, _ring_onehot(), (((1,), (1,)), ((), ())),
                                     preferred_element_type=F32, precision=lax.Precision.HIGHEST)

    return _call(
        body, name=name,
        in_specs=[pl.BlockSpec(memory_space=pltpu.VMEM)], out_specs=pl.BlockSpec(memory_space=pltpu.VMEM),
        out_shape=jax.ShapeDtypeStruct((N_HEADS, N_REL_PAD), F32),
        compiler_params=_params(),
    )(db)


def _ca_logits(qh, kw, bias_h, first_key_ok):
    sc = _nt(qh, kw) + bias_h
    col = lax.broadcasted_iota(jnp.int32, sc.shape, 1)
    return jnp.where(col >= first_key_ok, sc, NEG_BIG)


def ca_fwd(qkn, qkv, table, name, rider=None):
    s = qkv.shape[0]
    tq = CA_QBLK

    def block(i, q_ref, k_ref, v_ref, b_ref, o_ref, kp_ref, vp_ref):

        @pl.when(i == 0)
        def _():
            kp_ref[0:CA_PAD, :] = jnp.zeros((CA_PAD, PAIR), BF16)
            vp_ref[0:CA_PAD, :] = jnp.zeros((CA_PAD, PAIR), BF16)
            kp_ref[CA_PAD:, :] = k_ref[...]
            vp_ref[CA_PAD:, :] = v_ref[...]

        lane = lax.broadcasted_iota(jnp.int32, (CA_SUB, PAIR), 1)
        hms = [lane < HEAD_DIM, lane >= HEAD_DIM]
        subs = range(tq // CA_SUB)
        rows = [slice(r * CA_SUB, (r + 1) * CA_SUB) for r in subs]
        cols = [slice(r * CA_SUB, r * CA_SUB + CA_SUBWIN) for r in subs]
        offs = [pl.multiple_of(i * tq + r * CA_SUB, CA_SUB) for r in subs]
        kws = [kp_ref[pl.ds(off, CA_SUBWIN), :] for off in offs]
        vws = [vp_ref[pl.ds(off, CA_SUBWIN), :] for off in offs]
        chains = [(r, h) for r in subs for h in range(2)]
        qhs = [jnp.where(hms[h], q_ref[rows[r], :], 0) * QK_SCALE for r, h in chains]
        scs = [_ca_logits(qhs[c], kws[r], b_ref[h, rows[r], cols[r]], CA_PAD - i * tq - r * CA_SUB)
               for c, (r, h) in enumerate(chains)]
        es = [jnp.exp(sc - jnp.max(sc, axis=1, keepdims=True)) for sc in scs]
        ps = [(e / jnp.sum(e, axis=1, keepdims=True)).astype(BF16) for e in es]
        outs = [_nn(ps[c], vws[r]) for c, (r, h) in enumerate(chains)]
        for r in subs:
            o_ref[rows[r], :] = jnp.where(lane < HEAD_DIM, outs[2 * r], outs[2 * r + 1]).astype(o_ref.dtype)

    def body(q_ref, k_ref, v_ref, b_ref, o_ref, *scratch):
        for sub in range(Q_SUBS):
            rows = pl.ds(sub * tq, tq)
            block(pl.program_id(1) * Q_SUBS + sub, q_ref.at[rows], k_ref, v_ref, b_ref, o_ref.at[rows], *scratch)

    def first_last():
        p, i = pl.program_id(0), pl.program_id(1)
        return (p == 0) & (i == 0), (p == N_PAIRS - 1) & (i == s // (Q_SUBS * tq) - 1)

    return hosted_call(
        body, first_last, rider, name=name, grid=(N_PAIRS, s // (Q_SUBS * tq)),
        in_specs=[
            pl.BlockSpec((Q_SUBS * tq, PAIR), lambda p, i: (i, p)),
            pl.BlockSpec((s, PAIR), lambda p, i: (0, N_PAIRS + p)),
            pl.BlockSpec((s, PAIR), lambda p, i: (0, 2 * N_PAIRS + p)),
            pl.BlockSpec((2, tq, CA_WIN), lambda p, i: (p, 0, 0)),
        ],
        out_specs=[pl.BlockSpec((Q_SUBS * tq, PAIR), lambda p, i: (i, p))],
        out_shape=[jax.ShapeDtypeStruct((s, D_MODEL), BF16)],
        scratch_shapes=[pltpu.VMEM((s + CA_PAD, PAIR), BF16), pltpu.VMEM((s + CA_PAD, PAIR), BF16)],
        args=(qkn, qkn, qkv, table),
        compiler_params=_params(("arbitrary", "arbitrary")),
    )


def ca_bwd(qkn, qkv, o, do, table, name, rider=None):
    s = qkv.shape[0]
    tq = CA_QBLK
    nq = s // tq

    def block(i, q_ref, k_ref, v_ref, o_ref, do_ref, b_ref, dq_ref, dk_ref, dv_ref, db_ref,
              kp_ref, vp_ref, dka_ref, dva_ref):

        @pl.when(i == 0)
        def _():
            kp_ref[0:CA_PAD, :] = jnp.zeros((CA_PAD, PAIR), BF16)
            vp_ref[0:CA_PAD, :] = jnp.zeros((CA_PAD, PAIR), BF16)
            kp_ref[CA_PAD:, :] = k_ref[...]
            vp_ref[CA_PAD:, :] = v_ref[...]
            dka_ref[...] = jnp.zeros_like(dka_ref)
            dva_ref[...] = jnp.zeros_like(dva_ref)
            db_ref[...] = jnp.zeros_like(db_ref)

        lane = lax.broadcasted_iota(jnp.int32, (CA_SUB, PAIR), 1)
        hms = [lane < HEAD_DIM, lane >= HEAD_DIM]
        subs = range(tq // CA_SUB)
        rows = [slice(r * CA_SUB, (r + 1) * CA_SUB) for r in subs]
        cols = [slice(r * CA_SUB, r * CA_SUB + CA_SUBWIN) for r in subs]
        offs = [pl.multiple_of(i * tq + r * CA_SUB, CA_SUB) for r in subs]
        kws = [kp_ref[pl.ds(off, CA_SUBWIN), :] for off in offs]
        vws = [vp_ref[pl.ds(off, CA_SUBWIN), :] for off in offs]
        dovs = [do_ref[rows[r], :] for r in subs]
        prods = [dovs[r].astype(F32) * o_ref[rows[r], :].astype(F32) for r in subs]
        chains = [(r, h) for r in subs for h in range(2)]
        qhs = [jnp.where(hms[h], q_ref[rows[r], :], 0) * QK_SCALE for r, h in chains]
        dohs = [jnp.where(hms[h], dovs[r], 0) for r, h in chains]
        deltas = [jnp.sum(jnp.where(hms[h], prods[r], 0.0), axis=1, keepdims=True) for r, h in chains]
        scs = [_ca_logits(qhs[c], kws[r], b_ref[h, rows[r], cols[r]], CA_PAD - i * tq - r * CA_SUB)
               for c, (r, h) in enumerate(chains)]
        dps = [_nt(dohs[c], vws[r]) for c, (r, h) in enumerate(chains)]
        es = [jnp.exp(sc - jnp.max(sc, axis=1, keepdims=True)) for sc in scs]
        ps = [e / jnp.sum(e, axis=1, keepdims=True) for e in es]
        dss = [ps[c] * (dps[c] - deltas[c]) for c in range(len(chains))]
        for c, (r, h) in enumerate(chains):
            db_ref[h, rows[r], cols[r]] += dss[c]
        dsbs = [ds.astype(BF16) for ds in dss]
        pbs = [p.astype(BF16) for p in ps]
        dqs = [_nn(dsbs[c], kws[r]) for c, (r, h) in enumerate(chains)]
        for r in subs:
            dq_ref[rows[r], :] = (jnp.where(lane < HEAD_DIM, dqs[2 * r], dqs[2 * r + 1]) * QK_SCALE).astype(dq_ref.dtype)
        for r in subs:
            dka_ref[pl.ds(offs[r], CA_SUBWIN), :] += _tn(dsbs[2 * r], qhs[2 * r]) + _tn(dsbs[2 * r + 1], qhs[2 * r + 1])
            dva_ref[pl.ds(offs[r], CA_SUBWIN), :] += _tn(pbs[2 * r], dohs[2 * r]) + _tn(pbs[2 * r + 1], dohs[2 * r + 1])

        @pl.when(i == nq - 1)
        def _():
            dk_ref[...] = dka_ref[CA_PAD:, :].astype(dk_ref.dtype)
            dv_ref[...] = dva_ref[CA_PAD:, :].astype(dv_ref.dtype)

    qblk = pl.BlockSpec((Q_SUBS * tq, PAIR), lambda p, i: (i, p))
    full = pl.BlockSpec((s, PAIR), lambda p, i: (0, p))
    tblk = pl.BlockSpec((2, tq, CA_WIN), lambda p, i: (p, 0, 0))
    out = jax.ShapeDtypeStruct((s, D_MODEL), BF16)

    def body(q_ref, k_ref, v_ref, o_ref, do_ref, b_ref, dq_ref, dk_ref, dv_ref, db_ref, *scratch):
        for sub in range(Q_SUBS):
            rows = pl.ds(sub * tq, tq)
            block(pl.program_id(1) * Q_SUBS + sub, q_ref.at[rows], k_ref, v_ref, o_ref.at[rows], do_ref.at[rows], b_ref,
                  dq_ref.at[rows], dk_ref, dv_ref, db_ref, *scratch)

    def first_last():
        p, i = pl.program_id(0), pl.program_id(1)
        return (p == 0) & (i == 0), (p == N_PAIRS - 1) & (i == nq // Q_SUBS - 1)

    return hosted_call(
        body, first_last, rider, name=name, grid=(N_PAIRS, nq // Q_SUBS),
        in_specs=[
            qblk,
            pl.BlockSpec((s, PAIR), lambda p, i: (0, N_PAIRS + p)),
            pl.BlockSpec((s, PAIR), lambda p, i: (0, 2 * N_PAIRS + p)),
            qblk, qblk, tblk,
        ],
        out_specs=[qblk, full, full, tblk],
        out_shape=[out, out, out, jax.ShapeDtypeStruct((N_HEADS, tq, CA_WIN), F32)],
        scratch_shapes=[pltpu.VMEM((s + CA_PAD, PAIR), BF16), pltpu.VMEM((s + CA_PAD, PAIR), BF16),
                        pltpu.VMEM((s + CA_PAD, PAIR), F32), pltpu.VMEM((s + CA_PAD, PAIR), F32)],
        args=(qkn, qkn, qkv, o, do, table),
        compiler_params=_params(("arbitrary", "arbitrary")),
    )


def _me():
    x, y, c = lax.axis_index("x"), lax.axis_index("y"), lax.axis_index("c")
    return x, y, c, 4 * x + 2 * y + c


def _peer(k):
    x, y, c, _ = _me()
    px, py, pc = x ^ ((k >> 2) & 1), y ^ ((k >> 1) & 1), c ^ (k & 1)
    return (px, py, pc), 4 * px + 2 * py + pc


ANY = pl.BlockSpec(memory_space=pl.ANY)


class Exchange:
    def __init__(self, n, copy, in_arrays, out_shape):
        self.n, self.copy, self.in_arrays, self.out_shape = n, copy, list(in_arrays), list(out_shape)
        self.sems = [pltpu.SemaphoreType.DMA((n, N_DEV - 1)), pltpu.SemaphoreType.DMA((n, N_DEV - 1)),
                     pltpu.SemaphoreType.DMA((n,))]

    def _copies(self, data, sems):
        send_sems, recv_sems, local_sems = sems
        _, _, _, me = _me()
        local, sends, recvs = [], [], []
        for a in range(self.n):
            s_ref, d_ref = self.copy(data, a, me, me)
            local.append(pltpu.make_async_copy(s_ref, d_ref, local_sems.at[a]))
            for k in range(1, N_DEV):
                peer, pidx = _peer(k)
                s_ref, d_ref = self.copy(data, a, me, pidx)
                sends.append(pltpu.make_async_remote_copy(
                    src_ref=s_ref, dst_ref=d_ref, send_sem=send_sems.at[a, k - 1], recv_sem=recv_sems.at[a, k - 1],
                    device_id=peer, device_id_type=MESH))
                s_ref, d_ref = self.copy(data, a, pidx, me)
                recvs.append(pltpu.make_async_remote_copy(
                    src_ref=s_ref, dst_ref=d_ref, send_sem=send_sems.at[a, k - 1], recv_sem=recv_sems.at[a, k - 1],
                    device_id=peer, device_id_type=MESH))
        return local, sends, recvs

    def start(self, data, sems):
        local, sends, _ = self._copies(data, sems)
        for cp in local + sends:
            cp.start()

    def finish(self, data, sems):
        local, sends, recvs = self._copies(data, sems)
        for cp in recvs:
            cp.wait_recv()
        for cp in sends:
            cp.wait_send()
        for cp in local:
            cp.wait()


def run_exchange(ex, name):
    n_data = len(ex.in_arrays) + len(ex.out_shape)

    def body(*refs):
        ex.start(refs[:n_data], refs[n_data:])
        ex.finish(refs[:n_data], refs[n_data:])

    return _call(
        body, name=name,
        in_specs=[ANY] * len(ex.in_arrays), out_specs=[ANY] * len(ex.out_shape), out_shape=ex.out_shape,
        scratch_shapes=ex.sems, compiler_params=pltpu.CompilerParams(has_side_effects=True),
    )(*ex.in_arrays)


def hosted_call(core_body, first_last, rider, *, in_specs, out_specs, out_shape, scratch_shapes, args, **kw):
    if rider is None:
        return _call(core_body, in_specs=in_specs, out_specs=out_specs, out_shape=out_shape,
                     scratch_shapes=scratch_shapes, **kw)(*args), []
    n_in, n_out, n_scr = len(in_specs), len(out_specs), len(scratch_shapes)
    r_in, r_out = len(rider.in_arrays), len(rider.out_shape)

    def body(*refs):
        it = iter(refs)
        take = lambda m: [next(it) for _ in range(m)]
        c_in, x_in, c_out, x_out, c_scr, sems = take(n_in), take(r_in), take(n_out), take(r_out), take(n_scr), take(3)
        first, last = first_last()

        @pl.when(first)
        def _():
            rider.start(x_in + x_out, sems)

        core_body(*c_in, *c_out, *c_scr)

        @pl.when(last)
        def _():
            rider.finish(x_in + x_out, sems)

    outs = _call(
        body, in_specs=list(in_specs) + [ANY] * r_in, out_specs=list(out_specs) + [ANY] * r_out,
        out_shape=list(out_shape) + rider.out_shape, scratch_shapes=list(scratch_shapes) + rider.sems, **kw,
    )(*args, *rider.in_arrays)
    return outs[:n_out], outs[n_out:]


def _window(full_ref, shard_shape, idx):
    a, b = shard_shape
    if a == full_ref.shape[0]:
        return full_ref.at[:, pl.ds(pl.multiple_of(idx * b, 128), b)]
    return full_ref.at[pl.ds(pl.multiple_of(idx * a, 8), a), :]


def gather_exchange(shards, full_shapes, keys):
    nw = len(shards)

    def copy(data, a, src_idx, dst_idx):
        w, layer = keys[a]
        s_ref = data[w].at[layer]
        return s_ref, _window(data[nw + a], s_ref.shape, src_idx)

    out_shape = [jax.ShapeDtypeStruct(full_shapes[w], shards[w].dtype) for w, _ in keys]
    return Exchange(len(keys), copy, shards, out_shape)


def scatter_exchange(partials, shard_shapes):
    n = len(partials)

    def copy(data, a, src_idx, dst_idx):
        return _window(data[a], shard_shapes[a], dst_idx), data[n + a].at[src_idx]

    out_shape = [jax.ShapeDtypeStruct((N_DEV,) + tuple(ss), p.dtype) for ss, p in zip(shard_shapes, partials)]
    return Exchange(n, copy, partials, out_shape)


def _adamw(w, g, m, v):
    m = ADAM_B1 * m + (1.0 - ADAM_B1) * g
    v = ADAM_B2 * v + (1.0 - ADAM_B2) * (g * g)
    m_hat = m / (1.0 - ADAM_B1 ** ADAM_STEP)
    v_hat = v / (1.0 - ADAM_B2 ** ADAM_STEP)
    delta = -ADAM_LR * (m_hat / (jnp.sqrt(v_hat) + ADAM_EPS) + ADAM_WD * w)
    return delta, m, v


def adam_shard(parts, w, m, v, name):
    _, r, c = w.shape
    tr = min(r, 256)
    nr = r // tr

    def body(*refs):
        p_refs = refs[:DEPTH]
        w_ref, m_ref, v_ref, g_out, d_out, m_out, v_out = refs[DEPTH:]
        layer = pl.program_id(0)
        for li in range(DEPTH):
            @pl.when(layer == li)
            def _(p_ref=p_refs[li]):
                g = p_ref[0].astype(F32)
                for e in range(1, N_DEV):
                    g = g + p_ref[e].astype(F32)
                delta, mn, vn = _adamw(w_ref[...], g, m_ref[...], v_ref[...])
                g_out[...] = g
                d_out[...] = delta
                m_out[...] = mn
                v_out[...] = vn

    def part_spec(li):
        return pl.BlockSpec((N_DEV, tr, c), lambda layer, i: (0, jnp.where(layer == li, i, 0), 0))

    blk = pl.BlockSpec((None, tr, c), lambda layer, i: (layer, i, 0))
    out = jax.ShapeDtypeStruct(w.shape, F32)
    return _call(
        body, name=name, grid=(DEPTH, nr),
        in_specs=[part_spec(li) for li in range(DEPTH)] + [blk, blk, blk],
        out_specs=[blk] * 4, out_shape=[out] * 4,
        compiler_params=_params(("arbitrary", "arbitrary")),
    )(*parts, w, m, v)


def small_allreduce_adam(pk, w, m, v, name):
    def body(pk_ref, w_ref, m_ref, v_ref, g_out, d_out, m_out, v_out, all_ref, send_sems, recv_sems):
        _, _, _, me = _me()
        all_ref[me] = pk_ref[...]
        sends = []
        for k in range(1, N_DEV):
            peer, _ = _peer(k)
            cp = pltpu.make_async_remote_copy(
                src_ref=pk_ref, dst_ref=all_ref.at[me], send_sem=send_sems.at[k - 1],
                recv_sem=recv_sems.at[k - 1], device_id=peer, device_id_type=MESH)
            cp.start()
            sends.append(cp)
        for k in range(1, N_DEV):
            peer, pidx = _peer(k)
            pltpu.make_async_remote_copy(
                src_ref=pk_ref, dst_ref=all_ref.at[pidx], send_sem=send_sems.at[k - 1],
                recv_sem=recv_sems.at[k - 1], device_id=peer, device_id_type=MESH).wait_recv()
        for cp in sends:
            cp.wait_send()
        g = all_ref[0]
        for e in range(1, N_DEV):
            g = g + all_ref[e]
        delta, mn, vn = _adamw(w_ref[...], g, m_ref[...], v_ref[...])
        g_out[...] = g
        d_out[...] = delta
        m_out[...] = mn
        v_out[...] = vn

    vm = pl.BlockSpec(memory_space=pltpu.VMEM)
    out = jax.ShapeDtypeStruct((PACK_ROWS, 128), F32)
    return _call(
        body, name=name,
        in_specs=[vm] * 4, out_specs=[vm] * 4, out_shape=[out] * 4,
        scratch_shapes=[pltpu.VMEM((N_DEV, PACK_ROWS, 128), F32), pltpu.SemaphoreType.DMA((N_DEV - 1,)),
                        pltpu.SemaphoreType.DMA((N_DEV - 1,))],
        compiler_params=pltpu.CompilerParams(has_side_effects=True),
    )(pk, w, m, v)


def _pack_small(mix, ffn, qn, kn, rel, loss):
    flat = jnp.concatenate([mix.reshape(-1), ffn.reshape(-1), qn.reshape(-1), kn.reshape(-1), rel.reshape(-1),
                            loss.reshape(-1)])
    return jnp.pad(flat, (0, PACK_LEN - flat.shape[0])).reshape(PACK_ROWS, 128)


def _unpack_small(pk):
    flat = pk.reshape(-1)
    return (flat[OFF_MIX:OFF_FFN].reshape(DEPTH, D_MODEL), flat[OFF_FFN:OFF_QN].reshape(DEPTH, D_MODEL),
            flat[OFF_QN:OFF_KN].reshape(2, HEAD_DIM), flat[OFF_KN:OFF_REL].reshape(2, HEAD_DIM),
            flat[OFF_REL:OFF_LOSS].reshape(2, N_HEADS, N_REL), flat[OFF_LOSS])


def _pair_gain(g):
    return jnp.concatenate([g, g]).reshape(1, PAIR)


def kernel(x, mix_norm, w_qkv, w_o, q_norm, k_norm, rel_bias, ffn_norm, w_up, w_down, loss_target, m_mix_norm, m_w_qkv, m_w_o, m_q_norm, m_k_norm, m_rel_bias, m_ffn_norm, m_w_up, m_w_down, v_mix_norm, v_w_qkv, v_w_o, v_q_norm, v_k_norm, v_rel_bias, v_ffn_norm, v_w_up, v_w_down):
    x0 = x[0]
    target = loss_target[0]
    shard_shapes = [w_qkv.shape[1:], w_o.shape[1:], w_up.shape[1:], w_down.shape[1:]]
    full_shapes = [(D_MODEL, 3 * D_MODEL), (D_MODEL, D_MODEL), (D_MODEL, D_FF), (D_FF, D_MODEL)]
    shards = [w_qkv.astype(BF16), w_o.astype(BF16), w_up.astype(BF16), w_down.astype(BF16)]
    nw = len(shards)
    weight = {}

    def fetch(keys, host, *args):
        outs, arrived = host(*args, rider=gather_exchange(shards, full_shapes, keys))
        weight.update(zip(keys, arrived))
        return outs

    first = [(0, 0)]
    weight.update(zip(first, run_exchange(gather_exchange(shards, full_shapes, first), "gather_first")))
    riding = {0: [(w, 0) for w in range(1, nw)] + [(w, 1) for w in range(nw)],
              1: [(w, 2) for w in range(nw)], 2: [(w, 3) for w in range(nw)]}
    tri = _tri_mats()

    saved = []
    xin = x0
    h0 = rms_fwd(x0, mix_norm[0:1], "rms_mix_0")
    for layer in range(DEPTH):
        mixer_b = layer % 2 == 1
        idx = layer // 2
        qkv_raw = dense(h0, weight[0, layer], nt=False, name=f"qkv_{layer}", out_dtype=BF16, tm=1024)
        if mixer_b:
            gq, gk = _pair_gain(q_norm[idx]), _pair_gain(k_norm[idx])
            qkn = headnorm_fwd(qkv_raw, gq, gk, f"headnorm_{layer}")
            rel_pad = jnp.pad(rel_bias[idx], ((0, 0), (0, N_REL_PAD - N_REL)))
            table = jnp.transpose(bias_table(rel_pad, f"bias_table_{layer}"), (1, 0, 2))
            if layer in riding:
                (o,) = fetch(riding[layer], ca_fwd, qkn, qkv_raw, table, f"ca_fwd_{layer}")
            else:
                (o,), _ = ca_fwd(qkn, qkv_raw, table, f"ca_fwd_{layer}")
            attn_saved = (qkv_raw, qkn, table)
        else:
            o, ltot = fetch(riding[layer], sb_fwd, qkv_raw, tri, f"sb_fwd_{layer}")
            attn_saved = (qkv_raw, ltot)
        x1, h1 = dense(o, weight[1, layer], nt=False, name=f"attn_out_{layer}", out_dtype=F32, tm=1024, res=xin,
                       norm_gain=ffn_norm[layer:layer + 1])
        pre = dense(h1, weight[2, layer], nt=False, name=f"up_{layer}", out_dtype=BF16, tm=1024)
        saved.append((xin, h0, attn_saved, o, x1, h1, pre))
        if layer + 1 < DEPTH:
            xin, h0 = dense(pre, weight[3, layer], nt=False, name=f"down_{layer}", out_dtype=F32, tm=512, relu2_in=True,
                            res=x1, norm_gain=mix_norm[layer + 1:layer + 2])
        else:
            xin = dense(pre, weight[3, layer], nt=False, name=f"down_{layer}", out_dtype=F32, tm=512, relu2_in=True,
                        res=x1)

    dx, loss_blk = loss_head(xin, target, "loss_head")

    partial = {}
    received = {}
    d_mix, d_ffn = [None] * DEPTH, [None] * DEPTH
    d_qn, d_kn, d_rel = [None] * 2, [None] * 2, [None] * 2

    def leaving():
        keys = [key for key in sorted(partial) if key not in received]
        return keys, scatter_exchange([partial[key] for key in keys], [shard_shapes[key[0]] for key in keys])

    def send(host, *args):
        keys, ex = leaving()
        outs, arrived = host(*args, rider=ex)
        received.update(zip(keys, arrived))
        return outs

    for layer in reversed(range(DEPTH)):
        mixer_b = layer % 2 == 1
        idx = layer // 2
        xin, h0, attn_saved, o, x1, h1, pre = saved[layer]
        g_qkv, g_o, g_up, g_down = (weight[w, layer] for w in range(nw))
        dpre = dense(dx, g_down, nt=True, name=f"d_pre_{layer}", out_dtype=BF16, tm=512, relu2_grad=pre)
        partial[3, layer] = dense_tn(pre, dx, tk=2048, bn=1024, name=f"dw_down_{layer}", relu2_in=True)
        partial[2, layer] = dense_tn(h1, dpre, tk=1024, bn=2048, name=f"dw_up_{layer}")
        dx, d_ffn[layer] = dense(dpre, g_up, nt=True, name=f"d_h1_{layer}", out_dtype=F32, tm=512,
                                 rms_back=(x1, ffn_norm[layer:layer + 1], dx))
        do = dense(dx, g_o, nt=True, name=f"d_o_{layer}", out_dtype=BF16, tm=1024)
        partial[1, layer] = dense_tn(o, dx, tk=1024, bn=1024, name=f"dw_o_{layer}")
        if mixer_b:
            qkv_raw, qkn, table = attn_saved
            dq, dk, dv, dtab = send(ca_bwd, qkn, qkv_raw, o, do, table, f"ca_bwd_{layer}")
            gq, gk = _pair_gain(q_norm[idx]), _pair_gain(k_norm[idx])
            dqkv, dgain = headnorm_bwd(dq, dk, dv, qkv_raw, gq, gk, f"headnorm_bwd_{layer}")
            d_qn[idx] = dgain[0, :HEAD_DIM] + dgain[0, HEAD_DIM:]
            d_kn[idx] = dgain[1, :HEAD_DIM] + dgain[1, HEAD_DIM:]
            d_rel[idx] = bias_fold(jnp.transpose(dtab, (1, 0, 2)), f"bias_fold_{layer}")[:, :N_REL]
        else:
            qkv_raw, ltot = attn_saved
            dq, dk, dv = send(sb_bwd, qkv_raw, do, ltot, tri, f"sb_bwd_{layer}")
            dqkv = jnp.concatenate([dq, dk, dv], axis=1)
        partial[0, layer] = dense_tn(h0, dqkv, tk=1024, bn=1536, name=f"dw_qkv_{layer}")
        dx, d_mix[layer] = dense(dqkv, g_qkv, nt=True, name=f"d_h0_{layer}", out_dtype=F32, tm=512,
                                 rms_back=(xin, mix_norm[layer:layer + 1], dx))

    grad_x = dx[None]

    keys, ex = leaving()
    received.update(zip(keys, run_exchange(ex, "scatter_rest")))

    def big(wi, w, m, v, name):
        return adam_shard([received[wi, layer] for layer in range(DEPTH)], w, m, v, name)

    a_qkv = big(0, w_qkv, m_w_qkv, v_w_qkv, "adam_qkv")
    a_o = big(1, w_o, m_w_o, v_w_o, "adam_o")
    a_up = big(2, w_up, m_w_up, v_w_up, "adam_up")
    a_down = big(3, w_down, m_w_down, v_w_down, "adam_down")

    pk = _pack_small(jnp.concatenate(d_mix), jnp.concatenate(d_ffn), jnp.stack(d_qn), jnp.stack(d_kn),
                     jnp.stack(d_rel), loss_blk[0, 0])
    zero = jnp.zeros((), F32)
    pw = _pack_small(mix_norm, ffn_norm, q_norm, k_norm, rel_bias, zero)
    pm = _pack_small(m_mix_norm, m_ffn_norm, m_q_norm, m_k_norm, m_rel_bias, zero)
    pv = _pack_small(v_mix_norm, v_ffn_norm, v_q_norm, v_k_norm, v_rel_bias, zero)
    s_g, s_d, s_m, s_v = small_allreduce_adam(pk, pw, pm, pv, "small_allreduce_adam")
    g_mix, g_ffn, g_qn, g_kn, g_rel, loss = _unpack_small(s_g)
    dl_mix, dl_ffn, dl_qn, dl_kn, dl_rel, _ = _unpack_small(s_d)
    nm_mix, nm_ffn, nm_qn, nm_kn, nm_rel, _ = _unpack_small(s_m)
    nv_mix, nv_ffn, nv_qn, nv_kn, nv_rel, _ = _unpack_small(s_v)

    return (loss, grad_x,
            g_mix, a_qkv[0], a_o[0], g_qn, g_kn, g_rel, g_ffn, a_up[0], a_down[0],
            dl_mix, a_qkv[1], a_o[1], dl_qn, dl_kn, dl_rel, dl_ffn, a_up[1], a_down[1],
            nm_mix, a_qkv[2], a_o[2], nm_qn, nm_kn, nm_rel, nm_ffn, a_up[2], a_down[2],
            nv_mix, a_qkv[3], a_o[3], nv_qn, nv_kn, nv_rel, nv_ffn, a_up[3], a_down[3])
```

```python
import jax
import jax.numpy as jnp
from jax import lax
from jax.experimental import pallas as pl
from jax.experimental.pallas import tpu as pltpu

F32 = jnp.float32
BF16 = jnp.bfloat16
MESH = pl.DeviceIdType.MESH

D_MODEL = 1024
N_HEADS = 16
HEAD_DIM = 64
PAIR = 2 * HEAD_DIM
N_PAIRS = N_HEADS // 2
D_FF = 4 * D_MODEL
DEPTH = 4
N_DEV = 8
RMS_EPS = 1e-6
QK_SCALE = HEAD_DIM ** -0.5

SB_TILE = 256
SB_DEAD = 104.0
Q_SUBS = 4

CHUNK = 64
LEFT_CHUNKS = 8
CA_QBLK = 256
CA_PAD = LEFT_CHUNKS * CHUNK
CA_WIN = CA_QBLK + CA_PAD
CA_SUB = 128
CA_SUBWIN = CA_SUB + CA_PAD
CA_RING = 1024
MAX_REL = 256
N_REL = 2 * MAX_REL + 1
N_REL_PAD = 640
NEG_BIG = -1e30

ADAM_LR = 0.001
ADAM_B1 = 0.9
ADAM_B2 = 0.999
ADAM_EPS = 1e-08
ADAM_WD = 0.01
ADAM_STEP = 10

ROW_TILE = 512
DENSE_CHUNK = 512
TN_ROWS = 1024
VMEM_LIMIT = 56 * 1024 * 1024

OFF_MIX = 0
OFF_FFN = OFF_MIX + DEPTH * D_MODEL
OFF_QN = OFF_FFN + DEPTH * D_MODEL
OFF_KN = OFF_QN + 2 * HEAD_DIM
OFF_REL = OFF_KN + 2 * HEAD_DIM
OFF_LOSS = OFF_REL + 2 * N_HEADS * N_REL
PACK_ROWS = 200
PACK_LEN = PACK_ROWS * 128


def _call(body, **kw):
    return pl.pallas_call(body, **kw)


def _params(sem=None, vmem=VMEM_LIMIT):
    if sem is None:
        return pltpu.CompilerParams(vmem_limit_bytes=vmem)
    return pltpu.CompilerParams(dimension_semantics=sem, vmem_limit_bytes=vmem)


def _nt(a, b):
    return lax.dot_general(a, b, (((1,), (1,)), ((), ())), preferred_element_type=F32)


def _tn(a, b):
    return lax.dot_general(a, b, (((0,), (0,)), ((), ())), preferred_element_type=F32)


def _nn(a, b):
    return jnp.dot(a, b, preferred_element_type=F32)


def _split_bf16(v):
    hi = v.astype(BF16)
    lo = (v - hi.astype(F32)).astype(BF16)
    return hi, lo


def rms_fwd(x, g, name):
    s = x.shape[0]
    tm = ROW_TILE

    def body(x_ref, g_ref, o_ref):
        xv = x_ref[...]
        r = lax.rsqrt(jnp.mean(xv * xv, axis=-1, keepdims=True) + RMS_EPS)
        o_ref[...] = (xv * r * g_ref[...]).astype(o_ref.dtype)

    return _call(
        body, name=name, grid=(s // tm,),
        in_specs=[pl.BlockSpec((tm, D_MODEL), lambda i: (i, 0)), pl.BlockSpec((1, D_MODEL), lambda i: (0, 0))],
        out_specs=pl.BlockSpec((tm, D_MODEL), lambda i: (i, 0)),
        out_shape=jax.ShapeDtypeStruct((s, D_MODEL), BF16),
        compiler_params=_params(("parallel",)),
    )(x, g)


def loss_head(y, target, name):
    s = y.shape[0]
    tm = ROW_TILE

    def body(y_ref, t_ref, dy_ref, l_ref):
        i = pl.program_id(0)
        e = y_ref[...] - t_ref[...]
        dy_ref[...] = e * (1.0 / D_MODEL)

        @pl.when(i == 0)
        def _():
            l_ref[...] = jnp.zeros_like(l_ref)

        per_row = jnp.sum(e * e, axis=-1, keepdims=True) * (1.0 / D_MODEL)
        l_ref[...] += jnp.broadcast_to(0.5 * jnp.sum(per_row, axis=0, keepdims=True), l_ref.shape)

    row = pl.BlockSpec((tm, D_MODEL), lambda i: (i, 0))
    return _call(
        body, name=name, grid=(s // tm,),
        in_specs=[row, row],
        out_specs=[row, pl.BlockSpec((8, 128), lambda i: (0, 0))],
        out_shape=[jax.ShapeDtypeStruct((s, D_MODEL), F32), jax.ShapeDtypeStruct((8, 128), F32)],
        compiler_params=_params(("arbitrary",)),
    )(y, target)


def _group_sum_matrix():
    r = lax.broadcasted_iota(jnp.int32, (PAIR, PAIR), 0) // HEAD_DIM
    c = lax.broadcasted_iota(jnp.int32, (PAIR, PAIR), 1) // HEAD_DIM
    return (r == c).astype(BF16)


def _head_mean(v, gmat):
    hi, lo = _split_bf16(v)
    return (_nn(hi, gmat) + _nn(lo, gmat)) * (1.0 / HEAD_DIM)


def headnorm_fwd(qkv, gq, gk, name):
    s = qkv.shape[0]
    tm = ROW_TILE

    def body(x_ref, gq_ref, gk_ref, o_ref):
        j = pl.program_id(1)
        gmat = _group_sum_matrix()
        gain = jnp.where(j == 0, gq_ref[...], gk_ref[...])
        for b in range(N_PAIRS):
            cols = slice(b * PAIR, (b + 1) * PAIR)
            xv = x_ref[:, cols].astype(F32)
            r = lax.rsqrt(_head_mean(xv * xv, gmat) + RMS_EPS)
            o_ref[:, cols] = (xv * r * gain).astype(o_ref.dtype)

    blk = pl.BlockSpec((tm, D_MODEL), lambda i, j: (i, j))
    gspec = pl.BlockSpec((1, PAIR), lambda i, j: (0, 0))
    return _call(
        body, name=name, grid=(s // tm, 2),
        in_specs=[blk, gspec, gspec], out_specs=blk,
        out_shape=jax.ShapeDtypeStruct((s, 2 * D_MODEL), BF16),
        compiler_params=_params(("parallel", "parallel")),
    )(qkv, gq, gk)


def headnorm_bwd(dq, dk, dv, qkv, gq, gk, name):
    s = qkv.shape[0]
    tm = ROW_TILE

    def body(dq_ref, dk_ref, dv_ref, x_ref, gq_ref, gk_ref, dx_ref, dg_ref):
        i = pl.program_id(0)
        j = pl.program_id(1)

        @pl.when((i == 0) & (j == 0))
        def _():
            dg_ref[...] = jnp.zeros_like(dg_ref)

        @pl.when(j == 2)
        def _():
            dx_ref[...] = dv_ref[...]

        for part, (d_ref, g_ref) in enumerate(((dq_ref, gq_ref), (dk_ref, gk_ref))):
            @pl.when(j == part)
            def _(part=part, d_ref=d_ref, g_ref=g_ref):
                gmat = _group_sum_matrix()
                gain = g_ref[...]
                dg = jnp.zeros((1, PAIR), F32)
                for b in range(N_PAIRS):
                    cols = slice(b * PAIR, (b + 1) * PAIR)
                    xv = x_ref[:, cols].astype(F32)
                    dn = d_ref[:, cols].astype(F32)
                    r = lax.rsqrt(_head_mean(xv * xv, gmat) + RMS_EPS)
                    xh = xv * r
                    dy = dn * gain
                    dx_ref[:, cols] = (r * (dy - xh * _head_mean(dy * xh, gmat))).astype(dx_ref.dtype)
                    dg = dg + jnp.sum(dn * xh, axis=0, keepdims=True)
                dg_ref[part:part + 1, :] += dg

    row = pl.BlockSpec((tm, D_MODEL), lambda i, j: (i, 0))
    blk = pl.BlockSpec((tm, D_MODEL), lambda i, j: (i, j))
    gspec = pl.BlockSpec((1, PAIR), lambda i, j: (0, 0))
    dx, dg = _call(
        body, name=name, grid=(s // tm, 3),
        in_specs=[row, row, row, blk, gspec, gspec],
        out_specs=[blk, pl.BlockSpec((8, PAIR), lambda i, j: (0, 0))],
        out_shape=[jax.ShapeDtypeStruct(qkv.shape, BF16), jax.ShapeDtypeStruct((8, PAIR), F32)],
        compiler_params=_params(("arbitrary", "arbitrary")),
    )(dq, dk, dv, qkv, gq, gk)
    return dx, dg[0:2]


def _relu2(a):
    r = jnp.maximum(a.astype(F32), 0.0)
    return r * r


def dense(a, w, *, nt, name, out_dtype, tm, relu2_in=False, relu2_grad=None, res=None, norm_gain=None,
          rms_back=None, rider=None):
    s, k = a.shape
    n = w.shape[0] if nt else w.shape[1]
    nc = min(n, DENSE_CHUNK)

    def body(*refs):
        it = iter(refs)
        a_ref, w_ref = next(it), next(it)
        g_ref = next(it) if relu2_grad is not None else None
        r_ref = next(it) if res is not None else None
        ng_ref = next(it) if norm_gain is not None else None
        if rms_back is not None:
            x_ref, gain_ref, dres_ref = next(it), next(it), next(it)
        o_ref = next(it)
        h_ref = next(it) if norm_gain is not None else None
        dg_ref = next(it) if rms_back is not None else None
        av = a_ref[...]
        av = _relu2(av).astype(BF16) if relu2_in else av.astype(BF16)
        for c0 in range(0, n, nc):
            cols = slice(c0, c0 + nc)
            acc = _nt(av, w_ref[cols, :]) if nt else _nn(av, w_ref[:, cols])
            if g_ref is not None:
                acc = acc * (2.0 * jnp.maximum(g_ref[:, cols].astype(F32), 0.0))
            if r_ref is not None:
                acc = acc + r_ref[:, cols]
            o_ref[:, cols] = acc.astype(o_ref.dtype)
        if norm_gain is not None:
            xv = o_ref[...]
            r = lax.rsqrt(jnp.mean(xv * xv, axis=-1, keepdims=True) + RMS_EPS)
            h_ref[...] = (xv * r * ng_ref[...]).astype(h_ref.dtype)
        if rms_back is not None:
            dhv = o_ref[...]
            xv = x_ref[...]
            r = lax.rsqrt(jnp.mean(xv * xv, axis=-1, keepdims=True) + RMS_EPS)
            xh = xv * r
            dy = dhv * gain_ref[...]
            mdot = jnp.mean(dy * xh, axis=-1, keepdims=True)
            o_ref[...] = dres_ref[...] + r * (dy - xh * mdot)

            @pl.when(pl.program_id(0) == 0)
            def _():
                dg_ref[...] = jnp.zeros_like(dg_ref)

            dg_ref[0:1, :] += jnp.sum(dhv * xh, axis=0, keepdims=True)

    oblk = pl.BlockSpec((tm, n), lambda i: (i, 0))
    gblk = pl.BlockSpec((1, n), lambda i: (0, 0))
    in_specs = [pl.BlockSpec((tm, k), lambda i: (i, 0)), pl.BlockSpec(w.shape, lambda i: (0, 0))]
    args = [a, w]
    for e in (relu2_grad, res):
        if e is not None:
            in_specs.append(oblk)
            args.append(e)
    out_specs, out_shape = [oblk], [jax.ShapeDtypeStruct((s, n), out_dtype)]
    if norm_gain is not None:
        in_specs.append(gblk)
        args.append(norm_gain)
        out_specs.append(oblk)
        out_shape.append(jax.ShapeDtypeStruct((s, n), BF16))
    if rms_back is not None:
        in_specs += [oblk, gblk, oblk]
        args += list(rms_back)
        out_specs.append(pl.BlockSpec((8, n), lambda i: (0, 0)))
        out_shape.append(jax.ShapeDtypeStruct((8, n), F32))
    def first_last():
        return pl.program_id(0) == 0, pl.program_id(0) == s // tm - 1

    outs, arrived = hosted_call(
        body, first_last, rider, name=name, grid=(s // tm,), in_specs=in_specs, out_specs=out_specs,
        out_shape=out_shape, scratch_shapes=[], args=args,
        compiler_params=_params(("arbitrary",) if rms_back is not None or rider is not None else ("parallel",)),
    )
    if rider is not None:
        return outs, arrived
    if rms_back is not None:
        return outs[0], outs[1][0:1]
    return outs if norm_gain is not None else outs[0]


def dense_tn(a, b, *, tk, bn, name, relu2_in=False):
    s, k = a.shape
    n = b.shape[1]
    ts = TN_ROWS
    ns = s // ts
    nc = min(bn, DENSE_CHUNK)

    def body(a_ref, b_ref, o_ref, acc_ref):
        t = pl.program_id(2)

        @pl.when(t == 0)
        def _():
            acc_ref[...] = jnp.zeros_like(acc_ref)

        av = a_ref[...]
        av = _relu2(av).astype(BF16) if relu2_in else av.astype(BF16)
        at = av.T
        for c0 in range(0, bn, nc):
            cols = slice(c0, c0 + nc)
            acc_ref[:, cols] += _nn(at, b_ref[:, cols].astype(BF16))

        @pl.when(t == ns - 1)
        def _():
            o_ref[...] = acc_ref[...].astype(o_ref.dtype)

    return _call(
        body, name=name, grid=(k // tk, n // bn, ns),
        in_specs=[pl.BlockSpec((ts, tk), lambda kb, j, t: (t, kb)), pl.BlockSpec((ts, bn), lambda kb, j, t: (t, j))],
        out_specs=pl.BlockSpec((tk, bn), lambda kb, j, t: (kb, j)),
        out_shape=jax.ShapeDtypeStruct((k, n), BF16),
        scratch_shapes=[pltpu.VMEM((tk, bn), F32)],
        compiler_params=_params(("parallel", "parallel", "arbitrary")),
    )(a, b)


def _softplus(z):
    return jnp.maximum(z, 0.0) + jnp.log(1.0 + jnp.exp(-jnp.abs(z)))


def _tri_mats():
    t = SB_TILE
    r = jnp.arange(2 * t)[:, None] % t
    c = jnp.arange(t)[None, :]
    return jnp.stack([(r > c), (r < c), (r <= c)]).astype(BF16)


def _split_sum(v, u2):
    hi, lo = _split_bf16(v)
    return _nn(jnp.concatenate([hi, lo], axis=1), u2)


def sb_fwd(qkv, tri, name, rider=None):
    s = qkv.shape[0]
    t = SB_TILE
    tb = Q_SUBS * t

    def block(i, q_ref, k_ref, v_ref, u_ref, o_ref, lt_ref, c0_ref, c1_ref, acc0_ref, acc1_ref):
        lane = lax.broadcasted_iota(jnp.int32, (t, PAIR), 1)
        causal = lax.broadcasted_iota(jnp.int32, (t, t), 1) < lax.broadcasted_iota(jnp.int32, (t, t), 0)
        qv = q_ref[...]
        u_after = u_ref[0]
        qhs = [jnp.where(lane < HEAD_DIM, qv, 0) * QK_SCALE, jnp.where(lane >= HEAD_DIM, qv, 0) * QK_SCALE]
        cs = [c0_ref, c1_ref]
        accs = [acc0_ref, acc1_ref]
        for r in cs + accs:
            r[...] = jnp.zeros_like(r)

        half = t // 2
        chains = [(h, slice(r * half, (r + 1) * half)) for h in range(2) for r in range(2)]

        def sweep(tiles):
            kvs = []
            for j, _ in tiles:
                off = pl.multiple_of(j * t, t)
                kvs.append((k_ref[pl.ds(off, t), :], v_ref[pl.ds(off, t), :]))
            work = [(k, h, rows) for k in range(len(tiles)) for h, rows in chains]
            zs = [_nt(qhs[h][rows, :], kvs[k][0]) for k, h, rows in work]
            ls = [-_softplus(z) for z in zs]
            ls = [jnp.where(causal[rows, :], l, 0.0) if tiles[k][1] else l for l, (k, h, rows) in zip(ls, work)]
            afters = [_split_sum(l, u_after) for l in ls]
            run = {}
            for w, (k, h, rows) in enumerate(work):
                c = run.get((h, rows.start), cs[h][rows, :])
                a = jnp.exp(zs[w] + ls[w] + afters[w] + c)
                if tiles[k][1]:
                    a = jnp.where(causal[rows, :], a, 0.0)
                av = _nn(a.astype(BF16), kvs[k][1])
                run[h, rows.start] = c + (afters[w][:, 0:1] + ls[w][:, 0:1])
                run["acc", h, rows.start] = run.get(("acc", h, rows.start), 0.0) + av
            for h, rows in chains:
                accs[h][rows, :] += run["acc", h, rows.start]
                cs[h][rows, :] = run[h, rows.start]

        def alive():
            return jnp.max(jnp.maximum(c0_ref[...], c1_ref[...])) > -SB_DEAD

        @pl.when(i == 0)
        def _():
            sweep([(i, True)])

        @pl.when(i > 0)
        def _():
            sweep([(i, True), (i - 1, False)])

        def cond(st):
            return (st[0] < i) & st[1]

        def step(st):
            sweep([(i - 1 - st[0], False)])
            return st[0] + 1, alive()

        swept, _ = lax.while_loop(cond, step, (jnp.minimum(i, 1), alive()))
        o_ref[...] = jnp.where(lane < HEAD_DIM, acc0_ref[...], acc1_ref[...]).astype(o_ref.dtype)
        lt_ref[...] = jnp.where(lane == 0, c0_ref[...], jnp.where(lane == 1, c1_ref[...],
                                jnp.where(lane == 2, swept.astype(F32), 0.0)))

    def body(q_ref, k_ref, v_ref, u_ref, o_ref, lt_ref, *scratch):
        for sub in range(Q_SUBS):
            rows = pl.ds(sub * t, t)
            block(pl.program_id(1) * Q_SUBS + sub, q_ref.at[rows], k_ref, v_ref, u_ref, o_ref.at[rows],
                  lt_ref.at[rows], *scratch)

    def first_last():
        p, i = pl.program_id(0), pl.program_id(1)
        return (p == 0) & (i == 0), (p == N_PAIRS - 1) & (i == s // tb - 1)

    return hosted_call(
        body, first_last, rider, name=name, grid=(N_PAIRS, s // tb),
        in_specs=[
            pl.BlockSpec((tb, PAIR), lambda p, i: (i, p)),
            pl.BlockSpec((s, PAIR), lambda p, i: (0, N_PAIRS + p)),
            pl.BlockSpec((s, PAIR), lambda p, i: (0, 2 * N_PAIRS + p)),
            pl.BlockSpec((3, 2 * t, t), lambda p, i: (0, 0, 0)),
        ],
        out_specs=[pl.BlockSpec((tb, PAIR), lambda p, i: (i, p)), pl.BlockSpec((None, tb, PAIR), lambda p, i: (p, i, 0))],
        out_shape=[jax.ShapeDtypeStruct((s, D_MODEL), BF16), jax.ShapeDtypeStruct((N_PAIRS, s, PAIR), F32)],
        scratch_shapes=[pltpu.VMEM((t, 1), F32), pltpu.VMEM((t, 1), F32), pltpu.VMEM((t, PAIR), F32),
                        pltpu.VMEM((t, PAIR), F32)],
        args=(qkv, qkv, qkv, tri),
        compiler_params=_params(("arbitrary", "arbitrary")),
    )


def sb_bwd(qkv, do, ltot, tri, name, rider=None):
    s = qkv.shape[0]
    t = SB_TILE
    nq = s // t

    def block(i, q_ref, k_ref, v_ref, do_ref, lt_ref, u_ref, dq_ref, dk_ref, dv_ref,
              pl0_ref, pl1_ref, pg0_ref, pg1_ref, dqa0_ref, dqa1_ref, dka_ref, dva_ref):

        @pl.when(i == 0)
        def _():
            dka_ref[...] = jnp.zeros_like(dka_ref)
            dva_ref[...] = jnp.zeros_like(dva_ref)

        lane = lax.broadcasted_iota(jnp.int32, (t, PAIR), 1)
        causal = lax.broadcasted_iota(jnp.int32, (t, t), 1) < lax.broadcasted_iota(jnp.int32, (t, t), 0)
        qv = q_ref[...]
        dov = do_ref[...]
        ltv = lt_ref[...]
        u_before = u_ref[1]
        u_upto = u_ref[2, 0:t, :]
        hms = [lane < HEAD_DIM, lane >= HEAD_DIM]
        qhs = [jnp.where(hm, qv, 0) * QK_SCALE for hm in hms]
        dohs = [jnp.where(hm, dov, 0) for hm in hms]
        ltots = [ltv[:, h:h + 1] for h in range(2)]
        pls = [pl0_ref, pl1_ref]
        pgs = [pg0_ref, pg1_ref]
        dqas = [dqa0_ref, dqa1_ref]
        for r in pls + pgs + dqas:
            r[...] = jnp.zeros_like(r)

        half = t // 2
        chains = [(h, slice(r * half, (r + 1) * half)) for h in range(2) for r in range(2)]

        def sweep(tiles):
            offs = [pl.multiple_of(j * t, t) for j, _ in tiles]
            kvs = [(k_ref[pl.ds(off, t), :], v_ref[pl.ds(off, t), :]) for off in offs]
            work = [(k, h, rows) for k in range(len(tiles)) for h, rows in chains]
            qcs = [qhs[h][rows, :] for k, h, rows in work]
            docs = [dohs[h][rows, :] for k, h, rows in work]
            zs = [_nt(qcs[w], kvs[k][0]) for w, (k, h, rows) in enumerate(work)]
            das = [_nt(docs[w], kvs[k][1]) for w, (k, h, rows) in enumerate(work)]
            sps = [_softplus(z) for z in zs]
            ls = [jnp.where(causal[rows, :], -sp, 0.0) if tiles[k][1] else -sp for sp, (k, h, rows) in zip(sps, work)]
            befores = [_split_sum(l, u_before) for l in ls]
            run = {}
            dzbs, abs_ = [], []
            for w, (k, h, rows) in enumerate(work):
                key = (h, rows.start)
                pl_c = run.get(("pl",) + key, pls[h][rows, :])
                pg_c = run.get(("pg",) + key, pgs[h][rows, :])
                a = jnp.exp(zs[w] + (ltots[h][rows, :] - pl_c - befores[w]))
                if tiles[k][1]:
                    a = jnp.where(causal[rows, :], a, 0.0)
                g = a * das[w]
                upto = _nn(g.astype(BF16), u_upto)
                dz = g - jnp.exp(zs[w] - sps[w]) * (pg_c + upto)
                if tiles[k][1]:
                    dz = jnp.where(causal[rows, :], dz, 0.0)
                dzbs.append(dz.astype(BF16))
                abs_.append(a.astype(BF16))
                run[("pl",) + key] = pl_c + (befores[w][:, t - 1:t] + ls[w][:, t - 1:t])
                run[("pg",) + key] = pg_c + upto[:, t - 1:t]
                run[("dq",) + key] = run.get(("dq",) + key, 0.0) + _nn(dzbs[w], kvs[k][0])
            for k in range(len(tiles)):
                mine = [w for w, wk in enumerate(work) if wk[0] == k]
                dka_ref[pl.ds(offs[k], t), :] += sum(_tn(dzbs[w], qcs[w]) for w in mine)
                dva_ref[pl.ds(offs[k], t), :] += sum(_tn(abs_[w], docs[w]) for w in mine)
            for h, rows in chains:
                key = (h, rows.start)
                dqas[h][rows, :] += run[("dq",) + key]
                pls[h][rows, :] = run[("pl",) + key]
                pgs[h][rows, :] = run[("pg",) + key]

        def step(j, carry):
            sweep([(j, False)])
            return carry

        swept = jnp.max(ltv[0:8, 2:3]).astype(jnp.int32)
        lax.fori_loop(i - swept, i, step, 0)
        sweep([(i, True)])

        dq_ref[...] = (jnp.where(lane < HEAD_DIM, dqa0_ref[...], dqa1_ref[...]) * QK_SCALE).astype(dq_ref.dtype)

        @pl.when(i == nq - 1)
        def _():
            dk_ref[...] = dka_ref[...].astype(dk_ref.dtype)
            dv_ref[...] = dva_ref[...].astype(dv_ref.dtype)

    qblk = pl.BlockSpec((Q_SUBS * t, PAIR), lambda p, i: (i, p))
    full = pl.BlockSpec((s, PAIR), lambda p, i: (0, p))
    out = jax.ShapeDtypeStruct((s, D_MODEL), BF16)

    def body(q_ref, k_ref, v_ref, do_ref, lt_ref, u_ref, dq_ref, dk_ref, dv_ref, *scratch):
        for sub in range(Q_SUBS):
            rows = pl.ds(sub * t, t)
            block(pl.program_id(1) * Q_SUBS + sub, q_ref.at[rows], k_ref, v_ref, do_ref.at[rows], lt_ref.at[rows],
                  u_ref, dq_ref.at[rows], dk_ref, dv_ref, *scratch)

    def first_last():
        p, i = pl.program_id(0), pl.program_id(1)
        return (p == 0) & (i == 0), (p == N_PAIRS - 1) & (i == nq // Q_SUBS - 1)

    return hosted_call(
        body, first_last, rider, name=name, grid=(N_PAIRS, nq // Q_SUBS),
        in_specs=[
            qblk,
            pl.BlockSpec((s, PAIR), lambda p, i: (0, N_PAIRS + p)),
            pl.BlockSpec((s, PAIR), lambda p, i: (0, 2 * N_PAIRS + p)),
            qblk,
            pl.BlockSpec((None, Q_SUBS * t, PAIR), lambda p, i: (p, i, 0)),
            pl.BlockSpec((3, 2 * t, t), lambda p, i: (0, 0, 0)),
        ],
        out_specs=[qblk, full, full],
        out_shape=[out, out, out],
        scratch_shapes=[pltpu.VMEM((t, 1), F32)] * 4 + [pltpu.VMEM((t, PAIR), F32)] * 2
        + [pltpu.VMEM((s, PAIR), F32), pltpu.VMEM((s, PAIR), F32)],
        args=(qkv, qkv, qkv, do, ltot, tri),
        compiler_params=_params(("arbitrary", "arbitrary")),
    )


def _ring_onehot():
    r = lax.broadcasted_iota(jnp.int32, (N_REL_PAD, CA_RING), 0)
    w = lax.broadcasted_iota(jnp.int32, (N_REL_PAD, CA_RING), 1)
    idx = jnp.where(w <= CA_WIN, jnp.clip(CA_WIN - w, 0, 2 * MAX_REL), 2 * MAX_REL)
    return (r == idx).astype(F32)


def bias_table(rel_pad, name):
    def body(rb_ref, o_ref):
        ring = jnp.dot(rb_ref[...], _ring_onehot(), preferred_element_type=F32, precision=lax.Precision.HIGHEST)
        col = lax.broadcasted_iota(jnp.int32, (N_HEADS, CA_WIN), 1)

        def row(tl, carry):
            v = pltpu.roll(ring, tl, 1)[:, :CA_WIN]
            first = (tl // CHUNK) * CHUNK
            ok = (col >= first) & (col < first + (LEFT_CHUNKS + 1) * CHUNK)
            o_ref[tl] = jnp.where(ok, v, NEG_BIG)
            return carry

        lax.fori_loop(0, CA_QBLK, row, 0, unroll=8)

    return _call(
        body, name=name,
        in_specs=[pl.BlockSpec(memory_space=pltpu.VMEM)], out_specs=pl.BlockSpec(memory_space=pltpu.VMEM),
        out_shape=jax.ShapeDtypeStruct((CA_QBLK, N_HEADS, CA_WIN), F32),
        compiler_params=_params(),
    )(rel_pad)


def bias_fold(db, name):
    def body(db_ref, o_ref):
        zeros = jnp.zeros((N_HEADS, CA_RING - CA_WIN), F32)

        def row(tl, acc):
            v = jnp.concatenate([db_ref[tl], zeros], axis=1)
            return acc + pltpu.roll(v, (CA_RING - tl) % CA_RING, 1)

        ring = lax.fori_loop(0, CA_QBLK, row, jnp.zeros((N_HEADS, CA_RING), F32))
        o_ref[...] = lax.dot_general(ring, _ring_onehot(), (((1,), (1,)), ((), ())),
                                     preferred_element_type=F32, precision=lax.Precision.HIGHEST)

    return _call(
        body, name=name,
        in_specs=[pl.BlockSpec(memory_space=pltpu.VMEM)], out_specs=pl.BlockSpec(memory_space=pltpu.VMEM),
        out_shape=jax.ShapeDtypeStruct((N_HEADS, N_REL_PAD), F32),
        compiler_params=_params(),
    )(db)


def _ca_logits(qh, kw, bias_h, first_key_ok):
    sc = _nt(qh, kw) + bias_h
    col = lax.broadcasted_iota(jnp.int32, sc.shape, 1)
    return jnp.where(col >= first_key_ok, sc, NEG_BIG)


def ca_fwd(qkn, qkv, table, name, rider=None):
    s = qkv.shape[0]
    tq = CA_QBLK

    def block(i, q_ref, k_ref, v_ref, b_ref, o_ref, kp_ref, vp_ref):

        @pl.when(i == 0)
        def _():
            kp_ref[0:CA_PAD, :] = jnp.zeros((CA_PAD, PAIR), BF16)
            vp_ref[0:CA_PAD, :] = jnp.zeros((CA_PAD, PAIR), BF16)
            kp_ref[CA_PAD:, :] = k_ref[...]
            vp_ref[CA_PAD:, :] = v_ref[...]

        lane = lax.broadcasted_iota(jnp.int32, (CA_SUB, PAIR), 1)
        hms = [lane < HEAD_DIM, lane >= HEAD_DIM]
        subs = range(tq // CA_SUB)
        rows = [slice(r * CA_SUB, (r + 1) * CA_SUB) for r in subs]
        cols = [slice(r * CA_SUB, r * CA_SUB + CA_SUBWIN) for r in subs]
        offs = [pl.multiple_of(i * tq + r * CA_SUB, CA_SUB) for r in subs]
        kws = [kp_ref[pl.ds(off, CA_SUBWIN), :] for off in offs]
        vws = [vp_ref[pl.ds(off, CA_SUBWIN), :] for off in offs]
        chains = [(r, h) for r in subs for h in range(2)]
        qhs = [jnp.where(hms[h], q_ref[rows[r], :], 0) * QK_SCALE for r, h in chains]
        scs = [_ca_logits(qhs[c], kws[r], b_ref[h, rows[r], cols[r]], CA_PAD - i * tq - r * CA_SUB)
               for c, (r, h) in enumerate(chains)]
        es = [jnp.exp(sc - jnp.max(sc, axis=1, keepdims=True)) for sc in scs]
        ps = [(e / jnp.sum(e, axis=1, keepdims=True)).astype(BF16) for e in es]
        outs = [_nn(ps[c], vws[r]) for c, (r, h) in enumerate(chains)]
        for r in subs:
            o_ref[rows[r], :] = jnp.where(lane < HEAD_DIM, outs[2 * r], outs[2 * r + 1]).astype(o_ref.dtype)

    def body(q_ref, k_ref, v_ref, b_ref, o_ref, *scratch):
        for sub in range(Q_SUBS):
            rows = pl.ds(sub * tq, tq)
            block(pl.program_id(1) * Q_SUBS + sub, q_ref.at[rows], k_ref, v_ref, b_ref, o_ref.at[rows], *scratch)

    def first_last():
        p, i = pl.program_id(0), pl.program_id(1)
        return (p == 0) & (i == 0), (p == N_PAIRS - 1) & (i == s // (Q_SUBS * tq) - 1)

    return hosted_call(
        body, first_last, rider, name=name, grid=(N_PAIRS, s // (Q_SUBS * tq)),
        in_specs=[
            pl.BlockSpec((Q_SUBS * tq, PAIR), lambda p, i: (i, p)),
            pl.BlockSpec((s, PAIR), lambda p, i: (0, N_PAIRS + p)),
            pl.BlockSpec((s, PAIR), lambda p, i: (0, 2 * N_PAIRS + p)),
            pl.BlockSpec((2, tq, CA_WIN), lambda p, i: (p, 0, 0)),
        ],
        out_specs=[pl.BlockSpec((Q_SUBS * tq, PAIR), lambda p, i: (i, p))],
        out_shape=[jax.ShapeDtypeStruct((s, D_MODEL), BF16)],
        scratch_shapes=[pltpu.VMEM((s + CA_PAD, PAIR), BF16), pltpu.VMEM((s + CA_PAD, PAIR), BF16)],
        args=(qkn, qkn, qkv, table),
        compiler_params=_params(("arbitrary", "arbitrary")),
    )


def ca_bwd(qkn, qkv, o, do, table, name, rider=None):
    s = qkv.shape[0]
    tq = CA_QBLK
    nq = s // tq

    def block(i, q_ref, k_ref, v_ref, o_ref, do_ref, b_ref, dq_ref, dk_ref, dv_ref, db_ref,
              kp_ref, vp_ref, dka_ref, dva_ref):

        @pl.when(i == 0)
        def _():
            kp_ref[0:CA_PAD, :] = jnp.zeros((CA_PAD, PAIR), BF16)
            vp_ref[0:CA_PAD, :] = jnp.zeros((CA_PAD, PAIR), BF16)
            kp_ref[CA_PAD:, :] = k_ref[...]
            vp_ref[CA_PAD:, :] = v_ref[...]
            dka_ref[...] = jnp.zeros_like(dka_ref)
            dva_ref[...] = jnp.zeros_like(dva_ref)
            db_ref[...] = jnp.zeros_like(db_ref)

        lane = lax.broadcasted_iota(jnp.int32, (CA_SUB, PAIR), 1)
        hms = [lane < HEAD_DIM, lane >= HEAD_DIM]
        subs = range(tq // CA_SUB)
        rows = [slice(r * CA_SUB, (r + 1) * CA_SUB) for r in subs]
        cols = [slice(r * CA_SUB, r * CA_SUB + CA_SUBWIN) for r in subs]
        offs = [pl.multiple_of(i * tq + r * CA_SUB, CA_SUB) for r in subs]
        kws = [kp_ref[pl.ds(off, CA_SUBWIN), :] for off in offs]
        vws = [vp_ref[pl.ds(off, CA_SUBWIN), :] for off in offs]
        dovs = [do_ref[rows[r], :] for r in subs]
        prods = [dovs[r].astype(F32) * o_ref[rows[r], :].astype(F32) for r in subs]
        chains = [(r, h) for r in subs for h in range(2)]
        qhs = [jnp.where(hms[h], q_ref[rows[r], :], 0) * QK_SCALE for r, h in chains]
        dohs = [jnp.where(hms[h], dovs[r], 0) for r, h in chains]
        deltas = [jnp.sum(jnp.where(hms[h], prods[r], 0.0), axis=1, keepdims=True) for r, h in chains]
        scs = [_ca_logits(qhs[c], kws[r], b_ref[h, rows[r], cols[r]], CA_PAD - i * tq - r * CA_SUB)
               for c, (r, h) in enumerate(chains)]
        dps = [_nt(dohs[c], vws[r]) for c, (r, h) in enumerate(chains)]
        es = [jnp.exp(sc - jnp.max(sc, axis=1, keepdims=True)) for sc in scs]
        ps = [e / jnp.sum(e, axis=1, keepdims=True) for e in es]
        dss = [ps[c] * (dps[c] - deltas[c]) for c in range(len(chains))]
        for c, (r, h) in enumerate(chains):
            db_ref[h, rows[r], cols[r]] += dss[c]
        dsbs = [ds.astype(BF16) for ds in dss]
        pbs = [p.astype(BF16) for p in ps]
        dqs = [_nn(dsbs[c], kws[r]) for c, (r, h) in enumerate(chains)]
        for r in subs:
            dq_ref[rows[r], :] = (jnp.where(lane < HEAD_DIM, dqs[2 * r], dqs[2 * r + 1]) * QK_SCALE).astype(dq_ref.dtype)
        for r in subs:
            dka_ref[pl.ds(offs[r], CA_SUBWIN), :] += _tn(dsbs[2 * r], qhs[2 * r]) + _tn(dsbs[2 * r + 1], qhs[2 * r + 1])
            dva_ref[pl.ds(offs[r], CA_SUBWIN), :] += _tn(pbs[2 * r], dohs[2 * r]) + _tn(pbs[2 * r + 1], dohs[2 * r + 1])

        @pl.when(i == nq - 1)
        def _():
            dk_ref[...] = dka_ref[CA_PAD:, :].astype(dk_ref.dtype)
            dv_ref[...] = dva_ref[CA_PAD:, :].astype(dv_ref.dtype)

    qblk = pl.BlockSpec((Q_SUBS * tq, PAIR), lambda p, i: (i, p))
    full = pl.BlockSpec((s, PAIR), lambda p, i: (0, p))
    tblk = pl.BlockSpec((2, tq, CA_WIN), lambda p, i: (p, 0, 0))
    out = jax.ShapeDtypeStruct((s, D_MODEL), BF16)

    def body(q_ref, k_ref, v_ref, o_ref, do_ref, b_ref, dq_ref, dk_ref, dv_ref, db_ref, *scratch):
        for sub in range(Q_SUBS):
            rows = pl.ds(sub * tq, tq)
            block(pl.program_id(1) * Q_SUBS + sub, q_ref.at[rows], k_ref, v_ref, o_ref.at[rows], do_ref.at[rows], b_ref,
                  dq_ref.at[rows], dk_ref, dv_ref, db_ref, *scratch)

    def first_last():
        p, i = pl.program_id(0), pl.program_id(1)
        return (p == 0) & (i == 0), (p == N_PAIRS - 1) & (i == nq // Q_SUBS - 1)

    return hosted_call(
        body, first_last, rider, name=name, grid=(N_PAIRS, nq // Q_SUBS),
        in_specs=[
            qblk,
            pl.BlockSpec((s, PAIR), lambda p, i: (0, N_PAIRS + p)),
            pl.BlockSpec((s, PAIR), lambda p, i: (0, 2 * N_PAIRS + p)),
            qblk, qblk, tblk,
        ],
        out_specs=[qblk, full, full, tblk],
        out_shape=[out, out, out, jax.ShapeDtypeStruct((N_HEADS, tq, CA_WIN), F32)],
        scratch_shapes=[pltpu.VMEM((s + CA_PAD, PAIR), BF16), pltpu.VMEM((s + CA_PAD, PAIR), BF16),
                        pltpu.VMEM((s + CA_PAD, PAIR), F32), pltpu.VMEM((s + CA_PAD, PAIR), F32)],
        args=(qkn, qkn, qkv, o, do, table),
        compiler_params=_params(("arbitrary", "arbitrary")),
    )


def _me():
    x, y, c = lax.axis_index("x"), lax.axis_index("y"), lax.axis_index("c")
    return x, y, c, 4 * x + 2 * y + c


def _peer(k):
    x, y, c, _ = _me()
    px, py, pc = x ^ ((k >> 2) & 1), y ^ ((k >> 1) & 1), c ^ (k & 1)
    return (px, py, pc), 4 * px + 2 * py + pc


ANY = pl.BlockSpec(memory_space=pl.ANY)


class Exchange:
    def __init__(self, n, copy, in_arrays, out_shape):
        self.n, self.copy, self.in_arrays, self.out_shape = n, copy, list(in_arrays), list(out_shape)
        self.sems = [pltpu.SemaphoreType.DMA((n, N_DEV - 1)), pltpu.SemaphoreType.DMA((n, N_DEV - 1)),
                     pltpu.SemaphoreType.DMA((n,))]

    def _copies(self, data, sems):
        send_sems, recv_sems, local_sems = sems
        _, _, _, me = _me()
        local, sends, recvs = [], [], []
        for a in range(self.n):
            s_ref, d_ref = self.copy(data, a, me, me)
            local.append(pltpu.make_async_copy(s_ref, d_ref, local_sems.at[a]))
            for k in range(1, N_DEV):
                peer, pidx = _peer(k)
                s_ref, d_ref = self.copy(data, a, me, pidx)
                sends.append(pltpu.make_async_remote_copy(
                    src_ref=s_ref, dst_ref=d_ref, send_sem=send_sems.at[a, k - 1], recv_sem=recv_sems.at[a, k - 1],
                    device_id=peer, device_id_type=MESH))
                s_ref, d_ref = self.copy(data, a, pidx, me)
                recvs.append(pltpu.make_async_remote_copy(
                    src_ref=s_ref, dst_ref=d_ref, send_sem=send_sems.at[a, k - 1], recv_sem=recv_sems.at[a, k - 1],
                    device_id=peer, device_id_type=MESH))
        return local, sends, recvs

    def start(self, data, sems):
        local, sends, _ = self._copies(data, sems)
        for cp in local + sends:
            cp.start()

    def finish(self, data, sems):
        local, sends, recvs = self._copies(data, sems)
        for cp in recvs:
            cp.wait_recv()
        for cp in sends:
            cp.wait_send()
        for cp in local:
            cp.wait()


def run_exchange(ex, name):
    n_data = len(ex.in_arrays) + len(ex.out_shape)

    def body(*refs):
        ex.start(refs[:n_data], refs[n_data:])
        ex.finish(refs[:n_data], refs[n_data:])

    return _call(
        body, name=name,
        in_specs=[ANY] * len(ex.in_arrays), out_specs=[ANY] * len(ex.out_shape), out_shape=ex.out_shape,
        scratch_shapes=ex.sems, compiler_params=pltpu.CompilerParams(has_side_effects=True),
    )(*ex.in_arrays)


def hosted_call(core_body, first_last, rider, *, in_specs, out_specs, out_shape, scratch_shapes, args, **kw):
    if rider is None:
        return _call(core_body, in_specs=in_specs, out_specs=out_specs, out_shape=out_shape,
                     scratch_shapes=scratch_shapes, **kw)(*args), []
    n_in, n_out, n_scr = len(in_specs), len(out_specs), len(scratch_shapes)
    r_in, r_out = len(rider.in_arrays), len(rider.out_shape)

    def body(*refs):
        it = iter(refs)
        take = lambda m: [next(it) for _ in range(m)]
        c_in, x_in, c_out, x_out, c_scr, sems = take(n_in), take(r_in), take(n_out), take(r_out), take(n_scr), take(3)
        first, last = first_last()

        @pl.when(first)
        def _():
            rider.start(x_in + x_out, sems)

        core_body(*c_in, *c_out, *c_scr)

        @pl.when(last)
        def _():
            rider.finish(x_in + x_out, sems)

    outs = _call(
        body, in_specs=list(in_specs) + [ANY] * r_in, out_specs=list(out_specs) + [ANY] * r_out,
        out_shape=list(out_shape) + rider.out_shape, scratch_shapes=list(scratch_shapes) + rider.sems, **kw,
    )(*args, *rider.in_arrays)
    return outs[:n_out], outs[n_out:]


def _window(full_ref, shard_shape, idx):
    a, b = shard_shape
    if a == full_ref.shape[0]:
        return full_ref.at[:, pl.ds(pl.multiple_of(idx * b, 128), b)]
    return full_ref.at[pl.ds(pl.multiple_of(idx * a, 8), a), :]


def gather_exchange(shards, full_shapes, keys):
    nw = len(shards)

    def copy(data, a, src_idx, dst_idx):
        w, layer = keys[a]
        s_ref = data[w].at[layer]
        return s_ref, _window(data[nw + a], s_ref.shape, src_idx)

    out_shape = [jax.ShapeDtypeStruct(full_shapes[w], shards[w].dtype) for w, _ in keys]
    return Exchange(len(keys), copy, shards, out_shape)


def scatter_exchange(partials, shard_shapes):
    n = len(partials)

    def copy(data, a, src_idx, dst_idx):
        return _window(data[a], shard_shapes[a], dst_idx), data[n + a].at[src_idx]

    out_shape = [jax.ShapeDtypeStruct((N_DEV,) + tuple(ss), p.dtype) for ss, p in zip(shard_shapes, partials)]
    return Exchange(n, copy, partials, out_shape)


def _adamw(w, g, m, v):
    m = ADAM_B1 * m + (1.0 - ADAM_B1) * g
    v = ADAM_B2 * v + (1.0 - ADAM_B2) * (g * g)
    m_hat = m / (1.0 - ADAM_B1 ** ADAM_STEP)
    v_hat = v / (1.0 - ADAM_B2 ** ADAM_STEP)
    delta = -ADAM_LR * (m_hat / (jnp.sqrt(v_hat) + ADAM_EPS) + ADAM_WD * w)
    return delta, m, v


def adam_shard(parts, w, m, v, name):
    _, r, c = w.shape
    tr = min(r, 256)
    nr = r // tr

    def body(*refs):
        p_refs = refs[:DEPTH]
        w_ref, m_ref, v_ref, g_out, d_out, m_out, v_out = refs[DEPTH:]
        layer = pl.program_id(0)
        for li in range(DEPTH):
            @pl.when(layer == li)
            def _(p_ref=p_refs[li]):
                g = p_ref[0].astype(F32)
                for e in range(1, N_DEV):
                    g = g + p_ref[e].astype(F32)
                delta, mn, vn = _adamw(w_ref[...], g, m_ref[...], v_ref[...])
                g_out[...] = g
                d_out[...] = delta
                m_out[...] = mn
                v_out[...] = vn

    def part_spec(li):
        return pl.BlockSpec((N_DEV, tr, c), lambda layer, i: (0, jnp.where(layer == li, i, 0), 0))

    blk = pl.BlockSpec((None, tr, c), lambda layer, i: (layer, i, 0))
    out = jax.ShapeDtypeStruct(w.shape, F32)
    return _call(
        body, name=name, grid=(DEPTH, nr),
        in_specs=[part_spec(li) for li in range(DEPTH)] + [blk, blk, blk],
        out_specs=[blk] * 4, out_shape=[out] * 4,
        compiler_params=_params(("arbitrary", "arbitrary")),
    )(*parts, w, m, v)


def small_allreduce_adam(pk, w, m, v, name):
    def body(pk_ref, w_ref, m_ref, v_ref, g_out, d_out, m_out, v_out, all_ref, send_sems, recv_sems):
        _, _, _, me = _me()
        all_ref[me] = pk_ref[...]
        sends = []
        for k in range(1, N_DEV):
            peer, _ = _peer(k)
            cp = pltpu.make_async_remote_copy(
                src_ref=pk_ref, dst_ref=all_ref.at[me], send_sem=send_sems.at[k - 1],
                recv_sem=recv_sems.at[k - 1], device_id=peer, device_id_type=MESH)
            cp.start()
            sends.append(cp)
        for k in range(1, N_DEV):
            peer, pidx = _peer(k)
            pltpu.make_async_remote_copy(
                src_ref=pk_ref, dst_ref=all_ref.at[pidx], send_sem=send_sems.at[k - 1],
                recv_sem=recv_sems.at[k - 1], device_id=peer, device_id_type=MESH).wait_recv()
        for cp in sends:
            cp.wait_send()
        g = all_ref[0]
        for e in range(1, N_DEV):
            g = g + all_ref[e]
        delta, mn, vn = _adamw(w_ref[...], g, m_ref[...], v_ref[...])
        g_out[...] = g
        d_out[...] = delta
        m_out[...] = mn
        v_out[...] = vn

    vm = pl.BlockSpec(memory_space=pltpu.VMEM)
    out = jax.ShapeDtypeStruct((PACK_ROWS, 128), F32)
    return _call(
        body, name=name,
        in_specs=[vm] * 4, out_specs=[vm] * 4, out_shape=[out] * 4,
        scratch_shapes=[pltpu.VMEM((N_DEV, PACK_ROWS, 128), F32), pltpu.SemaphoreType.DMA((N_DEV - 1,)),
                        pltpu.SemaphoreType.DMA((N_DEV - 1,))],
        compiler_params=pltpu.CompilerParams(has_side_effects=True),
    )(pk, w, m, v)


def _pack_small(mix, ffn, qn, kn, rel, loss):
    flat = jnp.concatenate([mix.reshape(-1), ffn.reshape(-1), qn.reshape(-1), kn.reshape(-1), rel.reshape(-1),
                            loss.reshape(-1)])
    return jnp.pad(flat, (0, PACK_LEN - flat.shape[0])).reshape(PACK_ROWS, 128)


def _unpack_small(pk):
    flat = pk.reshape(-1)
    return (flat[OFF_MIX:OFF_FFN].reshape(DEPTH, D_MODEL), flat[OFF_FFN:OFF_QN].reshape(DEPTH, D_MODEL),
            flat[OFF_QN:OFF_KN].reshape(2, HEAD_DIM), flat[OFF_KN:OFF_REL].reshape(2, HEAD_DIM),
            flat[OFF_REL:OFF_LOSS].reshape(2, N_HEADS, N_REL), flat[OFF_LOSS])


def _pair_gain(g):
    return jnp.concatenate([g, g]).reshape(1, PAIR)


def kernel(x, mix_norm, w_qkv, w_o, q_norm, k_norm, rel_bias, ffn_norm, w_up, w_down, loss_target, m_mix_norm, m_w_qkv, m_w_o, m_q_norm, m_k_norm, m_rel_bias, m_ffn_norm, m_w_up, m_w_down, v_mix_norm, v_w_qkv, v_w_o, v_q_norm, v_k_norm, v_rel_bias, v_ffn_norm, v_w_up, v_w_down):
    x0 = x[0]
    target = loss_target[0]
    shard_shapes = [w_qkv.shape[1:], w_o.shape[1:], w_up.shape[1:], w_down.shape[1:]]
    full_shapes = [(D_MODEL, 3 * D_MODEL), (D_MODEL, D_MODEL), (D_MODEL, D_FF), (D_FF, D_MODEL)]
    shards = [w_qkv.astype(BF16), w_o.astype(BF16), w_up.astype(BF16), w_down.astype(BF16)]
    nw = len(shards)
    weight = {}

    def fetch(keys, host, *args):
        outs, arrived = host(*args, rider=gather_exchange(shards, full_shapes, keys))
        weight.update(zip(keys, arrived))
        return outs

    first = [(0, 0)]
    weight.update(zip(first, run_exchange(gather_exchange(shards, full_shapes, first), "gather_first")))
    riding = {0: [(w, 0) for w in range(1, nw)] + [(w, 1) for w in range(nw)],
              1: [(w, 2) for w in range(nw)], 2: [(w, 3) for w in range(nw)]}
    tri = _tri_mats()

    saved = []
    xin = x0
    h0 = rms_fwd(x0, mix_norm[0:1], "rms_mix_0")
    for layer in range(DEPTH):
        mixer_b = layer % 2 == 1
        idx = layer // 2
        qkv_raw = dense(h0, weight[0, layer], nt=False, name=f"qkv_{layer}", out_dtype=BF16, tm=1024)
        if mixer_b:
            gq, gk = _pair_gain(q_norm[idx]), _pair_gain(k_norm[idx])
            qkn = headnorm_fwd(qkv_raw, gq, gk, f"headnorm_{layer}")
            rel_pad = jnp.pad(rel_bias[idx], ((0, 0), (0, N_REL_PAD - N_REL)))
            table = jnp.transpose(bias_table(rel_pad, f"bias_table_{layer}"), (1, 0, 2))
            if layer in riding:
                (o,) = fetch(riding[layer], ca_fwd, qkn, qkv_raw, table, f"ca_fwd_{layer}")
            else:
                (o,), _ = ca_fwd(qkn, qkv_raw, table, f"ca_fwd_{layer}")
            attn_saved = (qkv_raw, qkn, table)
        else:
            o, ltot = fetch(riding[layer], sb_fwd, qkv_raw, tri, f"sb_fwd_{layer}")
            attn_saved = (qkv_raw, ltot)
        x1, h1 = dense(o, weight[1, layer], nt=False, name=f"attn_out_{layer}", out_dtype=F32, tm=1024, res=xin,
                       norm_gain=ffn_norm[layer:layer + 1])
        pre = dense(h1, weight[2, layer], nt=False, name=f"up_{layer}", out_dtype=BF16, tm=1024)
        saved.append((xin, h0, attn_saved, o, x1, h1, pre))
        if layer + 1 < DEPTH:
            xin, h0 = dense(pre, weight[3, layer], nt=False, name=f"down_{layer}", out_dtype=F32, tm=512, relu2_in=True,
                            res=x1, norm_gain=mix_norm[layer + 1:layer + 2])
        else:
            xin = dense(pre, weight[3, layer], nt=False, name=f"down_{layer}", out_dtype=F32, tm=512, relu2_in=True,
                        res=x1)

    dx, loss_blk = loss_head(xin, target, "loss_head")

    partial = {}
    received = {}
    d_mix, d_ffn = [None] * DEPTH, [None] * DEPTH
    d_qn, d_kn, d_rel = [None] * 2, [None] * 2, [None] * 2

    def leaving():
        keys = [key for key in sorted(partial) if key not in received]
        return keys, scatter_exchange([partial[key] for key in keys], [shard_shapes[key[0]] for key in keys])

    def send(host, *args, **kw):
        keys, ex = leaving()
        outs, arrived = host(*args, rider=ex, **kw)
        received.update(zip(keys, arrived))
        return outs

    for layer in reversed(range(DEPTH)):
        mixer_b = layer % 2 == 1
        idx = layer // 2
        xin, h0, attn_saved, o, x1, h1, pre = saved[layer]
        g_qkv, g_o, g_up, g_down = (weight[w, layer] for w in range(nw))
        dpre = dense(dx, g_down, nt=True, name=f"d_pre_{layer}", out_dtype=BF16, tm=512, relu2_grad=pre)
        partial[3, layer] = dense_tn(pre, dx, tk=2048, bn=1024, name=f"dw_down_{layer}", relu2_in=True)
        partial[2, layer] = dense_tn(h1, dpre, tk=1024, bn=2048, name=f"dw_up_{layer}")
        dx, d_ffn[layer] = dense(dpre, g_up, nt=True, name=f"d_h1_{layer}", out_dtype=F32, tm=512,
                                 rms_back=(x1, ffn_norm[layer:layer + 1], dx))
        do = dense(dx, g_o, nt=True, name=f"d_o_{layer}", out_dtype=BF16, tm=1024)
        partial[1, layer] = dense_tn(o, dx, tk=1024, bn=1024, name=f"dw_o_{layer}")
        if mixer_b:
            qkv_raw, qkn, table = attn_saved
            dq, dk, dv, dtab = send(ca_bwd, qkn, qkv_raw, o, do, table, f"ca_bwd_{layer}")
            gq, gk = _pair_gain(q_norm[idx]), _pair_gain(k_norm[idx])
            dqkv, dgain = headnorm_bwd(dq, dk, dv, qkv_raw, gq, gk, f"headnorm_bwd_{layer}")
            d_qn[idx] = dgain[0, :HEAD_DIM] + dgain[0, HEAD_DIM:]
            d_kn[idx] = dgain[1, :HEAD_DIM] + dgain[1, HEAD_DIM:]
            d_rel[idx] = bias_fold(jnp.transpose(dtab, (1, 0, 2)), f"bias_fold_{layer}")[:, :N_REL]
        else:
            qkv_raw, ltot = attn_saved
            dq, dk, dv = send(sb_bwd, qkv_raw, do, ltot, tri, f"sb_bwd_{layer}")
            dqkv = jnp.concatenate([dq, dk, dv], axis=1)
        partial[0, layer] = dense_tn(h0, dqkv, tk=1024, bn=1536, name=f"dw_qkv_{layer}")
        back = dict(nt=True, name=f"d_h0_{layer}", out_dtype=F32, tm=512, rms_back=(xin, mix_norm[layer:layer + 1], dx))
        if layer == 0:
            dx, dg = send(dense, dqkv, g_qkv, **back)
            d_mix[layer] = dg[0:1]
        else:
            dx, d_mix[layer] = dense(dqkv, g_qkv, **back)

    grad_x = dx[None]


    def big(wi, w, m, v, name):
        return adam_shard([received[wi, layer] for layer in range(DEPTH)], w, m, v, name)

    a_qkv = big(0, w_qkv, m_w_qkv, v_w_qkv, "adam_qkv")
    a_o = big(1, w_o, m_w_o, v_w_o, "adam_o")
    a_up = big(2, w_up, m_w_up, v_w_up, "adam_up")
    a_down = big(3, w_down, m_w_down, v_w_down, "adam_down")

    pk = _pack_small(jnp.concatenate(d_mix), jnp.concatenate(d_ffn), jnp.stack(d_qn), jnp.stack(d_kn),
                     jnp.stack(d_rel), loss_blk[0, 0])
    zero = jnp.zeros((), F32)
    pw = _pack_small(mix_norm, ffn_norm, q_norm, k_norm, rel_bias, zero)
    pm = _pack_small(m_mix_norm, m_ffn_norm, m_q_norm, m_k_norm, m_rel_bias, zero)
    pv = _pack_small(v_mix_norm, v_ffn_norm, v_q_norm, v_k_norm, v_rel_bias, zero)
    s_g, s_d, s_m, s_v = small_allreduce_adam(pk, pw, pm, pv, "small_allreduce_adam")
    g_mix, g_ffn, g_qn, g_kn, g_rel, loss = _unpack_small(s_g)
    dl_mix, dl_ffn, dl_qn, dl_kn, dl_rel, _ = _unpack_small(s_d)
    nm_mix, nm_ffn, nm_qn, nm_kn, nm_rel, _ = _unpack_small(s_m)
    nv_mix, nv_ffn, nv_qn, nv_kn, nv_rel, _ = _unpack_small(s_v)

    return (loss, grad_x,
            g_mix, a_qkv[0], a_o[0], g_qn, g_kn, g_rel, g_ffn, a_up[0], a_down[0],
            dl_mix, a_qkv[1], a_o[1], dl_qn, dl_kn, dl_rel, dl_ffn, a_up[1], a_down[1],
            nm_mix, a_qkv[2], a_o[2], nm_qn, nm_kn, nm_rel, nm_ffn, a_up[2], a_down[2],
            nv_mix, a_qkv[3], a_o[3], nv_qn, nv_kn, nv_rel, nv_ffn, a_up[3], a_down[3])
```

```python
import jax
import jax.numpy as jnp
from jax import lax
from jax.experimental import pallas as pl
from jax.experimental.pallas import tpu as pltpu

F32 = jnp.float32
BF16 = jnp.bfloat16
MESH = pl.DeviceIdType.MESH

D_MODEL = 1024
N_HEADS = 16
HEAD_DIM = 64
PAIR = 2 * HEAD_DIM
N_PAIRS = N_HEADS // 2
D_FF = 4 * D_MODEL
DEPTH = 4
N_DEV = 8
RMS_EPS = 1e-6
QK_SCALE = HEAD_DIM ** -0.5

SB_TILE = 256
SB_DEAD = 104.0
Q_SUBS = 4

CHUNK = 64
LEFT_CHUNKS = 8
CA_QBLK = 256
CA_PAD = LEFT_CHUNKS * CHUNK
CA_WIN = CA_QBLK + CA_PAD
CA_SUB = 128
CA_SUBWIN = CA_SUB + CA_PAD
CA_RING = 1024
MAX_REL = 256
N_REL = 2 * MAX_REL + 1
N_REL_PAD = 640
NEG_BIG = -1e30

ADAM_LR = 0.001
ADAM_B1 = 0.9
ADAM_B2 = 0.999
ADAM_EPS = 1e-08
ADAM_WD = 0.01
ADAM_STEP = 10

ROW_TILE = 512
DENSE_CHUNK = 512
TN_ROWS = 1024
VMEM_LIMIT = 56 * 1024 * 1024

OFF_MIX = 0
OFF_FFN = OFF_MIX + DEPTH * D_MODEL
OFF_QN = OFF_FFN + DEPTH * D_MODEL
OFF_KN = OFF_QN + 2 * HEAD_DIM
OFF_REL = OFF_KN + 2 * HEAD_DIM
OFF_LOSS = OFF_REL + 2 * N_HEADS * N_REL
PACK_ROWS = 200
PACK_LEN = PACK_ROWS * 128


def _call(body, **kw):
    return pl.pallas_call(body, **kw)


def _params(sem=None, vmem=VMEM_LIMIT):
    if sem is None:
        return pltpu.CompilerParams(vmem_limit_bytes=vmem)
    return pltpu.CompilerParams(dimension_semantics=sem, vmem_limit_bytes=vmem)


def _nt(a, b):
    return lax.dot_general(a, b, (((1,), (1,)), ((), ())), preferred_element_type=F32)


def _tn(a, b):
    return lax.dot_general(a, b, (((0,), (0,)), ((), ())), preferred_element_type=F32)


def _nn(a, b):
    return jnp.dot(a, b, preferred_element_type=F32)


def _split_bf16(v):
    hi = v.astype(BF16)
    lo = (v - hi.astype(F32)).astype(BF16)
    return hi, lo


def rms_fwd(x, g, name):
    s = x.shape[0]
    tm = ROW_TILE

    def body(x_ref, g_ref, o_ref):
        xv = x_ref[...]
        r = lax.rsqrt(jnp.mean(xv * xv, axis=-1, keepdims=True) + RMS_EPS)
        o_ref[...] = (xv * r * g_ref[...]).astype(o_ref.dtype)

    return _call(
        body, name=name, grid=(s // tm,),
        in_specs=[pl.BlockSpec((tm, D_MODEL), lambda i: (i, 0)), pl.BlockSpec((1, D_MODEL), lambda i: (0, 0))],
        out_specs=pl.BlockSpec((tm, D_MODEL), lambda i: (i, 0)),
        out_shape=jax.ShapeDtypeStruct((s, D_MODEL), BF16),
        compiler_params=_params(("parallel",)),
    )(x, g)


def loss_head(y, target, name):
    s = y.shape[0]
    tm = ROW_TILE

    def body(y_ref, t_ref, dy_ref, l_ref):
        i = pl.program_id(0)
        e = y_ref[...] - t_ref[...]
        dy_ref[...] = e * (1.0 / D_MODEL)

        @pl.when(i == 0)
        def _():
            l_ref[...] = jnp.zeros_like(l_ref)

        per_row = jnp.sum(e * e, axis=-1, keepdims=True) * (1.0 / D_MODEL)
        l_ref[...] += jnp.broadcast_to(0.5 * jnp.sum(per_row, axis=0, keepdims=True), l_ref.shape)

    row = pl.BlockSpec((tm, D_MODEL), lambda i: (i, 0))
    return _call(
        body, name=name, grid=(s // tm,),
        in_specs=[row, row],
        out_specs=[row, pl.BlockSpec((8, 128), lambda i: (0, 0))],
        out_shape=[jax.ShapeDtypeStruct((s, D_MODEL), F32), jax.ShapeDtypeStruct((8, 128), F32)],
        compiler_params=_params(("arbitrary",)),
    )(y, target)


def _group_sum_matrix():
    r = lax.broadcasted_iota(jnp.int32, (PAIR, PAIR), 0) // HEAD_DIM
    c = lax.broadcasted_iota(jnp.int32, (PAIR, PAIR), 1) // HEAD_DIM
    return (r == c).astype(BF16)


def _head_mean(v, gmat):
    hi, lo = _split_bf16(v)
    return (_nn(hi, gmat) + _nn(lo, gmat)) * (1.0 / HEAD_DIM)


def headnorm_fwd(qkv, gq, gk, name):
    s = qkv.shape[0]
    tm = ROW_TILE

    def body(x_ref, gq_ref, gk_ref, o_ref):
        j = pl.program_id(1)
        gmat = _group_sum_matrix()
        gain = jnp.where(j == 0, gq_ref[...], gk_ref[...])
        for b in range(N_PAIRS):
            cols = slice(b * PAIR, (b + 1) * PAIR)
            xv = x_ref[:, cols].astype(F32)
            r = lax.rsqrt(_head_mean(xv * xv, gmat) + RMS_EPS)
            o_ref[:, cols] = (xv * r * gain).astype(o_ref.dtype)

    blk = pl.BlockSpec((tm, D_MODEL), lambda i, j: (i, j))
    gspec = pl.BlockSpec((1, PAIR), lambda i, j: (0, 0))
    return _call(
        body, name=name, grid=(s // tm, 2),
        in_specs=[blk, gspec, gspec], out_specs=blk,
        out_shape=jax.ShapeDtypeStruct((s, 2 * D_MODEL), BF16),
        compiler_params=_params(("parallel", "parallel")),
    )(qkv, gq, gk)


def headnorm_bwd(dq, dk, dv, qkv, gq, gk, name):
    s = qkv.shape[0]
    tm = ROW_TILE

    def body(dq_ref, dk_ref, dv_ref, x_ref, gq_ref, gk_ref, dx_ref, dg_ref):
        i = pl.program_id(0)
        j = pl.program_id(1)

        @pl.when((i == 0) & (j == 0))
        def _():
            dg_ref[...] = jnp.zeros_like(dg_ref)

        @pl.when(j == 2)
        def _():
            dx_ref[...] = dv_ref[...]

        for part, (d_ref, g_ref) in enumerate(((dq_ref, gq_ref), (dk_ref, gk_ref))):
            @pl.when(j == part)
            def _(part=part, d_ref=d_ref, g_ref=g_ref):
                gmat = _group_sum_matrix()
                gain = g_ref[...]
                dg = jnp.zeros((1, PAIR), F32)
                for b in range(N_PAIRS):
                    cols = slice(b * PAIR, (b + 1) * PAIR)
                    xv = x_ref[:, cols].astype(F32)
                    dn = d_ref[:, cols].astype(F32)
                    r = lax.rsqrt(_head_mean(xv * xv, gmat) + RMS_EPS)
                    xh = xv * r
                    dy = dn * gain
                    dx_ref[:, cols] = (r * (dy - xh * _head_mean(dy * xh, gmat))).astype(dx_ref.dtype)
                    dg = dg + jnp.sum(dn * xh, axis=0, keepdims=True)
                dg_ref[part:part + 1, :] += dg

    row = pl.BlockSpec((tm, D_MODEL), lambda i, j: (i, 0))
    blk = pl.BlockSpec((tm, D_MODEL), lambda i, j: (i, j))
    gspec = pl.BlockSpec((1, PAIR), lambda i, j: (0, 0))
    dx, dg = _call(
        body, name=name, grid=(s // tm, 3),
        in_specs=[row, row, row, blk, gspec, gspec],
        out_specs=[blk, pl.BlockSpec((8, PAIR), lambda i, j: (0, 0))],
        out_shape=[jax.ShapeDtypeStruct(qkv.shape, BF16), jax.ShapeDtypeStruct((8, PAIR), F32)],
        compiler_params=_params(("arbitrary", "arbitrary")),
    )(dq, dk, dv, qkv, gq, gk)
    return dx, dg[0:2]


def _relu2(a):
    r = jnp.maximum(a.astype(F32), 0.0)
    return r * r


def dense(a, w, *, nt, name, out_dtype, tm, relu2_in=False, relu2_grad=None, res=None, norm_gain=None,
          rms_back=None, rider=None):
    s, k = a.shape
    n = w.shape[0] if nt else w.shape[1]
    nc = min(n, DENSE_CHUNK)

    def body(*refs):
        it = iter(refs)
        a_ref, w_ref = next(it), next(it)
        g_ref = next(it) if relu2_grad is not None else None
        r_ref = next(it) if res is not None else None
        ng_ref = next(it) if norm_gain is not None else None
        if rms_back is not None:
            x_ref, gain_ref, dres_ref = next(it), next(it), next(it)
        o_ref = next(it)
        h_ref = next(it) if norm_gain is not None else None
        dg_ref = next(it) if rms_back is not None else None
        av = a_ref[...]
        av = _relu2(av).astype(BF16) if relu2_in else av.astype(BF16)
        for c0 in range(0, n, nc):
            cols = slice(c0, c0 + nc)
            acc = _nt(av, w_ref[cols, :]) if nt else _nn(av, w_ref[:, cols])
            if g_ref is not None:
                acc = acc * (2.0 * jnp.maximum(g_ref[:, cols].astype(F32), 0.0))
            if r_ref is not None:
                acc = acc + r_ref[:, cols]
            o_ref[:, cols] = acc.astype(o_ref.dtype)
        if norm_gain is not None:
            xv = o_ref[...]
            r = lax.rsqrt(jnp.mean(xv * xv, axis=-1, keepdims=True) + RMS_EPS)
            h_ref[...] = (xv * r * ng_ref[...]).astype(h_ref.dtype)
        if rms_back is not None:
            dhv = o_ref[...]
            xv = x_ref[...]
            r = lax.rsqrt(jnp.mean(xv * xv, axis=-1, keepdims=True) + RMS_EPS)
            xh = xv * r
            dy = dhv * gain_ref[...]
            mdot = jnp.mean(dy * xh, axis=-1, keepdims=True)
            o_ref[...] = dres_ref[...] + r * (dy - xh * mdot)

            @pl.when(pl.program_id(0) == 0)
            def _():
                dg_ref[...] = jnp.zeros_like(dg_ref)

            dg_ref[0:1, :] += jnp.sum(dhv * xh, axis=0, keepdims=True)

    oblk = pl.BlockSpec((tm, n), lambda i: (i, 0))
    gblk = pl.BlockSpec((1, n), lambda i: (0, 0))
    in_specs = [pl.BlockSpec((tm, k), lambda i: (i, 0)), pl.BlockSpec(w.shape, lambda i: (0, 0))]
    args = [a, w]
    for e in (relu2_grad, res):
        if e is not None:
            in_specs.append(oblk)
            args.append(e)
    out_specs, out_shape = [oblk], [jax.ShapeDtypeStruct((s, n), out_dtype)]
    if norm_gain is not None:
        in_specs.append(gblk)
        args.append(norm_gain)
        out_specs.append(oblk)
        out_shape.append(jax.ShapeDtypeStruct((s, n), BF16))
    if rms_back is not None:
        in_specs += [oblk, gblk, oblk]
        args += list(rms_back)
        out_specs.append(pl.BlockSpec((8, n), lambda i: (0, 0)))
        out_shape.append(jax.ShapeDtypeStruct((8, n), F32))
    def first_last():
        return pl.program_id(0) == 0, pl.program_id(0) == s // tm - 1

    outs, arrived = hosted_call(
        body, first_last, rider, name=name, grid=(s // tm,), in_specs=in_specs, out_specs=out_specs,
        out_shape=out_shape, scratch_shapes=[], args=args,
        compiler_params=_params(("arbitrary",) if rms_back is not None or rider is not None else ("parallel",)),
    )
    if rider is not None:
        return outs, arrived
    if rms_back is not None:
        return outs[0], outs[1][0:1]
    return outs if norm_gain is not None else outs[0]


def dense_tn(a, b, *, tk, bn, name, relu2_in=False):
    s, k = a.shape
    n = b.shape[1]
    ts = TN_ROWS
    ns = s // ts
    nc = min(bn, DENSE_CHUNK)

    def body(a_ref, b_ref, o_ref, acc_ref):
        t = pl.program_id(2)

        @pl.when(t == 0)
        def _():
            acc_ref[...] = jnp.zeros_like(acc_ref)

        av = a_ref[...]
        av = _relu2(av).astype(BF16) if relu2_in else av.astype(BF16)
        at = av.T
        for c0 in range(0, bn, nc):
            cols = slice(c0, c0 + nc)
            acc_ref[:, cols] += _nn(at, b_ref[:, cols].astype(BF16))

        @pl.when(t == ns - 1)
        def _():
            o_ref[...] = acc_ref[...].astype(o_ref.dtype)

    return _call(
        body, name=name, grid=(k // tk, n // bn, ns),
        in_specs=[pl.BlockSpec((ts, tk), lambda kb, j, t: (t, kb)), pl.BlockSpec((ts, bn), lambda kb, j, t: (t, j))],
        out_specs=pl.BlockSpec((tk, bn), lambda kb, j, t: (kb, j)),
        out_shape=jax.ShapeDtypeStruct((k, n), BF16),
        scratch_shapes=[pltpu.VMEM((tk, bn), F32)],
        compiler_params=_params(("parallel", "parallel", "arbitrary")),
    )(a, b)


def _softplus(z):
    return jnp.maximum(z, 0.0) + jnp.log(1.0 + jnp.exp(-jnp.abs(z)))


def _tri_mats():
    t = SB_TILE
    r = jnp.arange(2 * t)[:, None] % t
    c = jnp.arange(t)[None, :]
    return jnp.stack([(r > c), (r < c), (r <= c)]).astype(BF16)


def _split_sum(v, u2):
    hi, lo = _split_bf16(v)
    return _nn(jnp.concatenate([hi, lo], axis=1), u2)


def sb_fwd(qkv, tri, name, rider=None):
    s = qkv.shape[0]
    t = SB_TILE
    tb = Q_SUBS * t

    def block(i, q_ref, k_ref, v_ref, u_ref, o_ref, lt_ref, c0_ref, c1_ref, acc0_ref, acc1_ref):
        lane = lax.broadcasted_iota(jnp.int32, (t, PAIR), 1)
        causal = lax.broadcasted_iota(jnp.int32, (t, t), 1) < lax.broadcasted_iota(jnp.int32, (t, t), 0)
        qv = q_ref[...]
        u_after = u_ref[0]
        qhs = [jnp.where(lane < HEAD_DIM, qv, 0) * QK_SCALE, jnp.where(lane >= HEAD_DIM, qv, 0) * QK_SCALE]
        cs = [c0_ref, c1_ref]
        accs = [acc0_ref, acc1_ref]
        for r in cs + accs:
            r[...] = jnp.zeros_like(r)

        half = t // 2
        chains = [(h, slice(r * half, (r + 1) * half)) for h in range(2) for r in range(2)]

        def sweep(tiles):
            kvs = []
            for j, _ in tiles:
                off = pl.multiple_of(j * t, t)
                kvs.append((k_ref[pl.ds(off, t), :], v_ref[pl.ds(off, t), :]))
            work = [(k, h, rows) for k in range(len(tiles)) for h, rows in chains]
            zs = [_nt(qhs[h][rows, :], kvs[k][0]) for k, h, rows in work]
            ls = [-_softplus(z) for z in zs]
            ls = [jnp.where(causal[rows, :], l, 0.0) if tiles[k][1] else l for l, (k, h, rows) in zip(ls, work)]
            afters = [_split_sum(l, u_after) for l in ls]
            run = {}
            for w, (k, h, rows) in enumerate(work):
                c = run.get((h, rows.start), cs[h][rows, :])
                a = jnp.exp(zs[w] + ls[w] + afters[w] + c)
                if tiles[k][1]:
                    a = jnp.where(causal[rows, :], a, 0.0)
                av = _nn(a.astype(BF16), kvs[k][1])
                run[h, rows.start] = c + (afters[w][:, 0:1] + ls[w][:, 0:1])
                run["acc", h, rows.start] = run.get(("acc", h, rows.start), 0.0) + av
            for h, rows in chains:
                accs[h][rows, :] += run["acc", h, rows.start]
                cs[h][rows, :] = run[h, rows.start]

        def alive():
            return jnp.max(jnp.maximum(c0_ref[...], c1_ref[...])) > -SB_DEAD

        @pl.when(i == 0)
        def _():
            sweep([(i, True)])

        @pl.when(i > 0)
        def _():
            sweep([(i, True), (i - 1, False)])

        def cond(st):
            return (st[0] < i) & st[1]

        def step(st):
            sweep([(i - 1 - st[0], False)])
            return st[0] + 1, alive()

        swept, _ = lax.while_loop(cond, step, (jnp.minimum(i, 1), alive()))
        o_ref[...] = jnp.where(lane < HEAD_DIM, acc0_ref[...], acc1_ref[...]).astype(o_ref.dtype)
        lt_ref[...] = jnp.where(lane == 0, c0_ref[...], jnp.where(lane == 1, c1_ref[...],
                                jnp.where(lane == 2, swept.astype(F32), 0.0)))

    def body(q_ref, k_ref, v_ref, u_ref, o_ref, lt_ref, *scratch):
        for sub in range(Q_SUBS):
            rows = pl.ds(sub * t, t)
            block(pl.program_id(1) * Q_SUBS + sub, q_ref.at[rows], k_ref, v_ref, u_ref, o_ref.at[rows],
                  lt_ref.at[rows], *scratch)

    def first_last():
        p, i = pl.program_id(0), pl.program_id(1)
        return (p == 0) & (i == 0), (p == N_PAIRS - 1) & (i == s // tb - 1)

    return hosted_call(
        body, first_last, rider, name=name, grid=(N_PAIRS, s // tb),
        in_specs=[
            pl.BlockSpec((tb, PAIR), lambda p, i: (i, p)),
            pl.BlockSpec((s, PAIR), lambda p, i: (0, N_PAIRS + p)),
            pl.BlockSpec((s, PAIR), lambda p, i: (0, 2 * N_PAIRS + p)),
            pl.BlockSpec((3, 2 * t, t), lambda p, i: (0, 0, 0)),
        ],
        out_specs=[pl.BlockSpec((tb, PAIR), lambda p, i: (i, p)), pl.BlockSpec((None, tb, PAIR), lambda p, i: (p, i, 0))],
        out_shape=[jax.ShapeDtypeStruct((s, D_MODEL), BF16), jax.ShapeDtypeStruct((N_PAIRS, s, PAIR), F32)],
        scratch_shapes=[pltpu.VMEM((t, 1), F32), pltpu.VMEM((t, 1), F32), pltpu.VMEM((t, PAIR), F32),
                        pltpu.VMEM((t, PAIR), F32)],
        args=(qkv, qkv, qkv, tri),
        compiler_params=_params(("arbitrary", "arbitrary")),
    )


def sb_bwd(qkv, do, ltot, tri, name, rider=None):
    s = qkv.shape[0]
    t = SB_TILE
    nq = s // t

    def block(i, q_ref, k_ref, v_ref, do_ref, lt_ref, u_ref, dq_ref, dk_ref, dv_ref,
              pl0_ref, pl1_ref, pg0_ref, pg1_ref, dqa0_ref, dqa1_ref, dka_ref, dva_ref):

        @pl.when(i == 0)
        def _():
            dka_ref[...] = jnp.zeros_like(dka_ref)
            dva_ref[...] = jnp.zeros_like(dva_ref)

        lane = lax.broadcasted_iota(jnp.int32, (t, PAIR), 1)
        causal = lax.broadcasted_iota(jnp.int32, (t, t), 1) < lax.broadcasted_iota(jnp.int32, (t, t), 0)
        qv = q_ref[...]
        dov = do_ref[...]
        ltv = lt_ref[...]
        u_before = u_ref[1]
        u_upto = u_ref[2, 0:t, :]
        hms = [lane < HEAD_DIM, lane >= HEAD_DIM]
        qhs = [jnp.where(hm, qv, 0) * QK_SCALE for hm in hms]
        dohs = [jnp.where(hm, dov, 0) for hm in hms]
        ltots = [ltv[:, h:h + 1] for h in range(2)]
        pls = [pl0_ref, pl1_ref]
        pgs = [pg0_ref, pg1_ref]
        dqas = [dqa0_ref, dqa1_ref]
        for r in pls + pgs + dqas:
            r[...] = jnp.zeros_like(r)

        half = t // 2
        chains = [(h, slice(r * half, (r + 1) * half)) for h in range(2) for r in range(2)]

        def sweep(tiles):
            offs = [pl.multiple_of(j * t, t) for j, _ in tiles]
            kvs = [(k_ref[pl.ds(off, t), :], v_ref[pl.ds(off, t), :]) for off in offs]
            work = [(k, h, rows) for k in range(len(tiles)) for h, rows in chains]
            qcs = [qhs[h][rows, :] for k, h, rows in work]
            docs = [dohs[h][rows, :] for k, h, rows in work]
            zs = [_nt(qcs[w], kvs[k][0]) for w, (k, h, rows) in enumerate(work)]
            das = [_nt(docs[w], kvs[k][1]) for w, (k, h, rows) in enumerate(work)]
            sps = [_softplus(z) for z in zs]
            ls = [jnp.where(causal[rows, :], -sp, 0.0) if tiles[k][1] else -sp for sp, (k, h, rows) in zip(sps, work)]
            befores = [_split_sum(l, u_before) for l in ls]
            run = {}
            dzbs, abs_ = [], []
            for w, (k, h, rows) in enumerate(work):
                key = (h, rows.start)
                pl_c = run.get(("pl",) + key, pls[h][rows, :])
                pg_c = run.get(("pg",) + key, pgs[h][rows, :])
                a = jnp.exp(zs[w] + (ltots[h][rows, :] - pl_c - befores[w]))
                if tiles[k][1]:
                    a = jnp.where(causal[rows, :], a, 0.0)
                g = a * das[w]
                upto = _nn(g.astype(BF16), u_upto)
                dz = g - jnp.exp(zs[w] - sps[w]) * (pg_c + upto)
                if tiles[k][1]:
                    dz = jnp.where(causal[rows, :], dz, 0.0)
                dzbs.append(dz.astype(BF16))
                abs_.append(a.astype(BF16))
                run[("pl",) + key] = pl_c + (befores[w][:, t - 1:t] + ls[w][:, t - 1:t])
                run[("pg",) + key] = pg_c + upto[:, t - 1:t]
                run[("dq",) + key] = run.get(("dq",) + key, 0.0) + _nn(dzbs[w], kvs[k][0])
            for k in range(len(tiles)):
                mine = [w for w, wk in enumerate(work) if wk[0] == k]
                dka_ref[pl.ds(offs[k], t), :] += sum(_tn(dzbs[w], qcs[w]) for w in mine)
                dva_ref[pl.ds(offs[k], t), :] += sum(_tn(abs_[w], docs[w]) for w in mine)
            for h, rows in chains:
                key = (h, rows.start)
                dqas[h][rows, :] += run[("dq",) + key]
                pls[h][rows, :] = run[("pl",) + key]
                pgs[h][rows, :] = run[("pg",) + key]

        def step(j, carry):
            sweep([(j, False)])
            return carry

        swept = jnp.max(ltv[0:8, 2:3]).astype(jnp.int32)
        lax.fori_loop(i - swept, i, step, 0)
        sweep([(i, True)])

        dq_ref[...] = (jnp.where(lane < HEAD_DIM, dqa0_ref[...], dqa1_ref[...]) * QK_SCALE).astype(dq_ref.dtype)

        @pl.when(i == nq - 1)
        def _():
            dk_ref[...] = dka_ref[...].astype(dk_ref.dtype)
            dv_ref[...] = dva_ref[...].astype(dv_ref.dtype)

    qblk = pl.BlockSpec((Q_SUBS * t, PAIR), lambda p, i: (i, p))
    full = pl.BlockSpec((s, PAIR), lambda p, i: (0, p))
    out = jax.ShapeDtypeStruct((s, D_MODEL), BF16)

    def body(q_ref, k_ref, v_ref, do_ref, lt_ref, u_ref, dq_ref, dk_ref, dv_ref, *scratch):
        for sub in range(Q_SUBS):
            rows = pl.ds(sub * t, t)
            block(pl.program_id(1) * Q_SUBS + sub, q_ref.at[rows], k_ref, v_ref, do_ref.at[rows], lt_ref.at[rows],
                  u_ref, dq_ref.at[rows], dk_ref, dv_ref, *scratch)

    def first_last():
        p, i = pl.program_id(0), pl.program_id(1)
        return (p == 0) & (i == 0), (p == N_PAIRS - 1) & (i == nq // Q_SUBS - 1)

    return hosted_call(
        body, first_last, rider, name=name, grid=(N_PAIRS, nq // Q_SUBS),
        in_specs=[
            qblk,
            pl.BlockSpec((s, PAIR), lambda p, i: (0, N_PAIRS + p)),
            pl.BlockSpec((s, PAIR), lambda p, i: (0, 2 * N_PAIRS + p)),
            qblk,
            pl.BlockSpec((None, Q_SUBS * t, PAIR), lambda p, i: (p, i, 0)),
            pl.BlockSpec((3, 2 * t, t), lambda p, i: (0, 0, 0)),
        ],
        out_specs=[qblk, full, full],
        out_shape=[out, out, out],
        scratch_shapes=[pltpu.VMEM((t, 1), F32)] * 4 + [pltpu.VMEM((t, PAIR), F32)] * 2
        + [pltpu.VMEM((s, PAIR), F32), pltpu.VMEM((s, PAIR), F32)],
        args=(qkv, qkv, qkv, do, ltot, tri),
        compiler_params=_params(("arbitrary", "arbitrary")),
    )


def _ring_onehot():
    r = lax.broadcasted_iota(jnp.int32, (N_REL_PAD, CA_RING), 0)
    w = lax.broadcasted_iota(jnp.int32, (N_REL_PAD, CA_RING), 1)
    idx = jnp.where(w <= CA_WIN, jnp.clip(CA_WIN - w, 0, 2 * MAX_REL), 2 * MAX_REL)
    return (r == idx).astype(F32)


def bias_table(rel_pad, name):
    def body(rb_ref, o_ref):
        ring = jnp.dot(rb_ref[...], _ring_onehot(), preferred_element_type=F32, precision=lax.Precision.HIGHEST)
        col = lax.broadcasted_iota(jnp.int32, (N_HEADS, CA_WIN), 1)

        def row(tl, carry):
            v = pltpu.roll(ring, tl, 1)[:, :CA_WIN]
            first = (tl // CHUNK) * CHUNK
            ok = (col >= first) & (col < first + (LEFT_CHUNKS + 1) * CHUNK)
            o_ref[:, tl, :] = jnp.where(ok, v, NEG_BIG)
            return carry

        lax.fori_loop(0, CA_QBLK, row, 0, unroll=8)

    return _call(
        body, name=name,
        in_specs=[pl.BlockSpec(memory_space=pltpu.VMEM)], out_specs=pl.BlockSpec(memory_space=pltpu.VMEM),
        out_shape=jax.ShapeDtypeStruct((N_HEADS, CA_QBLK, CA_WIN), F32),
        compiler_params=_params(),
    )(rel_pad)


def bias_fold(db, name):
    def body(db_ref, o_ref):
        zeros = jnp.zeros((N_HEADS, CA_RING - CA_WIN), F32)

        def row(tl, acc):
            v = jnp.concatenate([db_ref[:, tl, :], zeros], axis=1)
            return acc + pltpu.roll(v, (CA_RING - tl) % CA_RING, 1)

        ring = lax.fori_loop(0, CA_QBLK, row, jnp.zeros((N_HEADS, CA_RING), F32))
        o_ref[...] = lax.dot_general(ring, _ring_onehot(), (((1,), (1,)), ((), ())),
                                     preferred_element_type=F32, precision=lax.Precision.HIGHEST)

    return _call(
        body, name=name,
        in_specs=[pl.BlockSpec(memory_space=pltpu.VMEM)], out_specs=pl.BlockSpec(memory_space=pltpu.VMEM),
        out_shape=jax.ShapeDtypeStruct((N_HEADS, N_REL_PAD), F32),
        compiler_params=_params(),
    )(db)


def _ca_logits(qh, kw, bias_h, first_key_ok):
    sc = _nt(qh, kw) + bias_h
    col = lax.broadcasted_iota(jnp.int32, sc.shape, 1)
    return jnp.where(col >= first_key_ok, sc, NEG_BIG)


def ca_fwd(qkn, qkv, table, name, rider=None):
    s = qkv.shape[0]
    tq = CA_QBLK

    def block(i, q_ref, k_ref, v_ref, b_ref, o_ref, kp_ref, vp_ref):

        @pl.when(i == 0)
        def _():
            kp_ref[0:CA_PAD, :] = jnp.zeros((CA_PAD, PAIR), BF16)
            vp_ref[0:CA_PAD, :] = jnp.zeros((CA_PAD, PAIR), BF16)
            kp_ref[CA_PAD:, :] = k_ref[...]
            vp_ref[CA_PAD:, :] = v_ref[...]

        lane = lax.broadcasted_iota(jnp.int32, (CA_SUB, PAIR), 1)
        hms = [lane < HEAD_DIM, lane >= HEAD_DIM]
        subs = range(tq // CA_SUB)
        rows = [slice(r * CA_SUB, (r + 1) * CA_SUB) for r in subs]
        cols = [slice(r * CA_SUB, r * CA_SUB + CA_SUBWIN) for r in subs]
        offs = [pl.multiple_of(i * tq + r * CA_SUB, CA_SUB) for r in subs]
        kws = [kp_ref[pl.ds(off, CA_SUBWIN), :] for off in offs]
        vws = [vp_ref[pl.ds(off, CA_SUBWIN), :] for off in offs]
        chains = [(r, h) for r in subs for h in range(2)]
        qhs = [jnp.where(hms[h], q_ref[rows[r], :], 0) * QK_SCALE for r, h in chains]
        scs = [_ca_logits(qhs[c], kws[r], b_ref[h, rows[r], cols[r]], CA_PAD - i * tq - r * CA_SUB)
               for c, (r, h) in enumerate(chains)]
        es = [jnp.exp(sc - jnp.max(sc, axis=1, keepdims=True)) for sc in scs]
        ps = [(e / jnp.sum(e, axis=1, keepdims=True)).astype(BF16) for e in es]
        outs = [_nn(ps[c], vws[r]) for c, (r, h) in enumerate(chains)]
        for r in subs:
            o_ref[rows[r], :] = jnp.where(lane < HEAD_DIM, outs[2 * r], outs[2 * r + 1]).astype(o_ref.dtype)

    def body(q_ref, k_ref, v_ref, b_ref, o_ref, *scratch):
        for sub in range(Q_SUBS):
            rows = pl.ds(sub * tq, tq)
            block(pl.program_id(1) * Q_SUBS + sub, q_ref.at[rows], k_ref, v_ref, b_ref, o_ref.at[rows], *scratch)

    def first_last():
        p, i = pl.program_id(0), pl.program_id(1)
        return (p == 0) & (i == 0), (p == N_PAIRS - 1) & (i == s // (Q_SUBS * tq) - 1)

    return hosted_call(
        body, first_last, rider, name=name, grid=(N_PAIRS, s // (Q_SUBS * tq)),
        in_specs=[
            pl.BlockSpec((Q_SUBS * tq, PAIR), lambda p, i: (i, p)),
            pl.BlockSpec((s, PAIR), lambda p, i: (0, N_PAIRS + p)),
            pl.BlockSpec((s, PAIR), lambda p, i: (0, 2 * N_PAIRS + p)),
            pl.BlockSpec((2, tq, CA_WIN), lambda p, i: (p, 0, 0)),
        ],
        out_specs=[pl.BlockSpec((Q_SUBS * tq, PAIR), lambda p, i: (i, p))],
        out_shape=[jax.ShapeDtypeStruct((s, D_MODEL), BF16)],
        scratch_shapes=[pltpu.VMEM((s + CA_PAD, PAIR), BF16), pltpu.VMEM((s + CA_PAD, PAIR), BF16)],
        args=(qkn, qkn, qkv, table),
        compiler_params=_params(("arbitrary", "arbitrary")),
    )


def ca_bwd(qkn, qkv, o, do, table, name, rider=None):
    s = qkv.shape[0]
    tq = CA_QBLK
    nq = s // tq

    def block(i, q_ref, k_ref, v_ref, o_ref, do_ref, b_ref, dq_ref, dk_ref, dv_ref, db_ref,
              kp_ref, vp_ref, dka_ref, dva_ref):

        @pl.when(i == 0)
        def _():
            kp_ref[0:CA_PAD, :] = jnp.zeros((CA_PAD, PAIR), BF16)
            vp_ref[0:CA_PAD, :] = jnp.zeros((CA_PAD, PAIR), BF16)
            kp_ref[CA_PAD:, :] = k_ref[...]
            vp_ref[CA_PAD:, :] = v_ref[...]
            dka_ref[...] = jnp.zeros_like(dka_ref)
            dva_ref[...] = jnp.zeros_like(dva_ref)
            db_ref[...] = jnp.zeros_like(db_ref)

        lane = lax.broadcasted_iota(jnp.int32, (CA_SUB, PAIR), 1)
        hms = [lane < HEAD_DIM, lane >= HEAD_DIM]
        subs = range(tq // CA_SUB)
        rows = [slice(r * CA_SUB, (r + 1) * CA_SUB) for r in subs]
        cols = [slice(r * CA_SUB, r * CA_SUB + CA_SUBWIN) for r in subs]
        offs = [pl.multiple_of(i * tq + r * CA_SUB, CA_SUB) for r in subs]
        kws = [kp_ref[pl.ds(off, CA_SUBWIN), :] for off in offs]
        vws = [vp_ref[pl.ds(off, CA_SUBWIN), :] for off in offs]
        dovs = [do_ref[rows[r], :] for r in subs]
        prods = [dovs[r].astype(F32) * o_ref[rows[r], :].astype(F32) for r in subs]
        chains = [(r, h) for r in subs for h in range(2)]
        qhs = [jnp.where(hms[h], q_ref[rows[r], :], 0) * QK_SCALE for r, h in chains]
        dohs = [jnp.where(hms[h], dovs[r], 0) for r, h in chains]
        deltas = [jnp.sum(jnp.where(hms[h], prods[r], 0.0), axis=1, keepdims=True) for r, h in chains]
        scs = [_ca_logits(qhs[c], kws[r], b_ref[h, rows[r], cols[r]], CA_PAD - i * tq - r * CA_SUB)
               for c, (r, h) in enumerate(chains)]
        dps = [_nt(dohs[c], vws[r]) for c, (r, h) in enumerate(chains)]
        es = [jnp.exp(sc - jnp.max(sc, axis=1, keepdims=True)) for sc in scs]
        ps = [e / jnp.sum(e, axis=1, keepdims=True) for e in es]
        dss = [ps[c] * (dps[c] - deltas[c]) for c in range(len(chains))]
        for c, (r, h) in enumerate(chains):
            db_ref[h, rows[r], cols[r]] += dss[c]
        dsbs = [ds.astype(BF16) for ds in dss]
        pbs = [p.astype(BF16) for p in ps]
        dqs = [_nn(dsbs[c], kws[r]) for c, (r, h) in enumerate(chains)]
        for r in subs:
            dq_ref[rows[r], :] = (jnp.where(lane < HEAD_DIM, dqs[2 * r], dqs[2 * r + 1]) * QK_SCALE).astype(dq_ref.dtype)
        for r in subs:
            dka_ref[pl.ds(offs[r], CA_SUBWIN), :] += _tn(dsbs[2 * r], qhs[2 * r]) + _tn(dsbs[2 * r + 1], qhs[2 * r + 1])
            dva_ref[pl.ds(offs[r], CA_SUBWIN), :] += _tn(pbs[2 * r], dohs[2 * r]) + _tn(pbs[2 * r + 1], dohs[2 * r + 1])

        @pl.when(i == nq - 1)
        def _():
            dk_ref[...] = dka_ref[CA_PAD:, :].astype(dk_ref.dtype)
            dv_ref[...] = dva_ref[CA_PAD:, :].astype(dv_ref.dtype)

    qblk = pl.BlockSpec((Q_SUBS * tq, PAIR), lambda p, i: (i, p))
    full = pl.BlockSpec((s, PAIR), lambda p, i: (0, p))
    tblk = pl.BlockSpec((2, tq, CA_WIN), lambda p, i: (p, 0, 0))
    out = jax.ShapeDtypeStruct((s, D_MODEL), BF16)

    def body(q_ref, k_ref, v_ref, o_ref, do_ref, b_ref, dq_ref, dk_ref, dv_ref, db_ref, *scratch):
        for sub in range(Q_SUBS):
            rows = pl.ds(sub * tq, tq)
            block(pl.program_id(1) * Q_SUBS + sub, q_ref.at[rows], k_ref, v_ref, o_ref.at[rows], do_ref.at[rows], b_ref,
                  dq_ref.at[rows], dk_ref, dv_ref, db_ref, *scratch)

    def first_last():
        p, i = pl.program_id(0), pl.program_id(1)
        return (p == 0) & (i == 0), (p == N_PAIRS - 1) & (i == nq // Q_SUBS - 1)

    return hosted_call(
        body, first_last, rider, name=name, grid=(N_PAIRS, nq // Q_SUBS),
        in_specs=[
            qblk,
            pl.BlockSpec((s, PAIR), lambda p, i: (0, N_PAIRS + p)),
            pl.BlockSpec((s, PAIR), lambda p, i: (0, 2 * N_PAIRS + p)),
            qblk, qblk, tblk,
        ],
        out_specs=[qblk, full, full, tblk],
        out_shape=[out, out, out, jax.ShapeDtypeStruct((N_HEADS, tq, CA_WIN), F32)],
        scratch_shapes=[pltpu.VMEM((s + CA_PAD, PAIR), BF16), pltpu.VMEM((s + CA_PAD, PAIR), BF16),
                        pltpu.VMEM((s + CA_PAD, PAIR), F32), pltpu.VMEM((s + CA_PAD, PAIR), F32)],
        args=(qkn, qkn, qkv, o, do, table),
        compiler_params=_params(("arbitrary", "arbitrary")),
    )


def _me():
    x, y, c = lax.axis_index("x"), lax.axis_index("y"), lax.axis_index("c")
    return x, y, c, 4 * x + 2 * y + c


def _peer(k):
    x, y, c, _ = _me()
    px, py, pc = x ^ ((k >> 2) & 1), y ^ ((k >> 1) & 1), c ^ (k & 1)
    return (px, py, pc), 4 * px + 2 * py + pc


ANY = pl.BlockSpec(memory_space=pl.ANY)


class Exchange:
    def __init__(self, n, copy, in_arrays, out_shape):
        self.n, self.copy, self.in_arrays, self.out_shape = n, copy, list(in_arrays), list(out_shape)
        self.sems = [pltpu.SemaphoreType.DMA((n, N_DEV - 1)), pltpu.SemaphoreType.DMA((n, N_DEV - 1)),
                     pltpu.SemaphoreType.DMA((n,))]

    def _copies(self, data, sems):
        send_sems, recv_sems, local_sems = sems
        _, _, _, me = _me()
        local, sends, recvs = [], [], []
        for a in range(self.n):
            s_ref, d_ref = self.copy(data, a, me, me)
            local.append(pltpu.make_async_copy(s_ref, d_ref, local_sems.at[a]))
            for k in range(1, N_DEV):
                peer, pidx = _peer(k)
                s_ref, d_ref = self.copy(data, a, me, pidx)
                sends.append(pltpu.make_async_remote_copy(
                    src_ref=s_ref, dst_ref=d_ref, send_sem=send_sems.at[a, k - 1], recv_sem=recv_sems.at[a, k - 1],
                    device_id=peer, device_id_type=MESH))
                s_ref, d_ref = self.copy(data, a, pidx, me)
                recvs.append(pltpu.make_async_remote_copy(
                    src_ref=s_ref, dst_ref=d_ref, send_sem=send_sems.at[a, k - 1], recv_sem=recv_sems.at[a, k - 1],
                    device_id=peer, device_id_type=MESH))
        return local, sends, recvs

    def start(self, data, sems):
        local, sends, _ = self._copies(data, sems)
        for cp in local + sends:
            cp.start()

    def finish(self, data, sems):
        local, sends, recvs = self._copies(data, sems)
        for cp in recvs:
            cp.wait_recv()
        for cp in sends:
            cp.wait_send()
        for cp in local:
            cp.wait()


def run_exchange(ex, name):
    n_data = len(ex.in_arrays) + len(ex.out_shape)

    def body(*refs):
        ex.start(refs[:n_data], refs[n_data:])
        ex.finish(refs[:n_data], refs[n_data:])

    return _call(
        body, name=name,
        in_specs=[ANY] * len(ex.in_arrays), out_specs=[ANY] * len(ex.out_shape), out_shape=ex.out_shape,
        scratch_shapes=ex.sems, compiler_params=pltpu.CompilerParams(has_side_effects=True),
    )(*ex.in_arrays)


def hosted_call(core_body, first_last, rider, *, in_specs, out_specs, out_shape, scratch_shapes, args, **kw):
    if rider is None:
        return _call(core_body, in_specs=in_specs, out_specs=out_specs, out_shape=out_shape,
                     scratch_shapes=scratch_shapes, **kw)(*args), []
    n_in, n_out, n_scr = len(in_specs), len(out_specs), len(scratch_shapes)
    r_in, r_out = len(rider.in_arrays), len(rider.out_shape)

    def body(*refs):
        it = iter(refs)
        take = lambda m: [next(it) for _ in range(m)]
        c_in, x_in, c_out, x_out, c_scr, sems = take(n_in), take(r_in), take(n_out), take(r_out), take(n_scr), take(3)
        first, last = first_last()

        @pl.when(first)
        def _():
            rider.start(x_in + x_out, sems)

        core_body(*c_in, *c_out, *c_scr)

        @pl.when(last)
        def _():
            rider.finish(x_in + x_out, sems)

    outs = _call(
        body, in_specs=list(in_specs) + [ANY] * r_in, out_specs=list(out_specs) + [ANY] * r_out,
        out_shape=list(out_shape) + rider.out_shape, scratch_shapes=list(scratch_shapes) + rider.sems, **kw,
    )(*args, *rider.in_arrays)
    return outs[:n_out], outs[n_out:]


def _window(full_ref, shard_shape, idx):
    a, b = shard_shape
    if a == full_ref.shape[0]:
        return full_ref.at[:, pl.ds(pl.multiple_of(idx * b, 128), b)]
    return full_ref.at[pl.ds(pl.multiple_of(idx * a, 8), a), :]


def gather_exchange(shards, full_shapes, keys):
    nw = len(shards)

    def copy(data, a, src_idx, dst_idx):
        w, layer = keys[a]
        s_ref = data[w].at[layer]
        return s_ref, _window(data[nw + a], s_ref.shape, src_idx)

    out_shape = [jax.ShapeDtypeStruct(full_shapes[w], shards[w].dtype) for w, _ in keys]
    return Exchange(len(keys), copy, shards, out_shape)


def scatter_exchange(partials, shard_shapes):
    n = len(partials)

    def copy(data, a, src_idx, dst_idx):
        return _window(data[a], shard_shapes[a], dst_idx), data[n + a].at[src_idx]

    out_shape = [jax.ShapeDtypeStruct((N_DEV,) + tuple(ss), p.dtype) for ss, p in zip(shard_shapes, partials)]
    return Exchange(n, copy, partials, out_shape)


def _adamw(w, g, m, v):
    m = ADAM_B1 * m + (1.0 - ADAM_B1) * g
    v = ADAM_B2 * v + (1.0 - ADAM_B2) * (g * g)
    m_hat = m / (1.0 - ADAM_B1 ** ADAM_STEP)
    v_hat = v / (1.0 - ADAM_B2 ** ADAM_STEP)
    delta = -ADAM_LR * (m_hat / (jnp.sqrt(v_hat) + ADAM_EPS) + ADAM_WD * w)
    return delta, m, v


def adam_shard(parts, w, m, v, name):
    _, r, c = w.shape
    tr = min(r, 256)
    nr = r // tr

    def body(*refs):
        p_refs = refs[:DEPTH]
        w_ref, m_ref, v_ref, g_out, d_out, m_out, v_out = refs[DEPTH:]
        layer = pl.program_id(0)
        for li in range(DEPTH):
            @pl.when(layer == li)
            def _(p_ref=p_refs[li]):
                g = p_ref[0].astype(F32)
                for e in range(1, N_DEV):
                    g = g + p_ref[e].astype(F32)
                delta, mn, vn = _adamw(w_ref[...], g, m_ref[...], v_ref[...])
                g_out[...] = g
                d_out[...] = delta
                m_out[...] = mn
                v_out[...] = vn

    def part_spec(li):
        return pl.BlockSpec((N_DEV, tr, c), lambda layer, i: (0, jnp.where(layer == li, i, 0), 0))

    blk = pl.BlockSpec((None, tr, c), lambda layer, i: (layer, i, 0))
    out = jax.ShapeDtypeStruct(w.shape, F32)
    return _call(
        body, name=name, grid=(DEPTH, nr),
        in_specs=[part_spec(li) for li in range(DEPTH)] + [blk, blk, blk],
        out_specs=[blk] * 4, out_shape=[out] * 4,
        compiler_params=_params(("arbitrary", "arbitrary")),
    )(*parts, w, m, v)


def small_allreduce_adam(pk, w, m, v, name):
    def body(pk_ref, w_ref, m_ref, v_ref, g_out, d_out, m_out, v_out, all_ref, send_sems, recv_sems):
        _, _, _, me = _me()
        all_ref[me] = pk_ref[...]
        sends = []
        for k in range(1, N_DEV):
            peer, _ = _peer(k)
            cp = pltpu.make_async_remote_copy(
                src_ref=pk_ref, dst_ref=all_ref.at[me], send_sem=send_sems.at[k - 1],
                recv_sem=recv_sems.at[k - 1], device_id=peer, device_id_type=MESH)
            cp.start()
            sends.append(cp)
        for k in range(1, N_DEV):
            peer, pidx = _peer(k)
            pltpu.make_async_remote_copy(
                src_ref=pk_ref, dst_ref=all_ref.at[pidx], send_sem=send_sems.at[k - 1],
                recv_sem=recv_sems.at[k - 1], device_id=peer, device_id_type=MESH).wait_recv()
        for cp in sends:
            cp.wait_send()
        g = all_ref[0]
        for e in range(1, N_DEV):
            g = g + all_ref[e]
        delta, mn, vn = _adamw(w_ref[...], g, m_ref[...], v_ref[...])
        g_out[...] = g
        d_out[...] = delta
        m_out[...] = mn
        v_out[...] = vn

    vm = pl.BlockSpec(memory_space=pltpu.VMEM)
    out = jax.ShapeDtypeStruct((PACK_ROWS, 128), F32)
    return _call(
        body, name=name,
        in_specs=[vm] * 4, out_specs=[vm] * 4, out_shape=[out] * 4,
        scratch_shapes=[pltpu.VMEM((N_DEV, PACK_ROWS, 128), F32), pltpu.SemaphoreType.DMA((N_DEV - 1,)),
                        pltpu.SemaphoreType.DMA((N_DEV - 1,))],
        compiler_params=pltpu.CompilerParams(has_side_effects=True),
    )(pk, w, m, v)


def _pack_small(mix, ffn, qn, kn, rel, loss):
    flat = jnp.concatenate([mix.reshape(-1), ffn.reshape(-1), qn.reshape(-1), kn.reshape(-1), rel.reshape(-1),
                            loss.reshape(-1)])
    return jnp.pad(flat, (0, PACK_LEN - flat.shape[0])).reshape(PACK_ROWS, 128)


def _unpack_small(pk):
    flat = pk.reshape(-1)
    return (flat[OFF_MIX:OFF_FFN].reshape(DEPTH, D_MODEL), flat[OFF_FFN:OFF_QN].reshape(DEPTH, D_MODEL),
            flat[OFF_QN:OFF_KN].reshape(2, HEAD_DIM), flat[OFF_KN:OFF_REL].reshape(2, HEAD_DIM),
            flat[OFF_REL:OFF_LOSS].reshape(2, N_HEADS, N_REL), flat[OFF_LOSS])


def _pair_gain(g):
    return jnp.concatenate([g, g]).reshape(1, PAIR)


def kernel(x, mix_norm, w_qkv, w_o, q_norm, k_norm, rel_bias, ffn_norm, w_up, w_down, loss_target, m_mix_norm, m_w_qkv, m_w_o, m_q_norm, m_k_norm, m_rel_bias, m_ffn_norm, m_w_up, m_w_down, v_mix_norm, v_w_qkv, v_w_o, v_q_norm, v_k_norm, v_rel_bias, v_ffn_norm, v_w_up, v_w_down):
    x0 = x[0]
    target = loss_target[0]
    shard_shapes = [w_qkv.shape[1:], w_o.shape[1:], w_up.shape[1:], w_down.shape[1:]]
    full_shapes = [(D_MODEL, 3 * D_MODEL), (D_MODEL, D_MODEL), (D_MODEL, D_FF), (D_FF, D_MODEL)]
    shards = [w_qkv.astype(BF16), w_o.astype(BF16), w_up.astype(BF16), w_down.astype(BF16)]
    nw = len(shards)
    weight = {}

    def fetch(keys, host, *args):
        outs, arrived = host(*args, rider=gather_exchange(shards, full_shapes, keys))
        weight.update(zip(keys, arrived))
        return outs

    first = [(0, 0)]
    weight.update(zip(first, run_exchange(gather_exchange(shards, full_shapes, first), "gather_first")))
    riding = {0: [(w, 0) for w in range(1, nw)] + [(w, 1) for w in range(nw)],
              1: [(w, 2) for w in range(nw)], 2: [(w, 3) for w in range(nw)]}
    tri = _tri_mats()

    saved = []
    xin = x0
    h0 = rms_fwd(x0, mix_norm[0:1], "rms_mix_0")
    for layer in range(DEPTH):
        mixer_b = layer % 2 == 1
        idx = layer // 2
        qkv_raw = dense(h0, weight[0, layer], nt=False, name=f"qkv_{layer}", out_dtype=BF16, tm=1024)
        if mixer_b:
            gq, gk = _pair_gain(q_norm[idx]), _pair_gain(k_norm[idx])
            qkn = headnorm_fwd(qkv_raw, gq, gk, f"headnorm_{layer}")
            rel_pad = jnp.pad(rel_bias[idx], ((0, 0), (0, N_REL_PAD - N_REL)))
            table = bias_table(rel_pad, f"bias_table_{layer}")
            if layer in riding:
                (o,) = fetch(riding[layer], ca_fwd, qkn, qkv_raw, table, f"ca_fwd_{layer}")
            else:
                (o,), _ = ca_fwd(qkn, qkv_raw, table, f"ca_fwd_{layer}")
            attn_saved = (qkv_raw, qkn, table)
        else:
            o, ltot = fetch(riding[layer], sb_fwd, qkv_raw, tri, f"sb_fwd_{layer}")
            attn_saved = (qkv_raw, ltot)
        x1, h1 = dense(o, weight[1, layer], nt=False, name=f"attn_out_{layer}", out_dtype=F32, tm=1024, res=xin,
                       norm_gain=ffn_norm[layer:layer + 1])
        pre = dense(h1, weight[2, layer], nt=False, name=f"up_{layer}", out_dtype=BF16, tm=1024)
        saved.append((xin, h0, attn_saved, o, x1, h1, pre))
        if layer + 1 < DEPTH:
            xin, h0 = dense(pre, weight[3, layer], nt=False, name=f"down_{layer}", out_dtype=F32, tm=512, relu2_in=True,
                            res=x1, norm_gain=mix_norm[layer + 1:layer + 2])
        else:
            xin = dense(pre, weight[3, layer], nt=False, name=f"down_{layer}", out_dtype=F32, tm=512, relu2_in=True,
                        res=x1)

    dx, loss_blk = loss_head(xin, target, "loss_head")

    partial = {}
    received = {}
    d_mix, d_ffn = [None] * DEPTH, [None] * DEPTH
    d_qn, d_kn, d_rel = [None] * 2, [None] * 2, [None] * 2

    def leaving():
        keys = [key for key in sorted(partial) if key not in received]
        return keys, scatter_exchange([partial[key] for key in keys], [shard_shapes[key[0]] for key in keys])

    def send(host, *args, **kw):
        keys, ex = leaving()
        outs, arrived = host(*args, rider=ex, **kw)
        received.update(zip(keys, arrived))
        return outs

    for layer in reversed(range(DEPTH)):
        mixer_b = layer % 2 == 1
        idx = layer // 2
        xin, h0, attn_saved, o, x1, h1, pre = saved[layer]
        g_qkv, g_o, g_up, g_down = (weight[w, layer] for w in range(nw))
        dpre = dense(dx, g_down, nt=True, name=f"d_pre_{layer}", out_dtype=BF16, tm=512, relu2_grad=pre)
        partial[3, layer] = dense_tn(pre, dx, tk=2048, bn=1024, name=f"dw_down_{layer}", relu2_in=True)
        partial[2, layer] = dense_tn(h1, dpre, tk=1024, bn=2048, name=f"dw_up_{layer}")
        dx, d_ffn[layer] = dense(dpre, g_up, nt=True, name=f"d_h1_{layer}", out_dtype=F32, tm=512,
                                 rms_back=(x1, ffn_norm[layer:layer + 1], dx))
        do = dense(dx, g_o, nt=True, name=f"d_o_{layer}", out_dtype=BF16, tm=1024)
        partial[1, layer] = dense_tn(o, dx, tk=1024, bn=1024, name=f"dw_o_{layer}")
        if mixer_b:
            qkv_raw, qkn, table = attn_saved
            dq, dk, dv, dtab = send(ca_bwd, qkn, qkv_raw, o, do, table, f"ca_bwd_{layer}")
            gq, gk = _pair_gain(q_norm[idx]), _pair_gain(k_norm[idx])
            dqkv, dgain = headnorm_bwd(dq, dk, dv, qkv_raw, gq, gk, f"headnorm_bwd_{layer}")
            d_qn[idx] = dgain[0, :HEAD_DIM] + dgain[0, HEAD_DIM:]
            d_kn[idx] = dgain[1, :HEAD_DIM] + dgain[1, HEAD_DIM:]
            d_rel[idx] = bias_fold(dtab, f"bias_fold_{layer}")[:, :N_REL]
        else:
            qkv_raw, ltot = attn_saved
            dq, dk, dv = send(sb_bwd, qkv_raw, do, ltot, tri, f"sb_bwd_{layer}")
            dqkv = jnp.concatenate([dq, dk, dv], axis=1)
        partial[0, layer] = dense_tn(h0, dqkv, tk=1024, bn=1536, name=f"dw_qkv_{layer}")
        back = dict(nt=True, name=f"d_h0_{layer}", out_dtype=F32, tm=512, rms_back=(xin, mix_norm[layer:layer + 1], dx))
        if layer == 0:
            dx, dg = send(dense, dqkv, g_qkv, **back)
            d_mix[layer] = dg[0:1]
        else:
            dx, d_mix[layer] = dense(dqkv, g_qkv, **back)

    grad_x = dx[None]


    def big(wi, w, m, v, name):
        return adam_shard([received[wi, layer] for layer in range(DEPTH)], w, m, v, name)

    a_qkv = big(0, w_qkv, m_w_qkv, v_w_qkv, "adam_qkv")
    a_o = big(1, w_o, m_w_o, v_w_o, "adam_o")
    a_up = big(2, w_up, m_w_up, v_w_up, "adam_up")
    a_down = big(3, w_down, m_w_down, v_w_down, "adam_down")

    pk = _pack_small(jnp.concatenate(d_mix), jnp.concatenate(d_ffn), jnp.stack(d_qn), jnp.stack(d_kn),
                     jnp.stack(d_rel), loss_blk[0, 0])
    zero = jnp.zeros((), F32)
    pw = _pack_small(mix_norm, ffn_norm, q_norm, k_norm, rel_bias, zero)
    pm = _pack_small(m_mix_norm, m_ffn_norm, m_q_norm, m_k_norm, m_rel_bias, zero)
    pv = _pack_small(v_mix_norm, v_ffn_norm, v_q_norm, v_k_norm, v_rel_bias, zero)
    s_g, s_d, s_m, s_v = small_allreduce_adam(pk, pw, pm, pv, "small_allreduce_adam")
    g_mix, g_ffn, g_qn, g_kn, g_rel, loss = _unpack_small(s_g)
    dl_mix, dl_ffn, dl_qn, dl_kn, dl_rel, _ = _unpack_small(s_d)
    nm_mix, nm_ffn, nm_qn, nm_kn, nm_rel, _ = _unpack_small(s_m)
    nv_mix, nv_ffn, nv_qn, nv_kn, nv_rel, _ = _unpack_small(s_v)

    return (loss, grad_x,
            g_mix, a_qkv[0], a_o[0], g_qn, g_kn, g_rel, g_ffn, a_up[0], a_down[0],
            dl_mix, a_qkv[1], a_o[1], dl_qn, dl_kn, dl_rel, dl_ffn, a_up[1], a_down[1],
            nm_mix, a_qkv[2], a_o[2], nm_qn, nm_kn, nm_rel, nm_ffn, a_up[2], a_down[2],
            nv_mix, a_qkv[3], a_o[3], nv_qn, nv_kn, nv_rel, nv_ffn, a_up[3], a_down[3])
```

```python
import jax
import jax.numpy as jnp
from jax import lax
from jax.experimental import pallas as pl
from jax.experimental.pallas import tpu as pltpu

F32 = jnp.float32
BF16 = jnp.bfloat16
MESH = pl.DeviceIdType.MESH

D_MODEL = 1024
N_HEADS = 16
HEAD_DIM = 64
PAIR = 2 * HEAD_DIM
N_PAIRS = N_HEADS // 2
D_FF = 4 * D_MODEL
DEPTH = 4
N_DEV = 8
RMS_EPS = 1e-6
QK_SCALE = HEAD_DIM ** -0.5

SB_TILE = 256
SB_DEAD = 104.0
Q_SUBS = 4

CHUNK = 64
LEFT_CHUNKS = 8
CA_QBLK = 256
CA_PAD = LEFT_CHUNKS * CHUNK
CA_WIN = CA_QBLK + CA_PAD
CA_SUB = 128
CA_SUBWIN = CA_SUB + CA_PAD
CA_RING = 1024
MAX_REL = 256
N_REL = 2 * MAX_REL + 1
N_REL_PAD = 640
NEG_BIG = -1e30

ADAM_LR = 0.001
ADAM_B1 = 0.9
ADAM_B2 = 0.999
ADAM_EPS = 1e-08
ADAM_WD = 0.01
ADAM_STEP = 10

ROW_TILE = 512
DENSE_CHUNK = 512
TN_ROWS = 1024
VMEM_LIMIT = 56 * 1024 * 1024

OFF_MIX = 0
OFF_FFN = OFF_MIX + DEPTH * D_MODEL
OFF_QN = OFF_FFN + DEPTH * D_MODEL
OFF_KN = OFF_QN + 2 * HEAD_DIM
OFF_REL = OFF_KN + 2 * HEAD_DIM
OFF_LOSS = OFF_REL + 2 * N_HEADS * N_REL
PACK_ROWS = 200
PACK_LEN = PACK_ROWS * 128


def _call(body, **kw):
    return pl.pallas_call(body, **kw)


def _params(sem=None, vmem=VMEM_LIMIT):
    if sem is None:
        return pltpu.CompilerParams(vmem_limit_bytes=vmem)
    return pltpu.CompilerParams(dimension_semantics=sem, vmem_limit_bytes=vmem)


def _nt(a, b):
    return lax.dot_general(a, b, (((1,), (1,)), ((), ())), preferred_element_type=F32)


def _tn(a, b):
    return lax.dot_general(a, b, (((0,), (0,)), ((), ())), preferred_element_type=F32)


def _nn(a, b):
    return jnp.dot(a, b, preferred_element_type=F32)


def _split_bf16(v):
    hi = v.astype(BF16)
    lo = (v - hi.astype(F32)).astype(BF16)
    return hi, lo


def rms_fwd(x, g, name):
    s = x.shape[0]
    tm = ROW_TILE

    def body(x_ref, g_ref, o_ref):
        xv = x_ref[...]
        r = lax.rsqrt(jnp.mean(xv * xv, axis=-1, keepdims=True) + RMS_EPS)
        o_ref[...] = (xv * r * g_ref[...]).astype(o_ref.dtype)

    return _call(
        body, name=name, grid=(s // tm,),
        in_specs=[pl.BlockSpec((tm, D_MODEL), lambda i: (i, 0)), pl.BlockSpec((1, D_MODEL), lambda i: (0, 0))],
        out_specs=pl.BlockSpec((tm, D_MODEL), lambda i: (i, 0)),
        out_shape=jax.ShapeDtypeStruct((s, D_MODEL), BF16),
        compiler_params=_params(("parallel",)),
    )(x, g)


def loss_head(y, target, name):
    s = y.shape[0]
    tm = ROW_TILE

    def body(y_ref, t_ref, dy_ref, l_ref):
        i = pl.program_id(0)
        e = y_ref[...] - t_ref[...]
        dy_ref[...] = e * (1.0 / D_MODEL)

        @pl.when(i == 0)
        def _():
            l_ref[...] = jnp.zeros_like(l_ref)

        per_row = jnp.sum(e * e, axis=-1, keepdims=True) * (1.0 / D_MODEL)
        l_ref[...] += jnp.broadcast_to(0.5 * jnp.sum(per_row, axis=0, keepdims=True), l_ref.shape)

    row = pl.BlockSpec((tm, D_MODEL), lambda i: (i, 0))
    return _call(
        body, name=name, grid=(s // tm,),
        in_specs=[row, row],
        out_specs=[row, pl.BlockSpec((8, 128), lambda i: (0, 0))],
        out_shape=[jax.ShapeDtypeStruct((s, D_MODEL), F32), jax.ShapeDtypeStruct((8, 128), F32)],
        compiler_params=_params(("arbitrary",)),
    )(y, target)


def _group_sum_matrix():
    r = lax.broadcasted_iota(jnp.int32, (PAIR, PAIR), 0) // HEAD_DIM
    c = lax.broadcasted_iota(jnp.int32, (PAIR, PAIR), 1) // HEAD_DIM
    return (r == c).astype(BF16)


def _head_mean(v, gmat):
    hi, lo = _split_bf16(v)
    return (_nn(hi, gmat) + _nn(lo, gmat)) * (1.0 / HEAD_DIM)


def headnorm_fwd(qkv, gq, gk, name):
    s = qkv.shape[0]
    tm = ROW_TILE

    def body(x_ref, gq_ref, gk_ref, o_ref):
        j = pl.program_id(1)
        gmat = _group_sum_matrix()
        gain = jnp.where(j == 0, gq_ref[...], gk_ref[...])
        for b in range(N_PAIRS):
            cols = slice(b * PAIR, (b + 1) * PAIR)
            xv = x_ref[:, cols].astype(F32)
            r = lax.rsqrt(_head_mean(xv * xv, gmat) + RMS_EPS)
            o_ref[:, cols] = (xv * r * gain).astype(o_ref.dtype)

    blk = pl.BlockSpec((tm, D_MODEL), lambda i, j: (i, j))
    gspec = pl.BlockSpec((1, PAIR), lambda i, j: (0, 0))
    return _call(
        body, name=name, grid=(s // tm, 2),
        in_specs=[blk, gspec, gspec], out_specs=blk,
        out_shape=jax.ShapeDtypeStruct((s, 2 * D_MODEL), BF16),
        compiler_params=_params(("parallel", "parallel")),
    )(qkv, gq, gk)


def headnorm_bwd(dq, dk, dv, qkv, gq, gk, name):
    s = qkv.shape[0]
    tm = ROW_TILE

    def body(dq_ref, dk_ref, dv_ref, x_ref, gq_ref, gk_ref, dx_ref, dg_ref):
        i = pl.program_id(0)
        j = pl.program_id(1)

        @pl.when((i == 0) & (j == 0))
        def _():
            dg_ref[...] = jnp.zeros_like(dg_ref)

        @pl.when(j == 2)
        def _():
            dx_ref[...] = dv_ref[...]

        for part, (d_ref, g_ref) in enumerate(((dq_ref, gq_ref), (dk_ref, gk_ref))):
            @pl.when(j == part)
            def _(part=part, d_ref=d_ref, g_ref=g_ref):
                gmat = _group_sum_matrix()
                gain = g_ref[...]
                dg = jnp.zeros((1, PAIR), F32)
                for b in range(N_PAIRS):
                    cols = slice(b * PAIR, (b + 1) * PAIR)
                    xv = x_ref[:, cols].astype(F32)
                    dn = d_ref[:, cols].astype(F32)
                    r = lax.rsqrt(_head_mean(xv * xv, gmat) + RMS_EPS)
                    xh = xv * r
                    dy = dn * gain
                    dx_ref[:, cols] = (r * (dy - xh * _head_mean(dy * xh, gmat))).astype(dx_ref.dtype)
                    dg = dg + jnp.sum(dn * xh, axis=0, keepdims=True)
                dg_ref[part:part + 1, :] += dg

    row = pl.BlockSpec((tm, D_MODEL), lambda i, j: (i, 0))
    blk = pl.BlockSpec((tm, D_MODEL), lambda i, j: (i, j))
    gspec = pl.BlockSpec((1, PAIR), lambda i, j: (0, 0))
    dx, dg = _call(
        body, name=name, grid=(s // tm, 3),
        in_specs=[row, row, row, blk, gspec, gspec],
        out_specs=[blk, pl.BlockSpec((8, PAIR), lambda i, j: (0, 0))],
        out_shape=[jax.ShapeDtypeStruct(qkv.shape, BF16), jax.ShapeDtypeStruct((8, PAIR), F32)],
        compiler_params=_params(("arbitrary", "arbitrary")),
    )(dq, dk, dv, qkv, gq, gk)
    return dx, dg[0:2]


def _relu2(a):
    r = jnp.maximum(a.astype(F32), 0.0)
    return r * r


def dense(a, w, *, nt, name, out_dtype, tm, relu2_in=False, relu2_grad=None, res=None, norm_gain=None,
          rms_back=None, rider=None):
    s, k = a.shape
    n = w.shape[0] if nt else w.shape[1]
    nc = min(n, DENSE_CHUNK)

    def body(*refs):
        it = iter(refs)
        a_ref, w_ref = next(it), next(it)
        g_ref = next(it) if relu2_grad is not None else None
        r_ref = next(it) if res is not None else None
        ng_ref = next(it) if norm_gain is not None else None
        if rms_back is not None:
            x_ref, gain_ref, dres_ref = next(it), next(it), next(it)
        o_ref = next(it)
        h_ref = next(it) if norm_gain is not None else None
        dg_ref = next(it) if rms_back is not None else None
        av = a_ref[...]
        av = _relu2(av).astype(BF16) if relu2_in else av.astype(BF16)
        for c0 in range(0, n, nc):
            cols = slice(c0, c0 + nc)
            acc = _nt(av, w_ref[cols, :]) if nt else _nn(av, w_ref[:, cols])
            if g_ref is not None:
                acc = acc * (2.0 * jnp.maximum(g_ref[:, cols].astype(F32), 0.0))
            if r_ref is not None:
                acc = acc + r_ref[:, cols]
            o_ref[:, cols] = acc.astype(o_ref.dtype)
        if norm_gain is not None:
            xv = o_ref[...]
            r = lax.rsqrt(jnp.mean(xv * xv, axis=-1, keepdims=True) + RMS_EPS)
            h_ref[...] = (xv * r * ng_ref[...]).astype(h_ref.dtype)
        if rms_back is not None:
            dhv = o_ref[...]
            xv = x_ref[...]
            r = lax.rsqrt(jnp.mean(xv * xv, axis=-1, keepdims=True) + RMS_EPS)
            xh = xv * r
            dy = dhv * gain_ref[...]
            mdot = jnp.mean(dy * xh, axis=-1, keepdims=True)
            o_ref[...] = dres_ref[...] + r * (dy - xh * mdot)

            @pl.when(pl.program_id(0) == 0)
            def _():
                dg_ref[...] = jnp.zeros_like(dg_ref)

            dg_ref[0:1, :] += jnp.sum(dhv * xh, axis=0, keepdims=True)

    oblk = pl.BlockSpec((tm, n), lambda i: (i, 0))
    gblk = pl.BlockSpec((1, n), lambda i: (0, 0))
    in_specs = [pl.BlockSpec((tm, k), lambda i: (i, 0)), pl.BlockSpec(w.shape, lambda i: (0, 0))]
    args = [a, w]
    for e in (relu2_grad, res):
        if e is not None:
            in_specs.append(oblk)
            args.append(e)
    out_specs, out_shape = [oblk], [jax.ShapeDtypeStruct((s, n), out_dtype)]
    if norm_gain is not None:
        in_specs.append(gblk)
        args.append(norm_gain)
        out_specs.append(oblk)
        out_shape.append(jax.ShapeDtypeStruct((s, n), BF16))
    if rms_back is not None:
        in_specs += [oblk, gblk, oblk]
        args += list(rms_back)
        out_specs.append(pl.BlockSpec((8, n), lambda i: (0, 0)))
        out_shape.append(jax.ShapeDtypeStruct((8, n), F32))
    def first_last():
        return pl.program_id(0) == 0, pl.program_id(0) == s // tm - 1

    outs, arrived = hosted_call(
        body, first_last, rider, name=name, grid=(s // tm,), in_specs=in_specs, out_specs=out_specs,
        out_shape=out_shape, scratch_shapes=[], args=args,
        compiler_params=_params(("arbitrary",) if rms_back is not None or rider is not None else ("parallel",)),
    )
    if rider is not None:
        return outs, arrived
    if rms_back is not None:
        return outs[0], outs[1][0:1]
    return outs if norm_gain is not None else outs[0]


def dense_tn(a, b, *, tk, bn, name, relu2_in=False):
    s, k = a.shape
    n = b.shape[1]
    ts = TN_ROWS
    ns = s // ts
    nc = min(bn, DENSE_CHUNK)

    def body(a_ref, b_ref, o_ref, acc_ref):
        t = pl.program_id(2)

        @pl.when(t == 0)
        def _():
            acc_ref[...] = jnp.zeros_like(acc_ref)

        av = a_ref[...]
        av = _relu2(av).astype(BF16) if relu2_in else av.astype(BF16)
        at = av.T
        for c0 in range(0, bn, nc):
            cols = slice(c0, c0 + nc)
            acc_ref[:, cols] += _nn(at, b_ref[:, cols].astype(BF16))

        @pl.when(t == ns - 1)
        def _():
            o_ref[...] = acc_ref[...].astype(o_ref.dtype)

    return _call(
        body, name=name, grid=(k // tk, n // bn, ns),
        in_specs=[pl.BlockSpec((ts, tk), lambda kb, j, t: (t, kb)), pl.BlockSpec((ts, bn), lambda kb, j, t: (t, j))],
        out_specs=pl.BlockSpec((tk, bn), lambda kb, j, t: (kb, j)),
        out_shape=jax.ShapeDtypeStruct((k, n), BF16),
        scratch_shapes=[pltpu.VMEM((tk, bn), F32)],
        compiler_params=_params(("parallel", "parallel", "arbitrary")),
    )(a, b)


def _softplus(z):
    return jnp.maximum(z, 0.0) + jnp.log(1.0 + jnp.exp(-jnp.abs(z)))


def _tri_mats():
    t = SB_TILE
    r = jnp.arange(2 * t)[:, None] % t
    c = jnp.arange(t)[None, :]
    return jnp.stack([(r > c), (r < c), (r <= c)]).astype(BF16)


def _split_sum(v, u2):
    hi, lo = _split_bf16(v)
    return _nn(jnp.concatenate([hi, lo], axis=1), u2)


def sb_fwd(qkv, tri, name, rider=None):
    s = qkv.shape[0]
    t = SB_TILE
    tb = Q_SUBS * t

    def block(i, q_ref, k_ref, v_ref, u_ref, o_ref, lt_ref, c0_ref, c1_ref, acc0_ref, acc1_ref):
        lane = lax.broadcasted_iota(jnp.int32, (t, PAIR), 1)
        causal = lax.broadcasted_iota(jnp.int32, (t, t), 1) < lax.broadcasted_iota(jnp.int32, (t, t), 0)
        qv = q_ref[...]
        u_after = u_ref[0]
        qhs = [jnp.where(lane < HEAD_DIM, qv, 0) * QK_SCALE, jnp.where(lane >= HEAD_DIM, qv, 0) * QK_SCALE]
        cs = [c0_ref, c1_ref]
        accs = [acc0_ref, acc1_ref]
        for r in cs + accs:
            r[...] = jnp.zeros_like(r)

        half = t // 2
        chains = [(h, slice(r * half, (r + 1) * half)) for h in range(2) for r in range(2)]

        def sweep(tiles):
            kvs = []
            for j, _ in tiles:
                off = pl.multiple_of(j * t, t)
                kvs.append((k_ref[pl.ds(off, t), :], v_ref[pl.ds(off, t), :]))
            work = [(k, h, rows) for k in range(len(tiles)) for h, rows in chains]
            zs = [_nt(qhs[h][rows, :], kvs[k][0]) for k, h, rows in work]
            ls = [-_softplus(z) for z in zs]
            ls = [jnp.where(causal[rows, :], l, 0.0) if tiles[k][1] else l for l, (k, h, rows) in zip(ls, work)]
            afters = [_split_sum(l, u_after) for l in ls]
            run = {}
            for w, (k, h, rows) in enumerate(work):
                c = run.get((h, rows.start), cs[h][rows, :])
                a = jnp.exp(zs[w] + ls[w] + afters[w] + c)
                if tiles[k][1]:
                    a = jnp.where(causal[rows, :], a, 0.0)
                av = _nn(a.astype(BF16), kvs[k][1])
                run[h, rows.start] = c + (afters[w][:, 0:1] + ls[w][:, 0:1])
                run["acc", h, rows.start] = run.get(("acc", h, rows.start), 0.0) + av
            for h, rows in chains:
                accs[h][rows, :] += run["acc", h, rows.start]
                cs[h][rows, :] = run[h, rows.start]

        def alive():
            return jnp.max(jnp.maximum(c0_ref[...], c1_ref[...])) > -SB_DEAD

        @pl.when(i == 0)
        def _():
            sweep([(i, True)])

        @pl.when(i > 0)
        def _():
            sweep([(i, True), (i - 1, False)])

        def cond(st):
            return (st[0] < i) & st[1]

        def step(st):
            sweep([(i - 1 - st[0], False)])
            return st[0] + 1, alive()

        swept, _ = lax.while_loop(cond, step, (jnp.minimum(i, 1), alive()))
        o_ref[...] = jnp.where(lane < HEAD_DIM, acc0_ref[...], acc1_ref[...]).astype(o_ref.dtype)
        lt_ref[...] = jnp.where(lane == 0, c0_ref[...], jnp.where(lane == 1, c1_ref[...],
                                jnp.where(lane == 2, swept.astype(F32), 0.0)))

    def body(q_ref, k_ref, v_ref, u_ref, o_ref, lt_ref, *scratch):
        for sub in range(Q_SUBS):
            rows = pl.ds(sub * t, t)
            block(pl.program_id(1) * Q_SUBS + sub, q_ref.at[rows], k_ref, v_ref, u_ref, o_ref.at[rows],
                  lt_ref.at[rows], *scratch)

    def first_last():
        p, i = pl.program_id(0), pl.program_id(1)
        return (p == 0) & (i == 0), (p == N_PAIRS - 1) & (i == s // tb - 1)

    return hosted_call(
        body, first_last, rider, name=name, grid=(N_PAIRS, s // tb),
        in_specs=[
            pl.BlockSpec((tb, PAIR), lambda p, i: (i, p)),
            pl.BlockSpec((s, PAIR), lambda p, i: (0, N_PAIRS + p)),
            pl.BlockSpec((s, PAIR), lambda p, i: (0, 2 * N_PAIRS + p)),
            pl.BlockSpec((3, 2 * t, t), lambda p, i: (0, 0, 0)),
        ],
        out_specs=[pl.BlockSpec((tb, PAIR), lambda p, i: (i, p)), pl.BlockSpec((None, tb, PAIR), lambda p, i: (p, i, 0))],
        out_shape=[jax.ShapeDtypeStruct((s, D_MODEL), BF16), jax.ShapeDtypeStruct((N_PAIRS, s, PAIR), F32)],
        scratch_shapes=[pltpu.VMEM((t, 1), F32), pltpu.VMEM((t, 1), F32), pltpu.VMEM((t, PAIR), F32),
                        pltpu.VMEM((t, PAIR), F32)],
        args=(qkv, qkv, qkv, tri),
        compiler_params=_params(("arbitrary", "arbitrary")),
    )


def sb_bwd(qkv, do, ltot, tri, name, rider=None):
    s = qkv.shape[0]
    t = SB_TILE
    nq = s // t

    def block(i, q_ref, k_ref, v_ref, do_ref, lt_ref, u_ref, dq_ref, dk_ref, dv_ref,
              pl0_ref, pl1_ref, pg0_ref, pg1_ref, dqa0_ref, dqa1_ref, dka_ref, dva_ref):

        @pl.when(i == 0)
        def _():
            dka_ref[...] = jnp.zeros_like(dka_ref)
            dva_ref[...] = jnp.zeros_like(dva_ref)

        lane = lax.broadcasted_iota(jnp.int32, (t, PAIR), 1)
        causal = lax.broadcasted_iota(jnp.int32, (t, t), 1) < lax.broadcasted_iota(jnp.int32, (t, t), 0)
        qv = q_ref[...]
        dov = do_ref[...]
        ltv = lt_ref[...]
        u_before = u_ref[1]
        u_upto = u_ref[2, 0:t, :]
        hms = [lane < HEAD_DIM, lane >= HEAD_DIM]
        qhs = [jnp.where(hm, qv, 0) * QK_SCALE for hm in hms]
        dohs = [jnp.where(hm, dov, 0) for hm in hms]
        ltots = [ltv[:, h:h + 1] for h in range(2)]
        pls = [pl0_ref, pl1_ref]
        pgs = [pg0_ref, pg1_ref]
        dqas = [dqa0_ref, dqa1_ref]
        for r in pls + pgs + dqas:
            r[...] = jnp.zeros_like(r)

        half = t // 2
        chains = [(h, slice(r * half, (r + 1) * half)) for h in range(2) for r in range(2)]

        def sweep(tiles):
            offs = [pl.multiple_of(j * t, t) for j, _ in tiles]
            kvs = [(k_ref[pl.ds(off, t), :], v_ref[pl.ds(off, t), :]) for off in offs]
            work = [(k, h, rows) for k in range(len(tiles)) for h, rows in chains]
            qcs = [qhs[h][rows, :] for k, h, rows in work]
            docs = [dohs[h][rows, :] for k, h, rows in work]
            zs = [_nt(qcs[w], kvs[k][0]) for w, (k, h, rows) in enumerate(work)]
            das = [_nt(docs[w], kvs[k][1]) for w, (k, h, rows) in enumerate(work)]
            sps = [_softplus(z) for z in zs]
            ls = [jnp.where(causal[rows, :], -sp, 0.0) if tiles[k][1] else -sp for sp, (k, h, rows) in zip(sps, work)]
            befores = [_split_sum(l, u_before) for l in ls]
            run = {}
            dzbs, abs_ = [], []
            for w, (k, h, rows) in enumerate(work):
                key = (h, rows.start)
                pl_c = run.get(("pl",) + key, pls[h][rows, :])
                pg_c = run.get(("pg",) + key, pgs[h][rows, :])
                a = jnp.exp(zs[w] + (ltots[h][rows, :] - pl_c - befores[w]))
                if tiles[k][1]:
                    a = jnp.where(causal[rows, :], a, 0.0)
                g = a * das[w]
                upto = _nn(g.astype(BF16), u_upto)
                dz = g - jnp.exp(zs[w] - sps[w]) * (pg_c + upto)
                if tiles[k][1]:
                    dz = jnp.where(causal[rows, :], dz, 0.0)
                dzbs.append(dz.astype(BF16))
                abs_.append(a.astype(BF16))
                run[("pl",) + key] = pl_c + (befores[w][:, t - 1:t] + ls[w][:, t - 1:t])
                run[("pg",) + key] = pg_c + upto[:, t - 1:t]
                run[("dq",) + key] = run.get(("dq",) + key, 0.0) + _nn(dzbs[w], kvs[k][0])
            for k in range(len(tiles)):
                mine = [w for w, wk in enumerate(work) if wk[0] == k]
                dka_ref[pl.ds(offs[k], t), :] += sum(_tn(dzbs[w], qcs[w]) for w in mine)
                dva_ref[pl.ds(offs[k], t), :] += sum(_tn(abs_[w], docs[w]) for w in mine)
            for h, rows in chains:
                key = (h, rows.start)
                dqas[h][rows, :] += run[("dq",) + key]
                pls[h][rows, :] = run[("pl",) + key]
                pgs[h][rows, :] = run[("pg",) + key]

        def step(j, carry):
            sweep([(j, False)])
            return carry

        swept = jnp.max(ltv[0:8, 2:3]).astype(jnp.int32)
        lax.fori_loop(i - swept, i, step, 0)
        sweep([(i, True)])

        dq_ref[...] = (jnp.where(lane < HEAD_DIM, dqa0_ref[...], dqa1_ref[...]) * QK_SCALE).astype(dq_ref.dtype)

        @pl.when(i == nq - 1)
        def _():
            dk_ref[...] = dka_ref[...].astype(dk_ref.dtype)
            dv_ref[...] = dva_ref[...].astype(dv_ref.dtype)

    qblk = pl.BlockSpec((Q_SUBS * t, PAIR), lambda p, i: (i, p))
    full = pl.BlockSpec((s, PAIR), lambda p, i: (0, p))
    out = jax.ShapeDtypeStruct((s, D_MODEL), BF16)

    def body(q_ref, k_ref, v_ref, do_ref, lt_ref, u_ref, dq_ref, dk_ref, dv_ref, *scratch):
        for sub in range(Q_SUBS):
            rows = pl.ds(sub * t, t)
            block(pl.program_id(1) * Q_SUBS + sub, q_ref.at[rows], k_ref, v_ref, do_ref.at[rows], lt_ref.at[rows],
                  u_ref, dq_ref.at[rows], dk_ref, dv_ref, *scratch)

    def first_last():
        p, i = pl.program_id(0), pl.program_id(1)
        return (p == 0) & (i == 0), (p == N_PAIRS - 1) & (i == nq // Q_SUBS - 1)

    return hosted_call(
        body, first_last, rider, name=name, grid=(N_PAIRS, nq // Q_SUBS),
        in_specs=[
            qblk,
            pl.BlockSpec((s, PAIR), lambda p, i: (0, N_PAIRS + p)),
            pl.BlockSpec((s, PAIR), lambda p, i: (0, 2 * N_PAIRS + p)),
            qblk,
            pl.BlockSpec((None, Q_SUBS * t, PAIR), lambda p, i: (p, i, 0)),
            pl.BlockSpec((3, 2 * t, t), lambda p, i: (0, 0, 0)),
        ],
        out_specs=[qblk, full, full],
        out_shape=[out, out, out],
        scratch_shapes=[pltpu.VMEM((t, 1), F32)] * 4 + [pltpu.VMEM((t, PAIR), F32)] * 2
        + [pltpu.VMEM((s, PAIR), F32), pltpu.VMEM((s, PAIR), F32)],
        args=(qkv, qkv, qkv, do, ltot, tri),
        compiler_params=_params(("arbitrary", "arbitrary")),
    )


def _ring_onehot():
    r = lax.broadcasted_iota(jnp.int32, (N_REL_PAD, CA_RING), 0)
    w = lax.broadcasted_iota(jnp.int32, (N_REL_PAD, CA_RING), 1)
    idx = jnp.where(w <= CA_WIN, jnp.clip(CA_WIN - w, 0, 2 * MAX_REL), 2 * MAX_REL)
    return (r == idx).astype(F32)


def bias_table(rel_pad, name):
    def body(rb_ref, o_ref):
        ring = jnp.dot(rb_ref[...], _ring_onehot(), preferred_element_type=F32, precision=lax.Precision.HIGHEST)
        col = lax.broadcasted_iota(jnp.int32, (N_HEADS, CA_WIN), 1)

        def row(tl, carry):
            v = pltpu.roll(ring, tl, 1)[:, :CA_WIN]
            first = (tl // CHUNK) * CHUNK
            ok = (col >= first) & (col < first + (LEFT_CHUNKS + 1) * CHUNK)
            o_ref[:, tl, :] = jnp.where(ok, v, NEG_BIG)
            return carry

        lax.fori_loop(0, CA_QBLK, row, 0, unroll=8)

    return _call(
        body, name=name,
        in_specs=[pl.BlockSpec(memory_space=pltpu.VMEM)], out_specs=pl.BlockSpec(memory_space=pltpu.VMEM),
        out_shape=jax.ShapeDtypeStruct((N_HEADS, CA_QBLK, CA_WIN), F32),
        compiler_params=_params(),
    )(rel_pad)


def bias_fold(db, name):
    def body(db_ref, o_ref):
        zeros = jnp.zeros((N_HEADS, CA_RING - CA_WIN), F32)

        def row(tl, acc):
            v = jnp.concatenate([db_ref[:, tl, :], zeros], axis=1)
            return acc + pltpu.roll(v, (CA_RING - tl) % CA_RING, 1)

        ring = lax.fori_loop(0, CA_QBLK, row, jnp.zeros((N_HEADS, CA_RING), F32))
        o_ref[...] = lax.dot_general(ring, _ring_onehot(), (((1,), (1,)), ((), ())),
                                     preferred_element_type=F32, precision=lax.Precision.HIGHEST)

    return _call(
        body, name=name,
        in_specs=[pl.BlockSpec(memory_space=pltpu.VMEM)], out_specs=pl.BlockSpec(memory_space=pltpu.VMEM),
        out_shape=jax.ShapeDtypeStruct((N_HEADS, N_REL_PAD), F32),
        compiler_params=_params(),
    )(db)


def _ca_logits(qh, kw, bias_h, first_key_ok):
    sc = _nt(qh, kw) + bias_h
    col = lax.broadcasted_iota(jnp.int32, sc.shape, 1)
    return jnp.where(col >= first_key_ok, sc, NEG_BIG)


def ca_fwd(qkn, qkv, table, name, rider=None):
    s = qkv.shape[0]
    tq = CA_QBLK

    def block(i, q_ref, k_ref, v_ref, b_ref, o_ref, kp_ref, vp_ref):

        @pl.when(i == 0)
        def _():
            kp_ref[0:CA_PAD, :] = jnp.zeros((CA_PAD, PAIR), BF16)
            vp_ref[0:CA_PAD, :] = jnp.zeros((CA_PAD, PAIR), BF16)
            kp_ref[CA_PAD:, :] = k_ref[...]
            vp_ref[CA_PAD:, :] = v_ref[...]

        lane = lax.broadcasted_iota(jnp.int32, (CA_SUB, PAIR), 1)
        hms = [lane < HEAD_DIM, lane >= HEAD_DIM]
        subs = range(tq // CA_SUB)
        rows = [slice(r * CA_SUB, (r + 1) * CA_SUB) for r in subs]
        cols = [slice(r * CA_SUB, r * CA_SUB + CA_SUBWIN) for r in subs]
        offs = [pl.multiple_of(i * tq + r * CA_SUB, CA_SUB) for r in subs]
        kws = [kp_ref[pl.ds(off, CA_SUBWIN), :] for off in offs]
        vws = [vp_ref[pl.ds(off, CA_SUBWIN), :] for off in offs]
        chains = [(r, h) for r in subs for h in range(2)]
        qhs = [jnp.where(hms[h], q_ref[rows[r], :], 0) * QK_SCALE for r, h in chains]
        scs = [_ca_logits(qhs[c], kws[r], b_ref[h, rows[r], cols[r]], CA_PAD - i * tq - r * CA_SUB)
               for c, (r, h) in enumerate(chains)]
        es = [jnp.exp(sc - jnp.max(sc, axis=1, keepdims=True)) for sc in scs]
        ps = [(e * (1.0 / jnp.sum(e, axis=1, keepdims=True))).astype(BF16) for e in es]
        outs = [_nn(ps[c], vws[r]) for c, (r, h) in enumerate(chains)]
        for r in subs:
            o_ref[rows[r], :] = jnp.where(lane < HEAD_DIM, outs[2 * r], outs[2 * r + 1]).astype(o_ref.dtype)

    def body(q_ref, k_ref, v_ref, b_ref, o_ref, *scratch):
        for sub in range(Q_SUBS):
            rows = pl.ds(sub * tq, tq)
            block(pl.program_id(1) * Q_SUBS + sub, q_ref.at[rows], k_ref, v_ref, b_ref, o_ref.at[rows], *scratch)

    def first_last():
        p, i = pl.program_id(0), pl.program_id(1)
        return (p == 0) & (i == 0), (p == N_PAIRS - 1) & (i == s // (Q_SUBS * tq) - 1)

    return hosted_call(
        body, first_last, rider, name=name, grid=(N_PAIRS, s // (Q_SUBS * tq)),
        in_specs=[
            pl.BlockSpec((Q_SUBS * tq, PAIR), lambda p, i: (i, p)),
            pl.BlockSpec((s, PAIR), lambda p, i: (0, N_PAIRS + p)),
            pl.BlockSpec((s, PAIR), lambda p, i: (0, 2 * N_PAIRS + p)),
            pl.BlockSpec((2, tq, CA_WIN), lambda p, i: (p, 0, 0)),
        ],
        out_specs=[pl.BlockSpec((Q_SUBS * tq, PAIR), lambda p, i: (i, p))],
        out_shape=[jax.ShapeDtypeStruct((s, D_MODEL), BF16)],
        scratch_shapes=[pltpu.VMEM((s + CA_PAD, PAIR), BF16), pltpu.VMEM((s + CA_PAD, PAIR), BF16)],
        args=(qkn, qkn, qkv, table),
        compiler_params=_params(("arbitrary", "arbitrary")),
    )


def ca_bwd(qkn, qkv, o, do, table, name, rider=None):
    s = qkv.shape[0]
    tq = CA_QBLK
    nq = s // tq

    def block(i, q_ref, k_ref, v_ref, o_ref, do_ref, b_ref, dq_ref, dk_ref, dv_ref, db_ref,
              kp_ref, vp_ref, dka_ref, dva_ref):

        @pl.when(i == 0)
        def _():
            kp_ref[0:CA_PAD, :] = jnp.zeros((CA_PAD, PAIR), BF16)
            vp_ref[0:CA_PAD, :] = jnp.zeros((CA_PAD, PAIR), BF16)
            kp_ref[CA_PAD:, :] = k_ref[...]
            vp_ref[CA_PAD:, :] = v_ref[...]
            dka_ref[...] = jnp.zeros_like(dka_ref)
            dva_ref[...] = jnp.zeros_like(dva_ref)
            db_ref[...] = jnp.zeros_like(db_ref)

        lane = lax.broadcasted_iota(jnp.int32, (CA_SUB, PAIR), 1)
        hms = [lane < HEAD_DIM, lane >= HEAD_DIM]
        subs = range(tq // CA_SUB)
        rows = [slice(r * CA_SUB, (r + 1) * CA_SUB) for r in subs]
        cols = [slice(r * CA_SUB, r * CA_SUB + CA_SUBWIN) for r in subs]
        offs = [pl.multiple_of(i * tq + r * CA_SUB, CA_SUB) for r in subs]
        kws = [kp_ref[pl.ds(off, CA_SUBWIN), :] for off in offs]
        vws = [vp_ref[pl.ds(off, CA_SUBWIN), :] for off in offs]
        dovs = [do_ref[rows[r], :] for r in subs]
        prods = [dovs[r].astype(F32) * o_ref[rows[r], :].astype(F32) for r in subs]
        chains = [(r, h) for r in subs for h in range(2)]
        qhs = [jnp.where(hms[h], q_ref[rows[r], :], 0) * QK_SCALE for r, h in chains]
        dohs = [jnp.where(hms[h], dovs[r], 0) for r, h in chains]
        deltas = [jnp.sum(jnp.where(hms[h], prods[r], 0.0), axis=1, keepdims=True) for r, h in chains]
        scs = [_ca_logits(qhs[c], kws[r], b_ref[h, rows[r], cols[r]], CA_PAD - i * tq - r * CA_SUB)
               for c, (r, h) in enumerate(chains)]
        dps = [_nt(dohs[c], vws[r]) for c, (r, h) in enumerate(chains)]
        es = [jnp.exp(sc - jnp.max(sc, axis=1, keepdims=True)) for sc in scs]
        ps = [e * (1.0 / jnp.sum(e, axis=1, keepdims=True)) for e in es]
        dss = [ps[c] * (dps[c] - deltas[c]) for c in range(len(chains))]
        for c, (r, h) in enumerate(chains):
            db_ref[h, rows[r], cols[r]] += dss[c]
        dsbs = [ds.astype(BF16) for ds in dss]
        pbs = [p.astype(BF16) for p in ps]
        dqs = [_nn(dsbs[c], kws[r]) for c, (r, h) in enumerate(chains)]
        for r in subs:
            dq_ref[rows[r], :] = (jnp.where(lane < HEAD_DIM, dqs[2 * r], dqs[2 * r + 1]) * QK_SCALE).astype(dq_ref.dtype)
        for r in subs:
            dka_ref[pl.ds(offs[r], CA_SUBWIN), :] += _tn(dsbs[2 * r], qhs[2 * r]) + _tn(dsbs[2 * r + 1], qhs[2 * r + 1])
            dva_ref[pl.ds(offs[r], CA_SUBWIN), :] += _tn(pbs[2 * r], dohs[2 * r]) + _tn(pbs[2 * r + 1], dohs[2 * r + 1])

        @pl.when(i == nq - 1)
        def _():
            dk_ref[...] = dka_ref[CA_PAD:, :].astype(dk_ref.dtype)
            dv_ref[...] = dva_ref[CA_PAD:, :].astype(dv_ref.dtype)

    qblk = pl.BlockSpec((Q_SUBS * tq, PAIR), lambda p, i: (i, p))
    full = pl.BlockSpec((s, PAIR), lambda p, i: (0, p))
    tblk = pl.BlockSpec((2, tq, CA_WIN), lambda p, i: (p, 0, 0))
    out = jax.ShapeDtypeStruct((s, D_MODEL), BF16)

    def body(q_ref, k_ref, v_ref, o_ref, do_ref, b_ref, dq_ref, dk_ref, dv_ref, db_ref, *scratch):
        for sub in range(Q_SUBS):
            rows = pl.ds(sub * tq, tq)
            block(pl.program_id(1) * Q_SUBS + sub, q_ref.at[rows], k_ref, v_ref, o_ref.at[rows], do_ref.at[rows], b_ref,
                  dq_ref.at[rows], dk_ref, dv_ref, db_ref, *scratch)

    def first_last():
        p, i = pl.program_id(0), pl.program_id(1)
        return (p == 0) & (i == 0), (p == N_PAIRS - 1) & (i == nq // Q_SUBS - 1)

    return hosted_call(
        body, first_last, rider, name=name, grid=(N_PAIRS, nq // Q_SUBS),
        in_specs=[
            qblk,
            pl.BlockSpec((s, PAIR), lambda p, i: (0, N_PAIRS + p)),
            pl.BlockSpec((s, PAIR), lambda p, i: (0, 2 * N_PAIRS + p)),
            qblk, qblk, tblk,
        ],
        out_specs=[qblk, full, full, tblk],
        out_shape=[out, out, out, jax.ShapeDtypeStruct((N_HEADS, tq, CA_WIN), F32)],
        scratch_shapes=[pltpu.VMEM((s + CA_PAD, PAIR), BF16), pltpu.VMEM((s + CA_PAD, PAIR), BF16),
                        pltpu.VMEM((s + CA_PAD, PAIR), F32), pltpu.VMEM((s + CA_PAD, PAIR), F32)],
        args=(qkn, qkn, qkv, o, do, table),
        compiler_params=_params(("arbitrary", "arbitrary")),
    )


def _me():
    x, y, c = lax.axis_index("x"), lax.axis_index("y"), lax.axis_index("c")
    return x, y, c, 4 * x + 2 * y + c


def _peer(k):
    x, y, c, _ = _me()
    px, py, pc = x ^ ((k >> 2) & 1), y ^ ((k >> 1) & 1), c ^ (k & 1)
    return (px, py, pc), 4 * px + 2 * py + pc


ANY = pl.BlockSpec(memory_space=pl.ANY)


class Exchange:
    def __init__(self, n, copy, in_arrays, out_shape):
        self.n, self.copy, self.in_arrays, self.out_shape = n, copy, list(in_arrays), list(out_shape)
        self.sems = [pltpu.SemaphoreType.DMA((n, N_DEV - 1)), pltpu.SemaphoreType.DMA((n, N_DEV - 1)),
                     pltpu.SemaphoreType.DMA((n,))]

    def _copies(self, data, sems):
        send_sems, recv_sems, local_sems = sems
        _, _, _, me = _me()
        local, sends, recvs = [], [], []
        for a in range(self.n):
            s_ref, d_ref = self.copy(data, a, me, me)
            local.append(pltpu.make_async_copy(s_ref, d_ref, local_sems.at[a]))
            for k in range(1, N_DEV):
                peer, pidx = _peer(k)
                s_ref, d_ref = self.copy(data, a, me, pidx)
                sends.append(pltpu.make_async_remote_copy(
                    src_ref=s_ref, dst_ref=d_ref, send_sem=send_sems.at[a, k - 1], recv_sem=recv_sems.at[a, k - 1],
                    device_id=peer, device_id_type=MESH))
                s_ref, d_ref = self.copy(data, a, pidx, me)
                recvs.append(pltpu.make_async_remote_copy(
                    src_ref=s_ref, dst_ref=d_ref, send_sem=send_sems.at[a, k - 1], recv_sem=recv_sems.at[a, k - 1],
                    device_id=peer, device_id_type=MESH))
        return local, sends, recvs

    def start(self, data, sems):
        local, sends, _ = self._copies(data, sems)
        for cp in local + sends:
            cp.start()

    def finish(self, data, sems):
        local, sends, recvs = self._copies(data, sems)
        for cp in recvs:
            cp.wait_recv()
        for cp in sends:
            cp.wait_send()
        for cp in local:
            cp.wait()


def run_exchange(ex, name):
    n_data = len(ex.in_arrays) + len(ex.out_shape)

    def body(*refs):
        ex.start(refs[:n_data], refs[n_data:])
        ex.finish(refs[:n_data], refs[n_data:])

    return _call(
        body, name=name,
        in_specs=[ANY] * len(ex.in_arrays), out_specs=[ANY] * len(ex.out_shape), out_shape=ex.out_shape,
        scratch_shapes=ex.sems, compiler_params=pltpu.CompilerParams(has_side_effects=True),
    )(*ex.in_arrays)


def hosted_call(core_body, first_last, rider, *, in_specs, out_specs, out_shape, scratch_shapes, args, **kw):
    if rider is None:
        return _call(core_body, in_specs=in_specs, out_specs=out_specs, out_shape=out_shape,
                     scratch_shapes=scratch_shapes, **kw)(*args), []
    n_in, n_out, n_scr = len(in_specs), len(out_specs), len(scratch_shapes)
    r_in, r_out = len(rider.in_arrays), len(rider.out_shape)

    def body(*refs):
        it = iter(refs)
        take = lambda m: [next(it) for _ in range(m)]
        c_in, x_in, c_out, x_out, c_scr, sems = take(n_in), take(r_in), take(n_out), take(r_out), take(n_scr), take(3)
        first, last = first_last()

        @pl.when(first)
        def _():
            rider.start(x_in + x_out, sems)

        core_body(*c_in, *c_out, *c_scr)

        @pl.when(last)
        def _():
            rider.finish(x_in + x_out, sems)

    outs = _call(
        body, in_specs=list(in_specs) + [ANY] * r_in, out_specs=list(out_specs) + [ANY] * r_out,
        out_shape=list(out_shape) + rider.out_shape, scratch_shapes=list(scratch_shapes) + rider.sems, **kw,
    )(*args, *rider.in_arrays)
    return outs[:n_out], outs[n_out:]


def _window(full_ref, shard_shape, idx):
    a, b = shard_shape
    if a == full_ref.shape[0]:
        return full_ref.at[:, pl.ds(pl.multiple_of(idx * b, 128), b)]
    return full_ref.at[pl.ds(pl.multiple_of(idx * a, 8), a), :]


def gather_exchange(shards, full_shapes, keys):
    nw = len(shards)

    def copy(data, a, src_idx, dst_idx):
        w, layer = keys[a]
        s_ref = data[w].at[layer]
        return s_ref, _window(data[nw + a], s_ref.shape, src_idx)

    out_shape = [jax.ShapeDtypeStruct(full_shapes[w], shards[w].dtype) for w, _ in keys]
    return Exchange(len(keys), copy, shards, out_shape)


def scatter_exchange(partials, shard_shapes):
    n = len(partials)

    def copy(data, a, src_idx, dst_idx):
        return _window(data[a], shard_shapes[a], dst_idx), data[n + a].at[src_idx]

    out_shape = [jax.ShapeDtypeStruct((N_DEV,) + tuple(ss), p.dtype) for ss, p in zip(shard_shapes, partials)]
    return Exchange(n, copy, partials, out_shape)


def _adamw(w, g, m, v):
    m = ADAM_B1 * m + (1.0 - ADAM_B1) * g
    v = ADAM_B2 * v + (1.0 - ADAM_B2) * (g * g)
    m_hat = m / (1.0 - ADAM_B1 ** ADAM_STEP)
    v_hat = v / (1.0 - ADAM_B2 ** ADAM_STEP)
    delta = -ADAM_LR * (m_hat / (jnp.sqrt(v_hat) + ADAM_EPS) + ADAM_WD * w)
    return delta, m, v


def adam_shard(parts, w, m, v, name):
    _, r, c = w.shape
    tr = min(r, 256)
    nr = r // tr

    def body(*refs):
        p_refs = refs[:DEPTH]
        w_ref, m_ref, v_ref, g_out, d_out, m_out, v_out = refs[DEPTH:]
        layer = pl.program_id(0)
        for li in range(DEPTH):
            @pl.when(layer == li)
            def _(p_ref=p_refs[li]):
                g = p_ref[0].astype(F32)
                for e in range(1, N_DEV):
                    g = g + p_ref[e].astype(F32)
                delta, mn, vn = _adamw(w_ref[...], g, m_ref[...], v_ref[...])
                g_out[...] = g
                d_out[...] = delta
                m_out[...] = mn
                v_out[...] = vn

    def part_spec(li):
        return pl.BlockSpec((N_DEV, tr, c), lambda layer, i: (0, jnp.where(layer == li, i, 0), 0))

    blk = pl.BlockSpec((None, tr, c), lambda layer, i: (layer, i, 0))
    out = jax.ShapeDtypeStruct(w.shape, F32)
    return _call(
        body, name=name, grid=(DEPTH, nr),
        in_specs=[part_spec(li) for li in range(DEPTH)] + [blk, blk, blk],
        out_specs=[blk] * 4, out_shape=[out] * 4,
        compiler_params=_params(("arbitrary", "arbitrary")),
    )(*parts, w, m, v)


def small_allreduce_adam(pk, w, m, v, name):
    def body(pk_ref, w_ref, m_ref, v_ref, g_out, d_out, m_out, v_out, all_ref, send_sems, recv_sems):
        _, _, _, me = _me()
        all_ref[me] = pk_ref[...]
        sends = []
        for k in range(1, N_DEV):
            peer, _ = _peer(k)
            cp = pltpu.make_async_remote_copy(
                src_ref=pk_ref, dst_ref=all_ref.at[me], send_sem=send_sems.at[k - 1],
                recv_sem=recv_sems.at[k - 1], device_id=peer, device_id_type=MESH)
            cp.start()
            sends.append(cp)
        for k in range(1, N_DEV):
            peer, pidx = _peer(k)
            pltpu.make_async_remote_copy(
                src_ref=pk_ref, dst_ref=all_ref.at[pidx], send_sem=send_sems.at[k - 1],
                recv_sem=recv_sems.at[k - 1], device_id=peer, device_id_type=MESH).wait_recv()
        for cp in sends:
            cp.wait_send()
        g = all_ref[0]
        for e in range(1, N_DEV):
            g = g + all_ref[e]
        delta, mn, vn = _adamw(w_ref[...], g, m_ref[...], v_ref[...])
        g_out[...] = g
        d_out[...] = delta
        m_out[...] = mn
        v_out[...] = vn

    vm = pl.BlockSpec(memory_space=pltpu.VMEM)
    out = jax.ShapeDtypeStruct((PACK_ROWS, 128), F32)
    return _call(
        body, name=name,
        in_specs=[vm] * 4, out_specs=[vm] * 4, out_shape=[out] * 4,
        scratch_shapes=[pltpu.VMEM((N_DEV, PACK_ROWS, 128), F32), pltpu.SemaphoreType.DMA((N_DEV - 1,)),
                        pltpu.SemaphoreType.DMA((N_DEV - 1,))],
        compiler_params=pltpu.CompilerParams(has_side_effects=True),
    )(pk, w, m, v)


def _pack_small(mix, ffn, qn, kn, rel, loss):
    flat = jnp.concatenate([mix.reshape(-1), ffn.reshape(-1), qn.reshape(-1), kn.reshape(-1), rel.reshape(-1),
                            loss.reshape(-1)])
    return jnp.pad(flat, (0, PACK_LEN - flat.shape[0])).reshape(PACK_ROWS, 128)


def _unpack_small(pk):
    flat = pk.reshape(-1)
    return (flat[OFF_MIX:OFF_FFN].reshape(DEPTH, D_MODEL), flat[OFF_FFN:OFF_QN].reshape(DEPTH, D_MODEL),
            flat[OFF_QN:OFF_KN].reshape(2, HEAD_DIM), flat[OFF_KN:OFF_REL].reshape(2, HEAD_DIM),
            flat[OFF_REL:OFF_LOSS].reshape(2, N_HEADS, N_REL), flat[OFF_LOSS])


def _pair_gain(g):
    return jnp.concatenate([g, g]).reshape(1, PAIR)


def kernel(x, mix_norm, w_qkv, w_o, q_norm, k_norm, rel_bias, ffn_norm, w_up, w_down, loss_target, m_mix_norm, m_w_qkv, m_w_o, m_q_norm, m_k_norm, m_rel_bias, m_ffn_norm, m_w_up, m_w_down, v_mix_norm, v_w_qkv, v_w_o, v_q_norm, v_k_norm, v_rel_bias, v_ffn_norm, v_w_up, v_w_down):
    x0 = x[0]
    target = loss_target[0]
    shard_shapes = [w_qkv.shape[1:], w_o.shape[1:], w_up.shape[1:], w_down.shape[1:]]
    full_shapes = [(D_MODEL, 3 * D_MODEL), (D_MODEL, D_MODEL), (D_MODEL, D_FF), (D_FF, D_MODEL)]
    shards = [w_qkv.astype(BF16), w_o.astype(BF16), w_up.astype(BF16), w_down.astype(BF16)]
    nw = len(shards)
    weight = {}

    def fetch(keys, host, *args):
        outs, arrived = host(*args, rider=gather_exchange(shards, full_shapes, keys))
        weight.update(zip(keys, arrived))
        return outs

    first = [(0, 0)]
    weight.update(zip(first, run_exchange(gather_exchange(shards, full_shapes, first), "gather_first")))
    riding = {0: [(w, 0) for w in range(1, nw)] + [(w, 1) for w in range(nw)],
              1: [(w, 2) for w in range(nw)], 2: [(w, 3) for w in range(nw)]}
    tri = _tri_mats()

    saved = []
    xin = x0
    h0 = rms_fwd(x0, mix_norm[0:1], "rms_mix_0")
    for layer in range(DEPTH):
        mixer_b = layer % 2 == 1
        idx = layer // 2
        qkv_raw = dense(h0, weight[0, layer], nt=False, name=f"qkv_{layer}", out_dtype=BF16, tm=1024)
        if mixer_b:
            gq, gk = _pair_gain(q_norm[idx]), _pair_gain(k_norm[idx])
            qkn = headnorm_fwd(qkv_raw, gq, gk, f"headnorm_{layer}")
            rel_pad = jnp.pad(rel_bias[idx], ((0, 0), (0, N_REL_PAD - N_REL)))
            table = bias_table(rel_pad, f"bias_table_{layer}")
            if layer in riding:
                (o,) = fetch(riding[layer], ca_fwd, qkn, qkv_raw, table, f"ca_fwd_{layer}")
            else:
                (o,), _ = ca_fwd(qkn, qkv_raw, table, f"ca_fwd_{layer}")
            attn_saved = (qkv_raw, qkn, table)
        else:
            o, ltot = fetch(riding[layer], sb_fwd, qkv_raw, tri, f"sb_fwd_{layer}")
            attn_saved = (qkv_raw, ltot)
        x1, h1 = dense(o, weight[1, layer], nt=False, name=f"attn_out_{layer}", out_dtype=F32, tm=1024, res=xin,
                       norm_gain=ffn_norm[layer:layer + 1])
        pre = dense(h1, weight[2, layer], nt=False, name=f"up_{layer}", out_dtype=BF16, tm=1024)
        saved.append((xin, h0, attn_saved, o, x1, h1, pre))
        if layer + 1 < DEPTH:
            xin, h0 = dense(pre, weight[3, layer], nt=False, name=f"down_{layer}", out_dtype=F32, tm=512, relu2_in=True,
                            res=x1, norm_gain=mix_norm[layer + 1:layer + 2])
        else:
            xin = dense(pre, weight[3, layer], nt=False, name=f"down_{layer}", out_dtype=F32, tm=512, relu2_in=True,
                        res=x1)

    dx, loss_blk = loss_head(xin, target, "loss_head")

    partial = {}
    received = {}
    d_mix, d_ffn = [None] * DEPTH, [None] * DEPTH
    d_qn, d_kn, d_rel = [None] * 2, [None] * 2, [None] * 2

    def leaving():
        keys = [key for key in sorted(partial) if key not in received]
        return keys, scatter_exchange([partial[key] for key in keys], [shard_shapes[key[0]] for key in keys])

    def send(host, *args, **kw):
        keys, ex = leaving()
        outs, arrived = host(*args, rider=ex, **kw)
        received.update(zip(keys, arrived))
        return outs

    for layer in reversed(range(DEPTH)):
        mixer_b = layer % 2 == 1
        idx = layer // 2
        xin, h0, attn_saved, o, x1, h1, pre = saved[layer]
        g_qkv, g_o, g_up, g_down = (weight[w, layer] for w in range(nw))
        dpre = dense(dx, g_down, nt=True, name=f"d_pre_{layer}", out_dtype=BF16, tm=512, relu2_grad=pre)
        partial[3, layer] = dense_tn(pre, dx, tk=2048, bn=1024, name=f"dw_down_{layer}", relu2_in=True)
        partial[2, layer] = dense_tn(h1, dpre, tk=1024, bn=2048, name=f"dw_up_{layer}")
        dx, d_ffn[layer] = dense(dpre, g_up, nt=True, name=f"d_h1_{layer}", out_dtype=F32, tm=512,
                                 rms_back=(x1, ffn_norm[layer:layer + 1], dx))
        do = dense(dx, g_o, nt=True, name=f"d_o_{layer}", out_dtype=BF16, tm=1024)
        partial[1, layer] = dense_tn(o, dx, tk=1024, bn=1024, name=f"dw_o_{layer}")
        if mixer_b:
            qkv_raw, qkn, table = attn_saved
            dq, dk, dv, dtab = send(ca_bwd, qkn, qkv_raw, o, do, table, f"ca_bwd_{layer}")
            gq, gk = _pair_gain(q_norm[idx]), _pair_gain(k_norm[idx])
            dqkv, dgain = headnorm_bwd(dq, dk, dv, qkv_raw, gq, gk, f"headnorm_bwd_{layer}")
            d_qn[idx] = dgain[0, :HEAD_DIM] + dgain[0, HEAD_DIM:]
            d_kn[idx] = dgain[1, :HEAD_DIM] + dgain[1, HEAD_DIM:]
            d_rel[idx] = bias_fold(dtab, f"bias_fold_{layer}")[:, :N_REL]
        else:
            qkv_raw, ltot = attn_saved
            dq, dk, dv = send(sb_bwd, qkv_raw, do, ltot, tri, f"sb_bwd_{layer}")
            dqkv = jnp.concatenate([dq, dk, dv], axis=1)
        partial[0, layer] = dense_tn(h0, dqkv, tk=1024, bn=1536, name=f"dw_qkv_{layer}")
        back = dict(nt=True, name=f"d_h0_{layer}", out_dtype=F32, tm=512, rms_back=(xin, mix_norm[layer:layer + 1], dx))
        if layer == 0:
            dx, dg = send(dense, dqkv, g_qkv, **back)
            d_mix[layer] = dg[0:1]
        else:
            dx, d_mix[layer] = dense(dqkv, g_qkv, **back)

    grad_x = dx[None]


    def big(wi, w, m, v, name):
        return adam_shard([received[wi, layer] for layer in range(DEPTH)], w, m, v, name)

    a_qkv = big(0, w_qkv, m_w_qkv, v_w_qkv, "adam_qkv")
    a_o = big(1, w_o, m_w_o, v_w_o, "adam_o")
    a_up = big(2, w_up, m_w_up, v_w_up, "adam_up")
    a_down = big(3, w_down, m_w_down, v_w_down, "adam_down")

    pk = _pack_small(jnp.concatenate(d_mix), jnp.concatenate(d_ffn), jnp.stack(d_qn), jnp.stack(d_kn),
                     jnp.stack(d_rel), loss_blk[0, 0])
    zero = jnp.zeros((), F32)
    pw = _pack_small(mix_norm, ffn_norm, q_norm, k_norm, rel_bias, zero)
    pm = _pack_small(m_mix_norm, m_ffn_norm, m_q_norm, m_k_norm, m_rel_bias, zero)
    pv = _pack_small(v_mix_norm, v_ffn_norm, v_q_norm, v_k_norm, v_rel_bias, zero)
    s_g, s_d, s_m, s_v = small_allreduce_adam(pk, pw, pm, pv, "small_allreduce_adam")
    g_mix, g_ffn, g_qn, g_kn, g_rel, loss = _unpack_small(s_g)
    dl_mix, dl_ffn, dl_qn, dl_kn, dl_rel, _ = _unpack_small(s_d)
    nm_mix, nm_ffn, nm_qn, nm_kn, nm_rel, _ = _unpack_small(s_m)
    nv_mix, nv_ffn, nv_qn, nv_kn, nv_rel, _ = _unpack_small(s_v)

    return (loss, grad_x,
            g_mix, a_qkv[0], a_o[0], g_qn, g_kn, g_rel, g_ffn, a_up[0], a_down[0],
            dl_mix, a_qkv[1], a_o[1], dl_qn, dl_kn, dl_rel, dl_ffn, a_up[1], a_down[1],
            nm_mix, a_qkv[2], a_o[2], nm_qn, nm_kn, nm_rel, nm_ffn, a_up[2], a_down[2],
            nv_mix, a_qkv[3], a_o[3], nv_qn, nv_kn, nv_rel, nv_ffn, a_up[3], a_down[3])
```
